```python
import jax, jax.numpy as jnp
from jax import lax
import numpy as np

D_MODEL = 1024
BATCH = 8
SEQ = 4096
DEPTH = 4

PLE_DIM = 256
N_BRANCHES = 3
POOL_WINDOWS = (2, 4, 8, 16)
POOL_GROUPS = 4
POOL_WIDTH = D_MODEL
POOL_GROUP_DIM = POOL_WIDTH // POOL_GROUPS
ATTN_GROUPS = ((128, 1), (512, 4), (2048, 16))
ATTN_HEADS_PER_GROUP = 4
ATTN_HEADS = ATTN_HEADS_PER_GROUP * len(ATTN_GROUPS)
ATTN_HEAD_DIM = 128
ATTN_WIDTH = ATTN_HEADS * ATTN_HEAD_DIM
ATTN_OUT_WIDTH = ATTN_HEADS_PER_GROUP * ATTN_HEAD_DIM
ALIBI_MAX_BIAS = 8.0
NEG_INF = -1e30
HGRN_HEADS = 8
HGRN_HEAD_DIM = 128
HGRN_WIDTH = HGRN_HEADS * HGRN_HEAD_DIM
HGRN_CHUNK = 32
FFN_HIDDEN = ((8 * D_MODEL + 3 * 256 - 1) // (3 * 256)) * 256
IN_SIZES = (POOL_WIDTH, ATTN_WIDTH, ATTN_WIDTH, ATTN_WIDTH,
            HGRN_WIDTH, HGRN_WIDTH, HGRN_WIDTH, HGRN_WIDTH, HGRN_WIDTH,
            N_BRANCHES * D_MODEL)
IN_WIDTH = sum(IN_SIZES)
DEEPNORM_ALPHA = (2 * DEPTH) ** 0.25
DEEPNORM_BETA = (8 * DEPTH) ** -0.25
LN_EPS = 1e-5
RMS_EPS = 1e-6

kernel_name = "hybrid_pool_dilattn_hgrn2_encoder"


def layer_norm(x, g, b):
    xf = x.astype(jnp.float32)
    mu = jnp.mean(xf, axis=-1, keepdims=True)
    var = jnp.mean(jnp.square(xf - mu), axis=-1, keepdims=True)
    y = (xf - mu) * lax.rsqrt(var + LN_EPS)
    return (y * g.astype(jnp.float32) + b.astype(jnp.float32)).astype(x.dtype)


def multiscale_pool(a, pool_w, pool_scale):
    B, S, _ = a.shape
    af = a.astype(jnp.float32).reshape(B, S, POOL_GROUPS, POOL_GROUP_DIM)
    csum = jnp.concatenate([jnp.zeros_like(af[:, :1]), jnp.cumsum(af, axis=1)], axis=1)
    pos = jnp.arange(S, dtype=jnp.int32)[:, None]
    half = jnp.asarray(POOL_WINDOWS, jnp.int32)[None, :] // 2
    lo = jnp.clip(pos - half, 0, S)
    hi = jnp.clip(pos + half, 0, S)
    grp = jnp.arange(POOL_GROUPS, dtype=jnp.int32)[None, :]
    wsum = csum[:, hi, grp] - csum[:, lo, grp]
    count = (hi - lo).astype(jnp.float32)[None, :, :, None]
    mixed = wsum / count - af
    y = jnp.einsum('bsgc,gcd->bsgd', mixed, pool_w.astype(jnp.float32))
    return (y.reshape(B, S, POOL_WIDTH) * pool_scale.astype(jnp.float32)).astype(a.dtype)


def dilated_band_attention(q, k, v, dilation, side, slopes):
    B, S, H, Dh = q.shape
    L = S // dilation
    nb = -(-L // side)
    Lp = nb * side

    def to_sub(t):
        return t.astype(jnp.float32).reshape(B, L, dilation, H, Dh).transpose(0, 2, 3, 1, 4)

    qs = jnp.pad(to_sub(q), ((0, 0), (0, 0), (0, 0), (0, Lp - L), (0, 0)))
    kv_pad = ((0, 0), (0, 0), (0, 0), (side, Lp - L + side), (0, 0))
    ks = jnp.pad(to_sub(k), kv_pad)
    vs = jnp.pad(to_sub(v), kv_pad)
    qb = qs.reshape(B, dilation, H, nb, side, Dh)

    def windows(t):
        tb = t.reshape(B, dilation, H, nb + 2, side, Dh)
        return jnp.concatenate([tb[:, :, :, :-2], tb[:, :, :, 1:-1], tb[:, :, :, 2:]], axis=4)

    kw, vw = windows(ks), windows(vs)
    scores = jnp.einsum('brhnqe,brhnke->brhnqk', qb, kw) * (Dh ** -0.5)
    r = jnp.arange(side, dtype=jnp.int32)[:, None]
    c = jnp.arange(3 * side, dtype=jnp.int32)[None, :]
    off = c - side - r
    kj = jnp.arange(nb, dtype=jnp.int32)[:, None, None] * side - side + c[None]
    valid = (jnp.abs(off)[None] <= side) & (kj >= 0) & (kj < L)
    dist = (jnp.abs(off) * dilation).astype(jnp.float32)
    bias = -slopes.astype(jnp.float32)[:, None, None, None] * dist[None, None]
    scores = jnp.where(valid, scores + bias, NEG_INF)
    lse = jax.nn.logsumexp(scores, axis=-1)
    probs = jnp.exp(scores - lse[..., None])
    o = jnp.einsum('brhnqk,brhnke->brhnqe', probs, vw)
    o = o.reshape(B, dilation, H, Lp, Dh)[:, :, :, :L].transpose(0, 3, 1, 2, 4).reshape(B, S, H, Dh)
    lse = lse.reshape(B, dilation, H, Lp)[..., :L].transpose(0, 3, 1, 2).reshape(B, S, H)
    return o, lse


def dilated_attention_mixer(q_raw, k_raw, v_raw):
    B, S, _ = q_raw.shape
    shp = (B, S, ATTN_HEADS, ATTN_HEAD_DIM)
    q, k, v = q_raw.reshape(shp), k_raw.reshape(shp), v_raw.reshape(shp)
    slopes = 2.0 ** (-ALIBI_MAX_BIAS * jnp.arange(1, ATTN_HEADS + 1, dtype=jnp.float32) / ATTN_HEADS)
    outs, lses = [], []
    for g, (window, dilation) in enumerate(ATTN_GROUPS):
        sl = slice(g * ATTN_HEADS_PER_GROUP, (g + 1) * ATTN_HEADS_PER_GROUP)
        o, l = dilated_band_attention(q[:, :, sl], k[:, :, sl], v[:, :, sl],
                                      dilation, window // (2 * dilation), slopes[sl])
        outs.append(o)
        lses.append(l)
    w = jax.nn.softmax(jnp.stack(lses, axis=0), axis=0)
    o = jnp.sum(w[..., None] * jnp.stack(outs, axis=0), axis=0)
    return o.reshape(B, S, ATTN_OUT_WIDTH).astype(q_raw.dtype)


def hgrn2_scan(q, k, v, log_f):
    B, S, H, Dk = q.shape
    Dv = v.shape[-1]
    C = HGRN_CHUNK
    N = S // C

    def chunks(t):
        return t.reshape(B, N, C, H, t.shape[-1]).transpose(1, 0, 3, 2, 4)

    qc, kc, vc, fc = chunks(q), chunks(k), chunks(v), chunks(log_f)
    b = jnp.cumsum(fc, axis=3)
    b_end = b[:, :, :, -1:]
    qe = qc * jnp.exp(b)
    kd = kc * jnp.exp(b_end - b)
    att = jnp.einsum('nbhte,nbhse->nbhts', qe, kc * jnp.exp(-b))
    lower = jnp.tril(jnp.ones((C, C), dtype=bool))
    o_intra = jnp.einsum('nbhts,nbhsv->nbhtv', jnp.where(lower, att, 0.0), vc)
    decay = jnp.exp(b_end[:, :, :, 0])

    def step(state, xs):
        qe_n, kd_n, v_n, decay_n = xs
        o_n = jnp.einsum('bhte,bhev->bhtv', qe_n, state)
        state = decay_n[..., None] * state + jnp.einsum('bhse,bhsv->bhev', kd_n, v_n)
        return state, o_n

    init = jnp.zeros((B, H, Dk, Dv), jnp.float32)
    _, o_inter = lax.scan(step, init, (qe, kd, vc, decay))
    o = o_intra + o_inter
    return o.transpose(1, 0, 3, 2, 4).reshape(B, S, H, Dv)


def hgrn2_bidirectional(q_raw, i_raw, f_fwd_raw, f_bwd_raw, g_raw, lb_fwd, lb_bwd, norm_w):
    B, S, _ = q_raw.shape

    def heads(t):
        return t.astype(jnp.float32).reshape(B, S, HGRN_HEADS, HGRN_HEAD_DIM)

    q = jax.nn.silu(heads(q_raw))
    v = heads(i_raw)

    def forget(raw, lb):
        lb = lb.reshape(HGRN_HEADS, HGRN_HEAD_DIM)
        z = heads(raw)
        f = lb + (1.0 - lb) * jax.nn.sigmoid(z)
        return (1.0 - lb) * jax.nn.sigmoid(-z), jnp.log(f)

    k_f, logf_f = forget(f_fwd_raw, lb_fwd)
    k_b, logf_b = forget(f_bwd_raw, lb_bwd)
    flip = lambda t: jnp.flip(t, axis=1)
    o = hgrn2_scan(q, k_f, v, logf_f) + flip(hgrn2_scan(flip(q), flip(k_b), flip(v), flip(logf_b)))
    o = o * lax.rsqrt(jnp.mean(jnp.square(o), axis=-1, keepdims=True) + RMS_EPS)
    o = o * norm_w.astype(jnp.float32).reshape(HGRN_HEADS, HGRN_HEAD_DIM) * jax.nn.silu(heads(g_raw))
    return o.reshape(B, S, HGRN_WIDTH).astype(q_raw.dtype)


def setup_inputs(seed: int = 0) -> dict:
    key = jax.random.key(seed)
    ks = jax.random.split(key, 20)
    f32 = jnp.float32

    def dense(k, shape, fan_in, scale=1.0):
        return jax.random.normal(k, shape, f32) * (scale * fan_in ** -0.5)

    def gain(k, shape):
        return 1.0 + 0.02 * jax.random.normal(k, shape, f32)

    def small(k, shape):
        return 0.02 * jax.random.normal(k, shape, f32)

    return {
        "x": jax.random.normal(ks[0], (BATCH, SEQ, D_MODEL), f32),
        "p": jax.random.normal(ks[1], (DEPTH, BATCH, SEQ, PLE_DIM), f32),
        "w_in": dense(ks[2], (DEPTH, D_MODEL, IN_WIDTH), D_MODEL),
        "pool_w": dense(ks[3], (DEPTH, POOL_GROUPS, POOL_GROUP_DIM, POOL_GROUP_DIM), POOL_GROUP_DIM),
        "pool_scale": gain(ks[4], (DEPTH, POOL_WIDTH)),
        "w_branch_a": dense(ks[5], (DEPTH, POOL_WIDTH, D_MODEL), POOL_WIDTH),
        "w_branch_b": dense(ks[6], (DEPTH, ATTN_OUT_WIDTH, D_MODEL), ATTN_OUT_WIDTH),
        "w_branch_c": dense(ks[7], (DEPTH, HGRN_WIDTH, D_MODEL), HGRN_WIDTH),
        "hgrn_lb_logits": 0.5 * jax.random.normal(ks[8], (DEPTH, 2 * HGRN_WIDTH), f32),
        "hgrn_norm_w": gain(ks[9], (DEPTH, HGRN_WIDTH)),
        "w_out": dense(ks[10], (DEPTH, D_MODEL, D_MODEL), D_MODEL, DEEPNORM_BETA),
        "ln1_g": gain(ks[11], (DEPTH, D_MODEL)),
        "ln1_b": small(ks[12], (DEPTH, D_MODEL)),
        "w_ffn_gate": dense(ks[13], (DEPTH, D_MODEL, FFN_HIDDEN), D_MODEL),
        "w_ffn_up": dense(ks[14], (DEPTH, D_MODEL, FFN_HIDDEN), D_MODEL),
        "w_ffn_down": dense(ks[15], (DEPTH, FFN_HIDDEN, D_MODEL), FFN_HIDDEN, DEEPNORM_BETA),
        "w_ple_proj": dense(ks[16], (DEPTH, PLE_DIM, D_MODEL), PLE_DIM, DEEPNORM_BETA),
        "w_ple_gate": dense(ks[17], (DEPTH, D_MODEL, D_MODEL), D_MODEL),
        "ln2_g": gain(ks[18], (DEPTH, D_MODEL)),
        "ln2_b": small(ks[19], (DEPTH, D_MODEL)),
    }


def reference(x, p, w_in, pool_w, pool_scale, w_branch_a, w_branch_b, w_branch_c,
              hgrn_lb_logits, hgrn_norm_w, w_out, ln1_g, ln1_b, w_ffn_gate, w_ffn_up,
              w_ffn_down, w_ple_proj, w_ple_gate, ln2_g, ln2_b):
    B, S, _ = x.shape
    lb = jnp.cumsum(jax.nn.softmax(hgrn_lb_logits.astype(jnp.float32), axis=0), axis=0)
    lb = lb - lb[:1]
    splits = np.cumsum(IN_SIZES)[:-1].tolist()
    for i in range(DEPTH):
        proj = jnp.einsum('bsd,de->bse', x, w_in[i])
        (a_in, q_att, k_att, v_att, q_h, i_h, ff_h, fb_h, g_h, gate_raw) = jnp.split(proj, splits, axis=-1)
        y_a = jnp.einsum('bsc,cd->bsd', multiscale_pool(a_in, pool_w[i], pool_scale[i]), w_branch_a[i])
        y_b = jnp.einsum('bsc,cd->bsd', dilated_attention_mixer(q_att, k_att, v_att), w_branch_b[i])
        y_c = jnp.einsum('bsc,cd->bsd',
                         hgrn2_bidirectional(q_h, i_h, ff_h, fb_h, g_h,
                                             lb[i, :HGRN_WIDTH], lb[i, HGRN_WIDTH:], hgrn_norm_w[i]),
                         w_branch_c[i])
        gates = jax.nn.sigmoid(gate_raw.astype(jnp.float32)).reshape(B, S, N_BRANCHES, D_MODEL)
        merged = (gates[:, :, 0] * y_a + gates[:, :, 1] * y_b + gates[:, :, 2] * y_c).astype(x.dtype)
        mix = jnp.einsum('bsd,de->bse', merged, w_out[i])
        x1 = layer_norm(DEEPNORM_ALPHA * x + mix, ln1_g[i], ln1_b[i])
        hidden = jax.nn.silu(jnp.einsum('bsd,df->bsf', x1, w_ffn_gate[i])) * jnp.einsum('bsd,df->bsf', x1, w_ffn_up[i])
        ffn = jnp.einsum('bsf,fd->bsd', hidden, w_ffn_down[i])
        ple = jnp.einsum('bsc,cd->bsd', p[i], w_ple_proj[i]) * jax.nn.sigmoid(jnp.einsum('bsd,de->bse', x1, w_ple_gate[i]))
        x = layer_norm(DEEPNORM_ALPHA * x1 + ffn + ple, ln2_g[i], ln2_b[i])
    return x
```

```python
import functools

import jax
import jax.numpy as jnp
from jax import lax
from jax.experimental import pallas as pl
from jax.experimental.pallas import tpu as pltpu

F32 = jnp.float32
BF16 = jnp.bfloat16

D_MODEL = 1024
PLE_DIM = 256
POOL_WINDOWS = (2, 4, 8, 16)
POOL_GROUPS = 4
POOL_GROUP_DIM = D_MODEL // POOL_GROUPS
ATTN_GROUPS = ((128, 1), (512, 4), (2048, 16))
ATTN_HEADS_PER_GROUP = 4
ATTN_HEADS = ATTN_HEADS_PER_GROUP * len(ATTN_GROUPS)
ATTN_HEAD_DIM = 128
ATTN_GROUP_WIDTH = ATTN_HEADS_PER_GROUP * ATTN_HEAD_DIM
ATTN_SIDE = 64
ALIBI_MAX_BIAS = 8.0
NEG_INF = -1e30
HGRN_HEADS = 8
HGRN_HEAD_DIM = 128
HGRN_WIDTH = HGRN_HEADS * HGRN_HEAD_DIM
FFN_HIDDEN = 2816
LN_EPS = 1e-5
RMS_EPS = 1e-6

COL_POOL = 0
COL_HQ = 1024
COL_HI = 2048
COL_HFF = 3072
COL_HFB = 4096
COL_HG = 5120
COL_GATES = 6144
COL_AQ = 9216
COL_AK = 9728
COL_AV = 10240
MAIN_WIDTH = 10752
QKV_WIDTH = 3 * ATTN_GROUP_WIDTH

ATTN_QUERY_BLOCK = 128
ATTN_KEY_WINDOW = ATTN_QUERY_BLOCK + 2 * ATTN_SIDE
HGRN_BLOCK = 64
HGRN_HEADS_PER_STEP = 2
POOL_HALO = 16
VMEM_LIMIT = 52 * 1024 * 1024


def _params(*semantics):
    return pltpu.CompilerParams(dimension_semantics=semantics, vmem_limit_bytes=VMEM_LIMIT)


def _sigmoid(x):
    return 1.0 / (1.0 + jnp.exp(-x))


def _layer_norm(h, g, b):
    mu = jnp.mean(h, axis=-1, keepdims=True)
    d = h - mu
    var = jnp.mean(d * d, axis=-1, keepdims=True)
    return d * lax.rsqrt(var + LN_EPS) * g + b


def _matmul_kernel(x_ref, w_ref, o_ref):
    o_ref[...] = jnp.dot(x_ref[...], w_ref[...], preferred_element_type=F32).astype(o_ref.dtype)


def _inproj_main(x_bf, w):
    T, D = x_bf.shape
    N = w.shape[1]
    tm = min(T, 1024)
    tn = 1536
    return pl.pallas_call(
        _matmul_kernel,
        grid=(N // tn, T // tm),
        in_specs=[pl.BlockSpec((tm, D), lambda j, i: (i, 0)),
                  pl.BlockSpec((D, tn), lambda j, i: (0, j))],
        out_specs=pl.BlockSpec((tm, tn), lambda j, i: (i, j)),
        out_shape=jax.ShapeDtypeStruct((T, N), BF16),
        compiler_params=_params("parallel", "parallel"),
    )(x_bf, w)


def _inproj_dilated_kernel(x_ref, w_ref, o_ref, *, residues):
    for r in range(residues):
        xr = x_ref[0, :, r * D_MODEL:(r + 1) * D_MODEL]
        o_ref[0, r] = jnp.dot(xr, w_ref[...], preferred_element_type=F32).astype(o_ref.dtype)


def _inproj_dilated(x_bf, w, batch, seq, dilation):
    L = seq // dilation
    N = w.shape[1]
    xv = x_bf.reshape(batch, L, dilation * D_MODEL)
    residues = 4
    tl = min(L, 512)
    kern = functools.partial(_inproj_dilated_kernel, residues=residues)
    return pl.pallas_call(
        kern,
        grid=(batch, dilation // residues, L // tl),
        in_specs=[pl.BlockSpec((1, tl, residues * D_MODEL), lambda b, r, j: (b, j, r)),
                  pl.BlockSpec((D_MODEL, N), lambda b, r, j: (0, 0))],
        out_specs=pl.BlockSpec((1, residues, tl, N), lambda b, r, j: (b, r, j, 0)),
        out_shape=jax.ShapeDtypeStruct((batch, dilation, L, N), BF16),
        compiler_params=_params("parallel", "parallel", "parallel"),
    )(xv, w)


def _attn_kernel(q_ref, k_ref, v_ref, o_ref, l_ref, *, seq_len, dilation, slopes):
    c = pl.program_id(1)
    rows_per_step = q_ref.shape[1]
    qb, kw = ATTN_QUERY_BLOCK, ATTN_KEY_WINDOW
    scale = ATTN_HEAD_DIM ** -0.5
    row = lax.broadcasted_iota(jnp.int32, (qb, kw), 0)
    col = lax.broadcasted_iota(jnp.int32, (qb, kw), 1)
    col_minus_row = col - row

    def body(j, carry):
        r0 = pl.multiple_of(j * qb, qb)
        t0 = c * rows_per_step + r0
        ks = pl.multiple_of(jnp.clip(t0 - ATTN_SIDE, 0, seq_len - kw), ATTN_SIDE)
        dist = jnp.abs(col_minus_row + (ks - t0))
        valid = dist <= ATTN_SIDE
        distf = dist.astype(F32) * float(dilation)
        for h in range(ATTN_HEADS_PER_GROUP):
            lanes = slice(h * ATTN_HEAD_DIM, (h + 1) * ATTN_HEAD_DIM)
            q = q_ref[0, pl.ds(r0, qb), lanes]
            kk = k_ref[0, pl.ds(ks, kw), lanes]
            vv = v_ref[0, pl.ds(ks, kw), lanes]
            s = lax.dot_general(q, kk, (((1,), (1,)), ((), ())), preferred_element_type=F32)
            s = jnp.where(valid, s * scale - slopes[h] * distf, NEG_INF)
            m = jnp.max(s, axis=1, keepdims=True)
            p = jnp.exp(s - m)
            l = jnp.sum(p, axis=1, keepdims=True)
            o = jnp.dot(p.astype(BF16), vv, preferred_element_type=F32) / l
            o_ref[0, pl.ds(r0, qb), lanes] = o.astype(o_ref.dtype)
            l_ref[0, pl.ds(r0, qb), lanes] = jnp.broadcast_to(m + jnp.log(l), (qb, ATTN_HEAD_DIM))
        return carry

    lax.fori_loop(0, rows_per_step // qb, body, 0)


def _attention_group(q_arr, k_arr, v_arr, cols, batch, seq, dilation, slopes):
    L = seq // dilation
    lc = min(L, 512)
    gw = ATTN_GROUP_WIDTH
    kern = functools.partial(_attn_kernel, seq_len=L, dilation=dilation, slopes=slopes)
    qc, kc, vc = cols
    out_map = lambda n, c: (n // dilation, c, n % dilation)
    o, lse = pl.pallas_call(
        kern,
        grid=(batch * dilation, L // lc),
        in_specs=[pl.BlockSpec((1, lc, gw), lambda n, c: (n, c, qc)),
                  pl.BlockSpec((1, L, gw), lambda n, c: (n, 0, kc)),
                  pl.BlockSpec((1, L, gw), lambda n, c: (n, 0, vc))],
        out_specs=[pl.BlockSpec((1, lc, gw), out_map),
                   pl.BlockSpec((1, lc, gw), out_map)],
        out_shape=[jax.ShapeDtypeStruct((batch, L, dilation * gw), BF16),
                   jax.ShapeDtypeStruct((batch, L, dilation * gw), F32)],
        compiler_params=_params("parallel", "arbitrary"),
    )(q_arr, k_arr, v_arr)
    return o.reshape(batch * seq, gw), lse.reshape(batch * seq, gw)


def _prefix_sum_rows(x, row_index):
    n = x.shape[0]
    shift = 1
    while shift < n:
        x = x + jnp.where(row_index >= shift, pltpu.roll(x, shift, 0), 0.0)
        shift *= 2
    return x


def _hgrn_kernel(q_ref, i_ref, ff_ref, fb_ref, g_ref, lbf_ref, lbb_ref, nw_ref, o_ref,
                 accf_ref, accb_ref, sf_ref, sb_ref):
    seq = q_ref.shape[1]
    hp = HGRN_HEADS_PER_STEP
    blk = HGRN_BLOCK
    half = blk // 2
    nblk = seq // blk
    dh = HGRN_HEAD_DIM
    sf_ref[...] = jnp.zeros_like(sf_ref)
    sb_ref[...] = jnp.zeros_like(sb_ref)
    row_index = lax.broadcasted_iota(jnp.int32, (blk, dh), 0)
    ti = lax.broadcasted_iota(jnp.int32, (blk, blk), 0)
    si = lax.broadcasted_iota(jnp.int32, (blk, blk), 1)
    causal = si <= ti
    anticausal = si >= ti

    def one_direction(h, r0, z_ref, lb_ref, s_ref, acc_ref, backward):
        lanes = slice(h * dh, (h + 1) * dh)
        rows = pl.ds(r0, blk)
        qraw = q_ref[0, rows, lanes].astype(F32)
        q = qraw * _sigmoid(qraw)
        v = i_ref[0, rows, lanes]
        z = z_ref[0, rows, lanes].astype(F32)
        lb = lb_ref[:, lanes]
        f = lb + (1.0 - lb) * _sigmoid(z)
        logf = jnp.log(f)
        k = (1.0 - lb) * _sigmoid(-z)
        prefix = _prefix_sum_rows(logf, row_index)
        total = prefix[blk - 1:blk, :]
        if backward:
            a = total - prefix + logf
            mid = a[half:half + 1, :]
            mask = anticausal
        else:
            a = prefix
            mid = a[half - 1:half, :]
            mask = causal
        q_mid = (q * jnp.exp(a - mid)).astype(BF16)
        k_mid = (k * jnp.exp(mid - a)).astype(BF16)
        att = lax.dot_general(q_mid, k_mid, (((1,), (1,)), ((), ())), preferred_element_type=F32)
        att = jnp.where(mask, att, 0.0).astype(BF16)
        state_t = s_ref[h]
        q_in = (q * jnp.exp(a)).astype(BF16)
        o = jnp.dot(att, v, preferred_element_type=F32)
        o = o + lax.dot_general(q_in, state_t.astype(BF16), (((1,), (1,)), ((), ())),
                                preferred_element_type=F32)
        acc_ref[rows, lanes] = o
        k_out = (k * jnp.exp(total - a)).astype(BF16)
        upd = lax.dot_general(v, k_out, (((0,), (0,)), ((), ())), preferred_element_type=F32)
        s_ref[h] = state_t * jnp.exp(total) + upd

    def body(n, carry):
        rf = pl.multiple_of(n * blk, blk)
        rb = pl.multiple_of((nblk - 1 - n) * blk, blk)
        for h in range(hp):
            one_direction(h, rf, ff_ref, lbf_ref, sf_ref, accf_ref, False)
            one_direction(h, rb, fb_ref, lbb_ref, sb_ref, accb_ref, True)
        return carry

    lax.fori_loop(0, nblk, body, 0)

    fin = min(seq, 512)

    def finish(t, carry):
        rows = pl.ds(pl.multiple_of(t * fin, fin), fin)
        for h in range(hp):
            lanes = slice(h * dh, (h + 1) * dh)
            o = accf_ref[rows, lanes] + accb_ref[rows, lanes]
            o = o * lax.rsqrt(jnp.mean(o * o, axis=-1, keepdims=True) + RMS_EPS)
            g = g_ref[0, rows, lanes].astype(F32)
            o = o * nw_ref[:, lanes] * (g * _sigmoid(g))
            o_ref[0, rows, lanes] = o.astype(o_ref.dtype)
        return carry

    lax.fori_loop(0, seq // fin, finish, 0)


def _hgrn(proj3, lb_fwd, lb_bwd, norm_w):
    batch, seq, _ = proj3.shape
    hp = HGRN_HEADS_PER_STEP
    w = hp * HGRN_HEAD_DIM

    def col(offset):
        base = offset // w
        return pl.BlockSpec((1, seq, w), lambda b, h: (b, 0, base + h))

    vec = pl.BlockSpec((1, w), lambda b, h: (0, h))
    return pl.pallas_call(
        _hgrn_kernel,
        grid=(batch, HGRN_HEADS // hp),
        in_specs=[col(COL_HQ), col(COL_HI), col(COL_HFF), col(COL_HFB), col(COL_HG), vec, vec, vec],
        out_specs=pl.BlockSpec((1, seq, w), lambda b, h: (b, 0, h)),
        out_shape=jax.ShapeDtypeStruct((batch, seq, HGRN_WIDTH), BF16),
        scratch_shapes=[pltpu.VMEM((seq, w), F32), pltpu.VMEM((seq, w), F32),
                        pltpu.VMEM((hp, HGRN_HEAD_DIM, HGRN_HEAD_DIM), F32),
                        pltpu.VMEM((hp, HGRN_HEAD_DIM, HGRN_HEAD_DIM), F32)],
        compiler_params=_params("parallel", "parallel"),
    )(proj3, proj3, proj3, proj3, proj3, lb_fwd, lb_bwd, norm_w)


def _pool_kernel(prev_ref, cur_ref, next_ref, pw_ref, ps_ref, o_ref, *, seq_len):
    i = pl.program_id(1)
    last = pl.num_programs(1) - 1
    ts = cur_ref.shape[1]
    n = ts + 2 * POOL_HALO
    pos = i * ts + lax.broadcasted_iota(jnp.int32, (ts, 1), 0)
    for g in range(POOL_GROUPS):
        lanes = slice(g * POOL_GROUP_DIM, (g + 1) * POOL_GROUP_DIM)
        half = POOL_WINDOWS[g] // 2
        cur = cur_ref[0, :, lanes].astype(F32)
        prev = jnp.where(i > 0, prev_ref[0, :, lanes].astype(F32), 0.0)
        nxt = jnp.where(i < last, next_ref[0, :, lanes].astype(F32), 0.0)
        ext = jnp.concatenate([prev, cur, nxt], axis=0)
        w = ext + pltpu.roll(ext, 1, 0)
        step = 1
        while step < half:
            w = pltpu.roll(w, step, 0) + pltpu.roll(w, n - step, 0)
            step *= 2
        wsum = w[POOL_HALO:POOL_HALO + ts]
        count = (jnp.minimum(pos + half, seq_len) - jnp.maximum(pos - half, 0)).astype(F32)
        mixed = wsum / count - cur
        y = jnp.dot(mixed.astype(BF16), pw_ref[g], preferred_element_type=F32)
        o_ref[0, :, lanes] = (y * ps_ref[:, lanes]).astype(o_ref.dtype)


def _pool(proj3, pool_w, pool_scale):
    batch, seq, _ = proj3.shape
    ts = min(seq, 512)
    hb = ts // POOL_HALO
    nhalo = seq // POOL_HALO
    kern = functools.partial(_pool_kernel, seq_len=seq)
    return pl.pallas_call(
        kern,
        grid=(batch, seq // ts),
        in_specs=[pl.BlockSpec((1, POOL_HALO, D_MODEL), lambda b, i: (b, jnp.maximum(i * hb - 1, 0), 0)),
                  pl.BlockSpec((1, ts, D_MODEL), lambda b, i: (b, i, 0)),
                  pl.BlockSpec((1, POOL_HALO, D_MODEL), lambda b, i: (b, jnp.minimum((i + 1) * hb, nhalo - 1), 0)),
                  pl.BlockSpec((POOL_GROUPS, POOL_GROUP_DIM, POOL_GROUP_DIM), lambda b, i: (0, 0, 0)),
                  pl.BlockSpec((1, D_MODEL), lambda b, i: (0, 0))],
        out_specs=pl.BlockSpec((1, ts, D_MODEL), lambda b, i: (b, i, 0)),
        out_shape=jax.ShapeDtypeStruct((batch, seq, D_MODEL), BF16),
        compiler_params=_params("parallel", "parallel"),
    )(proj3, proj3, proj3, pool_w, pool_scale)


def _merge_kernel(ya_ref, o1_ref, o2_ref, o3_ref, l1_ref, l2_ref, l3_ref, hc_ref, gate_ref, x_ref,
                  wa_ref, wb_ref, wc_ref, wo_ref, g_ref, b_ref, xo_ref, xb_ref, *, alpha):
    l1, l2, l3 = l1_ref[...], l2_ref[...], l3_ref[...]
    m = jnp.maximum(jnp.maximum(l1, l2), l3)
    e1, e2, e3 = jnp.exp(l1 - m), jnp.exp(l2 - m), jnp.exp(l3 - m)
    att = (e1 * o1_ref[...].astype(F32) + e2 * o2_ref[...].astype(F32)
           + e3 * o3_ref[...].astype(F32)) / (e1 + e2 + e3)
    y_a = jnp.dot(ya_ref[...], wa_ref[...], preferred_element_type=F32)
    y_b = jnp.dot(att.astype(BF16), wb_ref[...], preferred_element_type=F32)
    y_c = jnp.dot(hc_ref[...], wc_ref[...], preferred_element_type=F32)
    d = D_MODEL
    merged = (_sigmoid(gate_ref[:, 0:d].astype(F32)) * y_a
              + _sigmoid(gate_ref[:, d:2 * d].astype(F32)) * y_b
              + _sigmoid(gate_ref[:, 2 * d:3 * d].astype(F32)) * y_c)
    mix = jnp.dot(merged.astype(BF16), wo_ref[...], preferred_element_type=F32)
    x1 = _layer_norm(alpha * x_ref[...] + mix, g_ref[...], b_ref[...])
    xo_ref[...] = x1
    xb_ref[...] = x1.astype(BF16)


def _merge(y_pool, o_list, l_list, y_hgrn, proj, x, wa, wb, wc, wo, ln_g, ln_b, alpha):
    T = x.shape[0]
    tm = min(T, 256)
    gw = ATTN_GROUP_WIDTH
    d = D_MODEL
    row = lambda w: pl.BlockSpec((tm, w), lambda i: (i, 0))
    const = lambda a: pl.BlockSpec(a.shape, lambda i: (0,) * a.ndim, pipeline_mode=pl.Buffered(1))
    kern = functools.partial(_merge_kernel, alpha=alpha)
    return pl.pallas_call(
        kern,
        grid=(T // tm,),
        in_specs=[row(d), row(gw), row(gw), row(gw), row(gw), row(gw), row(gw), row(d),
                  pl.BlockSpec((tm, 3 * d), lambda i: (i, COL_GATES // (3 * d))),
                  row(d), const(wa), const(wb), const(wc), const(wo), const(ln_g), const(ln_b)],
        out_specs=[row(d), row(d)],
        out_shape=[jax.ShapeDtypeStruct((T, d), F32), jax.ShapeDtypeStruct((T, d), BF16)],
        compiler_params=_params("parallel"),
    )(y_pool, *o_list, *l_list, y_hgrn, proj, x, wa, wb, wc, wo, ln_g, ln_b)


def _ffn_kernel(xb_ref, x_ref, p_ref, wg_ref, wu_ref, wd_ref, wpp_ref, wpg_ref, g_ref, b_ref,
                xo_ref, xbo_ref, *, alpha):
    xb = xb_ref[...]
    gate = jnp.dot(xb, wg_ref[...], preferred_element_type=F32)
    up = jnp.dot(xb, wu_ref[...], preferred_element_type=F32)
    hidden = (gate * _sigmoid(gate) * up).astype(BF16)
    ffn = jnp.dot(hidden, wd_ref[...], preferred_element_type=F32)
    ple = jnp.dot(p_ref[...].astype(BF16), wpp_ref[...], preferred_element_type=F32)
    ple = ple * _sigmoid(jnp.dot(xb, wpg_ref[...], preferred_element_type=F32))
    x2 = _layer_norm(alpha * x_ref[...] + ffn + ple, g_ref[...], b_ref[...])
    xo_ref[...] = x2
    xbo_ref[...] = x2.astype(BF16)


def _ffn(x1_bf, x1, p, wg, wu, wd, wpp, wpg, ln_g, ln_b, alpha):
    T = x1.shape[0]
    tm = min(T, 256)
    d = D_MODEL
    row = lambda w: pl.BlockSpec((tm, w), lambda i: (i, 0))
    const = lambda a: pl.BlockSpec(a.shape, lambda i: (0,) * a.ndim, pipeline_mode=pl.Buffered(1))
    kern = functools.partial(_ffn_kernel, alpha=alpha)
    return pl.pallas_call(
        kern,
        grid=(T // tm,),
        in_specs=[row(d), row(d), row(PLE_DIM), const(wg), const(wu), const(wd), const(wpp),
                  const(wpg), const(ln_g), const(ln_b)],
        out_specs=[row(d), row(d)],
        out_shape=[jax.ShapeDtypeStruct((T, d), F32), jax.ShapeDtypeStruct((T, d), BF16)],
        compiler_params=_params("parallel"),
    )(x1_bf, x1, p, wg, wu, wd, wpp, wpg, ln_g, ln_b)


def _split_in_weights(w_in):
    d, aw, gw = D_MODEL, ATTN_HEADS * ATTN_HEAD_DIM, ATTN_GROUP_WIDTH
    pool = w_in[:, :, 0:d]
    aq = w_in[:, :, d:d + aw]
    ak = w_in[:, :, d + aw:d + 2 * aw]
    av = w_in[:, :, d + 2 * aw:d + 3 * aw]
    rest = w_in[:, :, d + 3 * aw:]
    qkv = lambda g: [t[:, :, g * gw:(g + 1) * gw] for t in (aq, ak, av)]
    main = jnp.concatenate([pool, rest] + qkv(0), axis=-1).astype(BF16)
    return main, jnp.concatenate(qkv(1), axis=-1).astype(BF16), jnp.concatenate(qkv(2), axis=-1).astype(BF16)


def kernel(x, p, w_in, pool_w, pool_scale, w_branch_a, w_branch_b, w_branch_c, hgrn_lb_logits, hgrn_norm_w, w_out, ln1_g, ln1_b, w_ffn_gate, w_ffn_up, w_ffn_down, w_ple_proj, w_ple_gate, ln2_g, ln2_b):
    batch, seq, d = x.shape
    depth = w_in.shape[0]
    T = batch * seq
    alpha = float((2 * depth) ** 0.25)

    lb = jnp.cumsum(jax.nn.softmax(hgrn_lb_logits.astype(F32), axis=0), axis=0)
    lb = lb - lb[:1]

    w_main, w_g2, w_g3 = _split_in_weights(w_in)
    bf = lambda a: a.astype(BF16)
    pool_w_b, wa, wb, wc, wo = bf(pool_w), bf(w_branch_a), bf(w_branch_b), bf(w_branch_c), bf(w_out)
    wg, wu, wd, wpp, wpg = bf(w_ffn_gate), bf(w_ffn_up), bf(w_ffn_down), bf(w_ple_proj), bf(w_ple_gate)

    slopes = [2.0 ** (-ALIBI_MAX_BIAS * (i + 1) / ATTN_HEADS) for i in range(ATTN_HEADS)]
    hpg = ATTN_HEADS_PER_GROUP
    gw = ATTN_GROUP_WIDTH

    xf = x.reshape(T, d).astype(F32)
    xb = xf.astype(BF16)
    for i in range(depth):
        proj = _inproj_main(xb, w_main[i])
        proj3 = proj.reshape(batch, seq, MAIN_WIDTH)
        o_list, l_list = [], []
        for g, (_, dil) in enumerate(ATTN_GROUPS):
            gs = tuple(slopes[g * hpg:(g + 1) * hpg])
            if dil == 1:
                o, l = _attention_group(proj3, proj3, proj3, (COL_AQ // gw, COL_AK // gw, COL_AV // gw),
                                        batch, seq, dil, gs)
            else:
                qkv = _inproj_dilated(xb, (w_g2, w_g3)[g - 1][i], batch, seq, dil)
                qkv = qkv.reshape(batch * dil, seq // dil, QKV_WIDTH)
                o, l = _attention_group(qkv, qkv, qkv, (0, 1, 2), batch, seq, dil, gs)
            o_list.append(o)
            l_list.append(l)
        y_hgrn = _hgrn(proj3, lb[i:i + 1, :HGRN_WIDTH], lb[i:i + 1, HGRN_WIDTH:], hgrn_norm_w[i:i + 1])
        y_pool = _pool(proj3, pool_w_b[i], pool_scale[i:i + 1])
        x1, x1b = _merge(y_pool.reshape(T, d), o_list, l_list, y_hgrn.reshape(T, d), proj, xf,
                         wa[i], wb[i], wc[i], wo[i], ln1_g[i:i + 1], ln1_b[i:i + 1], alpha)
        xf, xb = _ffn(x1b, x1, p[i].reshape(T, PLE_DIM), wg[i], wu[i], wd[i], wpp[i], wpg[i],
                      ln2_g[i:i + 1], ln2_b[i:i + 1], alpha)
    return xf.reshape(batch, seq, d).astype(x.dtype)
```

```python
import functools

import jax
import jax.numpy as jnp
from jax import lax
from jax.experimental import pallas as pl
from jax.experimental.pallas import tpu as pltpu

F32 = jnp.float32
BF16 = jnp.bfloat16
LANES = 128

D_MODEL = 1024
PLE_DIM = 256
POOL_WINDOWS = (2, 4, 8, 16)
POOL_GROUPS = 4
POOL_GROUP_DIM = D_MODEL // POOL_GROUPS
ATTN_GROUPS = ((128, 1), (512, 4), (2048, 16))
ATTN_HEADS_PER_GROUP = 4
ATTN_HEADS = ATTN_HEADS_PER_GROUP * len(ATTN_GROUPS)
ATTN_HEAD_DIM = 128
ATTN_GROUP_WIDTH = ATTN_HEADS_PER_GROUP * ATTN_HEAD_DIM
ATTN_SIDE = 64
ALIBI_MAX_BIAS = 8.0
NEG_INF = -1e30
HGRN_HEADS = 8
HGRN_HEAD_DIM = 128
HGRN_WIDTH = HGRN_HEADS * HGRN_HEAD_DIM
FFN_HIDDEN = 2816
LN_EPS = 1e-5
RMS_EPS = 1e-6

PLAIN_POOL = 0
PLAIN_HI = 1024
PLAIN_AQ = 2048
PLAIN_AK = 2560
PLAIN_AV = 3072
PLAIN_WIDTH = 3584
QKV_WIDTH = 3 * ATTN_GROUP_WIDTH

ATTN_QUERY_BLOCK = 128
ATTN_KEY_WINDOW = ATTN_QUERY_BLOCK + 2 * ATTN_SIDE
ATTN_TOKENS_PER_STEP = 2048
HGRN_BLOCK = 64
HGRN_HEADS_PER_STEP = 4
HGRN_CHUNK = 512
POOL_HALO = 16
VMEM_LIMIT = 52 * 1024 * 1024


def _params(*semantics):
    return pltpu.CompilerParams(dimension_semantics=semantics, vmem_limit_bytes=VMEM_LIMIT)


def _sigmoid(x):
    return 1.0 / (1.0 + jnp.exp(-x))


def _layer_norm(h, g, b):
    mu = jnp.mean(h, axis=-1, keepdims=True)
    d = h - mu
    var = jnp.mean(d * d, axis=-1, keepdims=True)
    return d * lax.rsqrt(var + LN_EPS) * g + b


def _inproj_kernel(x_ref, w_ref, *rest, epilogue):
    acc = jnp.dot(x_ref[...], w_ref[...], preferred_element_type=F32)
    if epilogue == "forget":
        lb_ref, logf_ref, key_ref = rest
        lb = lb_ref[...]
        sig = _sigmoid(acc)
        logf_ref[...] = jnp.log(lb + (1.0 - lb) * sig)
        key_ref[...] = ((1.0 - lb) * (1.0 - sig)).astype(key_ref.dtype)
        return
    (o_ref,) = rest
    if epilogue == "silu":
        acc = acc * _sigmoid(acc)
    elif epilogue == "sigmoid":
        acc = _sigmoid(acc)
    o_ref[...] = acc.astype(o_ref.dtype)


def _inproj(x_bf, w, epilogue, tn, lb=None):
    T, D = x_bf.shape
    N = w.shape[1]
    tm = min(T, 1024)
    in_specs = [pl.BlockSpec((tm, D), lambda j, i: (i, 0)),
                pl.BlockSpec((D, tn), lambda j, i: (0, j))]
    out_block = pl.BlockSpec((tm, tn), lambda j, i: (i, j))
    args = [x_bf, w]
    if epilogue == "forget":
        in_specs.append(pl.BlockSpec((1, tn), lambda j, i: (0, j)))
        args.append(lb)
        out_specs = [out_block, out_block]
        out_shape = [jax.ShapeDtypeStruct((T, N), F32), jax.ShapeDtypeStruct((T, N), BF16)]
    else:
        out_specs = out_block
        out_shape = jax.ShapeDtypeStruct((T, N), BF16)
    return pl.pallas_call(
        functools.partial(_inproj_kernel, epilogue=epilogue),
        grid=(N // tn, T // tm),
        in_specs=in_specs,
        out_specs=out_specs,
        out_shape=out_shape,
        compiler_params=_params("parallel", "parallel"),
        name="inproj_" + epilogue,
    )(*args)


def _inproj_dilated_kernel(x_ref, w_ref, o_ref, slab_ref, xp_ref, *, dilation):
    n = x_ref.shape[0] // dilation
    for c in range(D_MODEL // LANES):
        slab_ref[c] = x_ref[:, c * LANES:(c + 1) * LANES]
    for r in range(dilation):
        for c in range(D_MODEL // LANES):
            xp_ref[r * n:(r + 1) * n, c * LANES:(c + 1) * LANES] = (
                slab_ref[c, pl.ds(r, n, stride=dilation), :].astype(BF16))
    res = jnp.dot(xp_ref[...], w_ref[...], preferred_element_type=F32).astype(o_ref.dtype)
    for r in range(dilation):
        o_ref[0, r] = res[r * n:(r + 1) * n]


def _inproj_dilated(x_f32, w, batch, seq, dilation):
    N = w.shape[1]
    tb = min(seq, 1024)
    steps = seq // tb
    n = tb // dilation
    return pl.pallas_call(
        functools.partial(_inproj_dilated_kernel, dilation=dilation),
        grid=(batch * steps,),
        in_specs=[pl.BlockSpec((tb, D_MODEL), lambda i: (i, 0)),
                  pl.BlockSpec((D_MODEL, N), lambda i: (0, 0))],
        out_specs=pl.BlockSpec((1, dilation, n, N), lambda i: (i // steps, 0, i % steps, 0)),
        out_shape=jax.ShapeDtypeStruct((batch, dilation, seq // dilation, N), BF16),
        scratch_shapes=[pltpu.VMEM((D_MODEL // LANES, tb, LANES), F32), pltpu.VMEM((tb, D_MODEL), BF16)],
        compiler_params=_params("parallel"),
        name="inproj_dilated%d" % dilation,
    )(x_f32, w)


def _attn_kernel(q_ref, k_ref, v_ref, o_ref, l_ref, oacc_ref, lacc_ref, *, seq_len, dilation, slopes):
    c = pl.program_id(1)
    lq = q_ref.shape[2]
    qb, kw = ATTN_QUERY_BLOCK, ATTN_KEY_WINDOW
    nq = lq // qb
    scale = ATTN_HEAD_DIM ** -0.5
    row = lax.broadcasted_iota(jnp.int32, (qb, kw), 0)
    col = lax.broadcasted_iota(jnp.int32, (qb, kw), 1)
    col_minus_row = col - row

    def body(idx, carry):
        r = idx // nq
        r0 = pl.multiple_of((idx % nq) * qb, qb)
        t0 = c * lq + r0
        ks = pl.multiple_of(jnp.clip(t0 - ATTN_SIDE, 0, seq_len - kw), ATTN_SIDE)
        dist = jnp.abs(col_minus_row + (ks - t0))
        valid = dist <= ATTN_SIDE
        distf = dist.astype(F32) * float(dilation)
        if dilation == 1:
            out_rows = pl.ds(r0, qb)
        else:
            out_rows = pl.ds(r0 * dilation + r, qb, stride=dilation)
        for h in range(ATTN_HEADS_PER_GROUP):
            lanes = slice(h * ATTN_HEAD_DIM, (h + 1) * ATTN_HEAD_DIM)
            q = q_ref[0, r, pl.ds(r0, qb), lanes]
            kk = k_ref[0, r, pl.ds(ks, kw), lanes]
            vv = v_ref[0, r, pl.ds(ks, kw), lanes]
            s = lax.dot_general(q, kk, (((1,), (1,)), ((), ())), preferred_element_type=F32)
            s = jnp.where(valid, s * scale - slopes[h] * distf, NEG_INF)
            m = jnp.max(s, axis=1, keepdims=True)
            p = jnp.exp(s - m)
            l = jnp.sum(p, axis=1, keepdims=True)
            o = jnp.dot(p.astype(BF16), vv, preferred_element_type=F32) / l
            oacc_ref[h, out_rows, :] = o
            lacc_ref[h, out_rows, :] = jnp.broadcast_to(m + jnp.log(l), (qb, ATTN_HEAD_DIM))
        return carry

    lax.fori_loop(0, dilation * nq, body, 0)
    for h in range(ATTN_HEADS_PER_GROUP):
        lanes = slice(h * ATTN_HEAD_DIM, (h + 1) * ATTN_HEAD_DIM)
        o_ref[0, :, lanes] = oacc_ref[h].astype(o_ref.dtype)
        l_ref[0, :, lanes] = lacc_ref[h]


def _attention_group(q_arr, k_arr, v_arr, cols, batch, seq, dilation, slopes):
    L = seq // dilation
    gw = ATTN_GROUP_WIDTH
    tc = min(seq, ATTN_TOKENS_PER_STEP)
    lq = tc // dilation
    kern = functools.partial(_attn_kernel, seq_len=L, dilation=dilation, slopes=slopes)
    qc, kc, vc = cols
    o, lse = pl.pallas_call(
        kern,
        grid=(batch, seq // tc),
        in_specs=[pl.BlockSpec((1, dilation, lq, gw), lambda b, c: (b, 0, c, qc)),
                  pl.BlockSpec((1, dilation, L, gw), lambda b, c: (b, 0, 0, kc)),
                  pl.BlockSpec((1, dilation, L, gw), lambda b, c: (b, 0, 0, vc))],
        out_specs=[pl.BlockSpec((1, tc, gw), lambda b, c: (b, c, 0)),
                   pl.BlockSpec((1, tc, gw), lambda b, c: (b, c, 0))],
        out_shape=[jax.ShapeDtypeStruct((batch, seq, gw), BF16),
                   jax.ShapeDtypeStruct((batch, seq, gw), F32)],
        scratch_shapes=[pltpu.VMEM((ATTN_HEADS_PER_GROUP, tc, ATTN_HEAD_DIM), F32),
                        pltpu.VMEM((ATTN_HEADS_PER_GROUP, tc, ATTN_HEAD_DIM), F32)],
        compiler_params=_params("parallel", "arbitrary"),
        name="attn_dil%d" % dilation,
    )(q_arr, k_arr, v_arr)
    return o.reshape(batch * seq, gw), lse.reshape(batch * seq, gw)


def _running_sum_rows(x, row_index, reverse):
    n = x.shape[0]
    shift = 1
    while shift < n:
        if reverse:
            x = x + jnp.where(row_index < n - shift, pltpu.roll(x, n - shift, 0), 0.0)
        else:
            x = x + jnp.where(row_index >= shift, pltpu.roll(x, shift, 0), 0.0)
        shift *= 2
    return x


def _hgrn_kernel(qf_ref, vf_ref, lf_ref, kf_ref, qb_ref, vb_ref, lb_ref, kb_ref, g_ref, nw_ref, o_ref,
                 accf_ref, accb_ref, sf_ref, sb_ref):
    c = pl.program_id(2)
    nc = pl.num_programs(2)
    chunk = qf_ref.shape[1]
    seq = o_ref.shape[1]
    hp = HGRN_HEADS_PER_STEP
    blk = HGRN_BLOCK
    half = blk // 2
    nblk = chunk // blk
    dh = HGRN_HEAD_DIM
    pw = 2 * dh

    @pl.when(c == 0)
    def _():
        sf_ref[...] = jnp.zeros_like(sf_ref)
        sb_ref[...] = jnp.zeros_like(sb_ref)

    row_index = lax.broadcasted_iota(jnp.int32, (blk, pw), 0)
    ti = lax.broadcasted_iota(jnp.int32, (blk, 2 * blk), 0)
    si = lax.broadcasted_iota(jnp.int32, (blk, 2 * blk), 1) % blk
    causal = si <= ti
    anticausal = si >= ti
    zeros_kv = jnp.zeros((blk, dh), BF16)
    zeros_st = jnp.zeros((dh, dh), BF16)

    def block_diag(a, b, z):
        return jnp.concatenate([jnp.concatenate([a, z], axis=1), jnp.concatenate([z, b], axis=1)], axis=0)

    def load_unit(pair, r0, q_ref, v_ref, logf_ref, key_ref, s_ref):
        lanes = slice(pair * pw, (pair + 1) * pw)
        rows = pl.ds(r0, blk)
        return (q_ref[0, rows, lanes], v_ref[0, rows, lanes], logf_ref[0, rows, lanes], key_ref[0, rows, lanes],
                s_ref[2 * pair], s_ref[2 * pair + 1])

    def compute_unit(loaded, backward):
        q, v, logf, k, st0, st1 = loaded
        q = q.astype(F32)
        k = k.astype(F32)
        a = _running_sum_rows(logf, row_index, backward)
        if backward:
            total = a[0:1, :]
            mid = a[half:half + 1, :]
            mask = anticausal
        else:
            total = a[blk - 1:blk, :]
            mid = a[half - 1:half, :]
            mask = causal
        q_mid = q * jnp.exp(a - mid)
        k_mid = k * jnp.exp(mid - a)
        k_mid_b = k_mid.astype(BF16)
        att = lax.dot_general(q_mid.astype(BF16), block_diag(k_mid_b[:, :dh], k_mid_b[:, dh:], zeros_kv),
                              (((1,), (1,)), ((), ())), preferred_element_type=F32)
        att = jnp.where(mask, att, 0.0).astype(BF16)
        q_in = (q_mid * jnp.exp(mid)).astype(BF16)
        o = jnp.dot(att, block_diag(v[:, :dh], v[:, dh:], zeros_kv), preferred_element_type=F32)
        o = o + lax.dot_general(q_in, block_diag(st0.astype(BF16), st1.astype(BF16), zeros_st),
                                (((1,), (1,)), ((), ())), preferred_element_type=F32)
        k_out = (k_mid * jnp.exp(total - mid)).astype(BF16)
        decay = jnp.exp(total)
        new_states = []
        for j, st in enumerate((st0, st1)):
            hl = slice(j * dh, (j + 1) * dh)
            upd = lax.dot_general(v[:, hl], k_out[:, hl], (((0,), (0,)), ((), ())), preferred_element_type=F32)
            new_states.append(st * decay[:, hl] + upd)
        return o, new_states

    def body(n, carry):
        rf = pl.multiple_of(n * blk, blk)
        rb = pl.multiple_of((nblk - 1 - n) * blk, blk)
        units = []
        for pair in range(hp // 2):
            units.append((pair, False, load_unit(pair, rf, qf_ref, vf_ref, lf_ref, kf_ref, sf_ref)))
            units.append((pair, True, load_unit(pair, rb, qb_ref, vb_ref, lb_ref, kb_ref, sb_ref)))
        results = [compute_unit(loaded, backward) for _, backward, loaded in units]
        for (pair, backward, _), (o, states) in zip(units, results):
            lanes = slice(pair * pw, (pair + 1) * pw)
            if backward:
                accb_ref[pl.ds(pl.multiple_of((nc - 1 - c) * chunk + rb, blk), blk), lanes] = o
                sb_ref[2 * pair], sb_ref[2 * pair + 1] = states
            else:
                accf_ref[pl.ds(pl.multiple_of(c * chunk + rf, blk), blk), lanes] = o
                sf_ref[2 * pair], sf_ref[2 * pair + 1] = states
        return carry

    lax.fori_loop(0, nblk, body, 0)

    @pl.when(c == nc - 1)
    def _():
        fin = min(seq, 256)

        def finish(t, carry):
            rows = pl.ds(pl.multiple_of(t * fin, fin), fin)
            for h in range(hp):
                lanes = slice(h * dh, (h + 1) * dh)
                o = accf_ref[rows, lanes] + accb_ref[rows, lanes]
                o = o * lax.rsqrt(jnp.mean(o * o, axis=-1, keepdims=True) + RMS_EPS)
                o = o * nw_ref[:, lanes] * g_ref[0, rows, lanes].astype(F32)
                o_ref[0, rows, lanes] = o.astype(o_ref.dtype)
            return carry

        lax.fori_loop(0, seq // fin, finish, 0)


def _hgrn(qg3, plain3, logf3, key3, norm_w):
    batch, seq, _ = qg3.shape
    hp = HGRN_HEADS_PER_STEP
    w = hp * HGRN_HEAD_DIM
    chunk = min(seq, HGRN_CHUNK)
    nc = seq // chunk

    def fwd(offset):
        base = offset // w
        return pl.BlockSpec((1, chunk, w), lambda b, h, c: (b, c, base + h))

    def bwd(offset):
        base = offset // w
        return pl.BlockSpec((1, chunk, w), lambda b, h, c: (b, nc - 1 - c, base + h))

    gbase = HGRN_WIDTH // w
    return pl.pallas_call(
        _hgrn_kernel,
        grid=(batch, HGRN_HEADS // hp, nc),
        in_specs=[fwd(0), fwd(PLAIN_HI), fwd(0), fwd(0),
                  bwd(0), bwd(PLAIN_HI), bwd(HGRN_WIDTH), bwd(HGRN_WIDTH),
                  pl.BlockSpec((1, seq, w), lambda b, h, c: (b, 0, gbase + h)),
                  pl.BlockSpec((1, w), lambda b, h, c: (0, h))],
        out_specs=pl.BlockSpec((1, seq, w), lambda b, h, c: (b, 0, h)),
        out_shape=jax.ShapeDtypeStruct((batch, seq, HGRN_WIDTH), BF16),
        scratch_shapes=[pltpu.VMEM((seq, w), F32), pltpu.VMEM((seq, w), F32),
                        pltpu.VMEM((hp, HGRN_HEAD_DIM, HGRN_HEAD_DIM), F32),
                        pltpu.VMEM((hp, HGRN_HEAD_DIM, HGRN_HEAD_DIM), F32)],
        compiler_params=_params("parallel", "parallel", "arbitrary"),
        name="hgrn",
    )(qg3, plain3, logf3, key3, qg3, plain3, logf3, key3, qg3, norm_w)


def _pool_kernel(prev_ref, cur_ref, next_ref, pw_ref, ps_ref, o_ref, *, seq_len):
    i = pl.program_id(1)
    last = pl.num_programs(1) - 1
    ts = cur_ref.shape[1]
    n = ts + 2 * POOL_HALO
    pos = i * ts + lax.broadcasted_iota(jnp.int32, (ts, 1), 0)
    for g in range(POOL_GROUPS):
        lanes = slice(g * POOL_GROUP_DIM, (g + 1) * POOL_GROUP_DIM)
        half = POOL_WINDOWS[g] // 2
        cur = cur_ref[0, :, lanes].astype(F32)
        prev = jnp.where(i > 0, prev_ref[0, :, lanes].astype(F32), 0.0)
        nxt = jnp.where(i < last, next_ref[0, :, lanes].astype(F32), 0.0)
        ext = jnp.concatenate([prev, cur, nxt], axis=0)
        w = ext + pltpu.roll(ext, 1, 0)
        step = 1
        while step < half:
            w = pltpu.roll(w, step, 0) + pltpu.roll(w, n - step, 0)
            step *= 2
        wsum = w[POOL_HALO:POOL_HALO + ts]
        count = (jnp.minimum(pos + half, seq_len) - jnp.maximum(pos - half, 0)).astype(F32)
        mixed = wsum / count - cur
        y = jnp.dot(mixed.astype(BF16), pw_ref[g], preferred_element_type=F32)
        o_ref[0, :, lanes] = (y * ps_ref[:, lanes]).astype(o_ref.dtype)


def _pool(plain3, pool_w, pool_scale):
    batch, seq, _ = plain3.shape
    ts = min(seq, 512)
    hb = ts // POOL_HALO
    nhalo = seq // POOL_HALO
    kern = functools.partial(_pool_kernel, seq_len=seq)
    return pl.pallas_call(
        kern,
        grid=(batch, seq // ts),
        in_specs=[pl.BlockSpec((1, POOL_HALO, D_MODEL), lambda b, i: (b, jnp.maximum(i * hb - 1, 0), 0)),
                  pl.BlockSpec((1, ts, D_MODEL), lambda b, i: (b, i, 0)),
                  pl.BlockSpec((1, POOL_HALO, D_MODEL), lambda b, i: (b, jnp.minimum((i + 1) * hb, nhalo - 1), 0)),
                  pl.BlockSpec((POOL_GROUPS, POOL_GROUP_DIM, POOL_GROUP_DIM), lambda b, i: (0, 0, 0)),
                  pl.BlockSpec((1, D_MODEL), lambda b, i: (0, 0))],
        out_specs=pl.BlockSpec((1, ts, D_MODEL), lambda b, i: (b, i, 0)),
        out_shape=jax.ShapeDtypeStruct((batch, seq, D_MODEL), BF16),
        compiler_params=_params("parallel", "parallel"),
        name="pool",
    )(plain3, plain3, plain3, pool_w, pool_scale)


def _merge_kernel(ya_ref, o1_ref, o2_ref, o3_ref, l1_ref, l2_ref, l3_ref, hc_ref, gate_ref, x_ref,
                  wa_ref, wb_ref, wc_ref, wo_ref, g_ref, b_ref, xo_ref, xb_ref, *, alpha):
    l1, l2, l3 = l1_ref[...], l2_ref[...], l3_ref[...]
    m = jnp.maximum(jnp.maximum(l1, l2), l3)
    e1, e2, e3 = jnp.exp(l1 - m), jnp.exp(l2 - m), jnp.exp(l3 - m)
    att = (e1 * o1_ref[...].astype(F32) + e2 * o2_ref[...].astype(F32)
           + e3 * o3_ref[...].astype(F32)) / (e1 + e2 + e3)
    y_a = jnp.dot(ya_ref[...], wa_ref[...], preferred_element_type=F32)
    y_b = jnp.dot(att.astype(BF16), wb_ref[...], preferred_element_type=F32)
    y_c = jnp.dot(hc_ref[...], wc_ref[...], preferred_element_type=F32)
    d = D_MODEL
    merged = (gate_ref[:, 0:d].astype(F32) * y_a + gate_ref[:, d:2 * d].astype(F32) * y_b
              + gate_ref[:, 2 * d:3 * d].astype(F32) * y_c)
    mix = jnp.dot(merged.astype(BF16), wo_ref[...], preferred_element_type=F32)
    x1 = _layer_norm(alpha * x_ref[...] + mix, g_ref[...], b_ref[...])
    xo_ref[...] = x1
    xb_ref[...] = x1.astype(BF16)


def _merge(y_pool, o_list, l_list, y_hgrn, gates, x, wa, wb, wc, wo, ln_g, ln_b, alpha):
    T = x.shape[0]
    tm = min(T, 256)
    gw = ATTN_GROUP_WIDTH
    d = D_MODEL
    row = lambda w: pl.BlockSpec((tm, w), lambda i: (i, 0))
    const = lambda a: pl.BlockSpec(a.shape, lambda i: (0,) * a.ndim, pipeline_mode=pl.Buffered(1))
    kern = functools.partial(_merge_kernel, alpha=alpha)
    return pl.pallas_call(
        kern,
        grid=(T // tm,),
        in_specs=[row(d), row(gw), row(gw), row(gw), row(gw), row(gw), row(gw), row(d), row(3 * d),
                  row(d), const(wa), const(wb), const(wc), const(wo), const(ln_g), const(ln_b)],
        out_specs=[row(d), row(d)],
        out_shape=[jax.ShapeDtypeStruct((T, d), F32), jax.ShapeDtypeStruct((T, d), BF16)],
        compiler_params=_params("parallel"),
        name="merge",
    )(y_pool, *o_list, *l_list, y_hgrn, gates, x, wa, wb, wc, wo, ln_g, ln_b)


def _ffn_kernel(xb_ref, x_ref, p_ref, wg_ref, wu_ref, wd_ref, wpp_ref, wpg_ref, g_ref, b_ref,
                xo_ref, xbo_ref, *, alpha):
    xb = xb_ref[...]
    gate = jnp.dot(xb, wg_ref[...], preferred_element_type=F32)
    up = jnp.dot(xb, wu_ref[...], preferred_element_type=F32)
    hidden = (gate * _sigmoid(gate) * up).astype(BF16)
    ffn = jnp.dot(hidden, wd_ref[...], preferred_element_type=F32)
    ple = jnp.dot(p_ref[...].astype(BF16), wpp_ref[...], preferred_element_type=F32)
    ple = ple * _sigmoid(jnp.dot(xb, wpg_ref[...], preferred_element_type=F32))
    x2 = _layer_norm(alpha * x_ref[...] + ffn + ple, g_ref[...], b_ref[...])
    xo_ref[...] = x2
    xbo_ref[...] = x2.astype(BF16)


def _ffn(x1_bf, x1, p, wg, wu, wd, wpp, wpg, ln_g, ln_b, alpha):
    T = x1.shape[0]
    tm = min(T, 256)
    d = D_MODEL
    row = lambda w: pl.BlockSpec((tm, w), lambda i: (i, 0))
    const = lambda a: pl.BlockSpec(a.shape, lambda i: (0,) * a.ndim, pipeline_mode=pl.Buffered(1))
    kern = functools.partial(_ffn_kernel, alpha=alpha)
    return pl.pallas_call(
        kern,
        grid=(T // tm,),
        in_specs=[row(d), row(d), row(PLE_DIM), const(wg), const(wu), const(wd), const(wpp),
                  const(wpg), const(ln_g), const(ln_b)],
        out_specs=[row(d), row(d)],
        out_shape=[jax.ShapeDtypeStruct((T, d), F32), jax.ShapeDtypeStruct((T, d), BF16)],
        compiler_params=_params("parallel"),
        name="ffn",
    )(x1_bf, x1, p, wg, wu, wd, wpp, wpg, ln_g, ln_b)


def _split_in_weights(w_in):
    d, aw, gw, hw = D_MODEL, ATTN_HEADS * ATTN_HEAD_DIM, ATTN_GROUP_WIDTH, HGRN_WIDTH
    pool = w_in[:, :, 0:d]
    aq = w_in[:, :, d:d + aw]
    ak = w_in[:, :, d + aw:d + 2 * aw]
    av = w_in[:, :, d + 2 * aw:d + 3 * aw]
    h0 = d + 3 * aw
    hq, hi, hff, hfb, hg = [w_in[:, :, h0 + n * hw:h0 + (n + 1) * hw] for n in range(5)]
    gates = w_in[:, :, h0 + 5 * hw:]
    qkv = lambda g: [t[:, :, g * gw:(g + 1) * gw] for t in (aq, ak, av)]
    cat = lambda parts: jnp.concatenate(parts, axis=-1).astype(BF16)
    return (cat([pool, hi] + qkv(0)), cat([hq, hg]), cat([hff, hfb]), gates.astype(BF16),
            cat(qkv(1)), cat(qkv(2)))


def kernel(x, p, w_in, pool_w, pool_scale, w_branch_a, w_branch_b, w_branch_c, hgrn_lb_logits, hgrn_norm_w, w_out, ln1_g, ln1_b, w_ffn_gate, w_ffn_up, w_ffn_down, w_ple_proj, w_ple_gate, ln2_g, ln2_b):
    batch, seq, d = x.shape
    depth = w_in.shape[0]
    T = batch * seq
    alpha = float((2 * depth) ** 0.25)

    lb = jnp.cumsum(jax.nn.softmax(hgrn_lb_logits.astype(F32), axis=0), axis=0)
    lb = lb - lb[:1]

    w_plain, w_silu, w_forget, w_gates, w_g2, w_g3 = _split_in_weights(w_in)
    bf = lambda a: a.astype(BF16)
    pool_w_b, wa, wb, wc, wo = bf(pool_w), bf(w_branch_a), bf(w_branch_b), bf(w_branch_c), bf(w_out)
    wg, wu, wd, wpp, wpg = bf(w_ffn_gate), bf(w_ffn_up), bf(w_ffn_down), bf(w_ple_proj), bf(w_ple_gate)

    slopes = [2.0 ** (-ALIBI_MAX_BIAS * (i + 1) / ATTN_HEADS) for i in range(ATTN_HEADS)]
    hpg = ATTN_HEADS_PER_GROUP
    gw = ATTN_GROUP_WIDTH

    xf = x.reshape(T, d).astype(F32)
    xb = xf.astype(BF16)
    for i in range(depth):
        plain = _inproj(xb, w_plain[i], "plain", PLAIN_WIDTH // 2)
        qg = _inproj(xb, w_silu[i], "silu", HGRN_WIDTH)
        logf, key = _inproj(xb, w_forget[i], "forget", HGRN_WIDTH, lb=lb[i:i + 1])
        gates = _inproj(xb, w_gates[i], "sigmoid", D_MODEL)
        plain3 = plain.reshape(batch, seq, PLAIN_WIDTH)
        o_list, l_list = [], []
        for g, (_, dil) in enumerate(ATTN_GROUPS):
            gs = tuple(slopes[g * hpg:(g + 1) * hpg])
            if dil == 1:
                qkv = plain.reshape(batch, 1, seq, PLAIN_WIDTH)
                cols = (PLAIN_AQ // gw, PLAIN_AK // gw, PLAIN_AV // gw)
            else:
                qkv = _inproj_dilated(xf, (w_g2, w_g3)[g - 1][i], batch, seq, dil)
                cols = (0, 1, 2)
            o, l = _attention_group(qkv, qkv, qkv, cols, batch, seq, dil, gs)
            o_list.append(o)
            l_list.append(l)
        y_hgrn = _hgrn(qg.reshape(batch, seq, 2 * HGRN_WIDTH), plain3,
                       logf.reshape(batch, seq, 2 * HGRN_WIDTH), key.reshape(batch, seq, 2 * HGRN_WIDTH),
                       hgrn_norm_w[i:i + 1])
        y_pool = _pool(plain3, pool_w_b[i], pool_scale[i:i + 1])
        x1, x1b = _merge(y_pool.reshape(T, d), o_list, l_list, y_hgrn.reshape(T, d), gates, xf,
                         wa[i], wb[i], wc[i], wo[i], ln1_g[i:i + 1], ln1_b[i:i + 1], alpha)
        xf, xb = _ffn(x1b, x1, p[i].reshape(T, PLE_DIM), wg[i], wu[i], wd[i], wpp[i], wpg[i],
                      ln2_g[i:i + 1], ln2_b[i:i + 1], alpha)
    return xf.reshape(batch, seq, d).astype(x.dtype)
```

```python
import functools

import jax
import jax.numpy as jnp
from jax import lax
from jax.experimental import pallas as pl
from jax.experimental.pallas import tpu as pltpu

F32 = jnp.float32
BF16 = jnp.bfloat16
LANES = 128
LOG2_E = 1.4426950408889634
LN_2 = 0.6931471805599453

D_MODEL = 1024
PLE_DIM = 256
POOL_WINDOWS = (2, 4, 8, 16)
POOL_GROUPS = 4
POOL_GROUP_DIM = D_MODEL // POOL_GROUPS
ATTN_GROUPS = ((128, 1), (512, 4), (2048, 16))
ATTN_HEADS_PER_GROUP = 4
ATTN_HEADS = ATTN_HEADS_PER_GROUP * len(ATTN_GROUPS)
ATTN_HEAD_DIM = 128
ATTN_GROUP_WIDTH = ATTN_HEADS_PER_GROUP * ATTN_HEAD_DIM
ATTN_SIDE = 64
ALIBI_MAX_BIAS = 8.0
NEG_INF = -1e30
HGRN_HEADS = 8
HGRN_HEAD_DIM = 128
HGRN_WIDTH = HGRN_HEADS * HGRN_HEAD_DIM
FFN_HIDDEN = 2816
LN_EPS = 1e-5
RMS_EPS = 1e-6

PLAIN_POOL = 0
PLAIN_HI = 1024
PLAIN_AQ = 2048
PLAIN_AK = 2560
PLAIN_AV = 3072
PLAIN_WIDTH = 3584
QKV_WIDTH = 3 * ATTN_GROUP_WIDTH

ATTN_QUERY_BLOCK = 128
ATTN_KEY_WINDOW = ATTN_QUERY_BLOCK + 2 * ATTN_SIDE
ATTN_TOKENS_PER_STEP = 2048
ATTN_BLOCKS_PER_ITER = 2
ATTN_WINDOW_OFFSETS = 3
HGRN_BLOCK = 64
HGRN_HEADS_PER_STEP = 4
HGRN_CHUNK = 512
POOL_HALO = 16
VMEM_LIMIT = 52 * 1024 * 1024


def _params(*semantics):
    return pltpu.CompilerParams(dimension_semantics=semantics, vmem_limit_bytes=VMEM_LIMIT)


def _sigmoid(x):
    return 1.0 / (1.0 + jnp.exp(-x))


def _layer_norm(h, g, b):
    mu = jnp.mean(h, axis=-1, keepdims=True)
    d = h - mu
    var = jnp.mean(d * d, axis=-1, keepdims=True)
    return d * lax.rsqrt(var + LN_EPS) * g + b


def _inproj_kernel(x_ref, w_ref, *rest, epilogue):
    acc = jnp.dot(x_ref[...], w_ref[...], preferred_element_type=F32)
    if epilogue == "forget":
        lb_ref, logf_ref, key_ref = rest
        lb = lb_ref[...]
        sig = _sigmoid(acc)
        logf_ref[...] = jnp.log(lb + (1.0 - lb) * sig)
        key_ref[...] = ((1.0 - lb) * (1.0 - sig)).astype(key_ref.dtype)
        return
    (o_ref,) = rest
    if epilogue == "silu":
        acc = acc * _sigmoid(acc)
    elif epilogue == "sigmoid":
        acc = _sigmoid(acc)
    o_ref[...] = acc.astype(o_ref.dtype)


def _inproj(x_bf, w, epilogue, tn, lb=None):
    T, D = x_bf.shape
    N = w.shape[1]
    tm = min(T, 1024)
    in_specs = [pl.BlockSpec((tm, D), lambda j, i: (i, 0)),
                pl.BlockSpec((D, tn), lambda j, i: (0, j))]
    out_block = pl.BlockSpec((tm, tn), lambda j, i: (i, j))
    args = [x_bf, w]
    if epilogue == "forget":
        in_specs.append(pl.BlockSpec((1, tn), lambda j, i: (0, j)))
        args.append(lb)
        out_specs = [out_block, out_block]
        out_shape = [jax.ShapeDtypeStruct((T, N), F32), jax.ShapeDtypeStruct((T, N), BF16)]
    else:
        out_specs = out_block
        out_shape = jax.ShapeDtypeStruct((T, N), BF16)
    return pl.pallas_call(
        functools.partial(_inproj_kernel, epilogue=epilogue),
        grid=(N // tn, T // tm),
        in_specs=in_specs,
        out_specs=out_specs,
        out_shape=out_shape,
        compiler_params=_params("parallel", "parallel"),
        name="inproj_" + epilogue,
    )(*args)


def _inproj_dilated_kernel(x_ref, w_ref, o_ref, slab_ref, xp_ref, *, dilation):
    n = x_ref.shape[0] // dilation
    for c in range(D_MODEL // LANES):
        slab_ref[c] = x_ref[:, c * LANES:(c + 1) * LANES]
    for r in range(dilation):
        for c in range(D_MODEL // LANES):
            xp_ref[r * n:(r + 1) * n, c * LANES:(c + 1) * LANES] = (
                slab_ref[c, pl.ds(r, n, stride=dilation), :].astype(BF16))
    res = jnp.dot(xp_ref[...], w_ref[...], preferred_element_type=F32).astype(o_ref.dtype)
    for r in range(dilation):
        o_ref[0, r] = res[r * n:(r + 1) * n]


def _inproj_dilated(x_f32, w, batch, seq, dilation):
    N = w.shape[1]
    tb = min(seq, 1024)
    steps = seq // tb
    n = tb // dilation
    return pl.pallas_call(
        functools.partial(_inproj_dilated_kernel, dilation=dilation),
        grid=(batch * steps,),
        in_specs=[pl.BlockSpec((tb, D_MODEL), lambda i: (i, 0)),
                  pl.BlockSpec((D_MODEL, N), lambda i: (0, 0))],
        out_specs=pl.BlockSpec((1, dilation, n, N), lambda i: (i // steps, 0, i % steps, 0)),
        out_shape=jax.ShapeDtypeStruct((batch, dilation, seq // dilation, N), BF16),
        scratch_shapes=[pltpu.VMEM((D_MODEL // LANES, tb, LANES), F32), pltpu.VMEM((tb, D_MODEL), BF16)],
        compiler_params=_params("parallel"),
        name="inproj_dilated%d" % dilation,
    )(x_f32, w)


def _attn_kernel(q_ref, k_ref, v_ref, o_ref, l_ref, oacc_ref, lacc_ref, bias_ref, *, seq_len, dilation, slopes):
    c = pl.program_id(1)
    lq = q_ref.shape[2]
    qb, kw = ATTN_QUERY_BLOCK, ATTN_KEY_WINDOW
    nq = lq // qb
    per_iter = ATTN_BLOCKS_PER_ITER
    scale2 = ATTN_HEAD_DIM ** -0.5 * LOG2_E
    heads = range(ATTN_HEADS_PER_GROUP)
    lanes = [slice(h * ATTN_HEAD_DIM, (h + 1) * ATTN_HEAD_DIM) for h in heads]

    @pl.when((pl.program_id(0) == 0) & (c == 0))
    def _():
        row = lax.broadcasted_iota(jnp.int32, (qb, kw), 0)
        col = lax.broadcasted_iota(jnp.int32, (qb, kw), 1)
        for w in range(ATTN_WINDOW_OFFSETS):
            dist = jnp.abs(col - row - w * ATTN_SIDE)
            distf = dist.astype(F32) * float(dilation)
            for h in heads:
                bias_ref[h, w] = jnp.where(dist <= ATTN_SIDE, distf * (-slopes[h] * LOG2_E), NEG_INF * LOG2_E)

    def body(it, carry):
        blocks = []
        for u in range(per_iter):
            idx = it * per_iter + u
            r = idx // nq
            r0 = pl.multiple_of((idx % nq) * qb, qb)
            t0 = c * lq + r0
            ks = pl.multiple_of(jnp.clip(t0 - ATTN_SIDE, 0, seq_len - kw), ATTN_SIDE)
            if dilation == 1:
                out_rows = pl.ds(r0, qb)
            else:
                out_rows = pl.ds(r0 * dilation + r, qb, stride=dilation)
            blocks.append(((t0 - ks) // ATTN_SIDE, out_rows, q_ref[0, r, pl.ds(r0, qb), :],
                           k_ref[0, r, pl.ds(ks, kw), :], v_ref[0, r, pl.ds(ks, kw), :]))
        scores = [[lax.dot_general(q[:, lanes[h]], k[:, lanes[h]], (((1,), (1,)), ((), ())),
                                   preferred_element_type=F32) for h in heads]
                  for _, _, q, k, _ in blocks]
        probs = []
        for (w, _, _, _, _), block_scores in zip(blocks, scores):
            block_probs = []
            for h in heads:
                s = block_scores[h] * scale2 + bias_ref[h, w]
                m = jnp.max(s, axis=1, keepdims=True)
                p = jnp.exp2(s - m)
                block_probs.append((p.astype(BF16), jnp.sum(p, axis=1, keepdims=True), m))
            probs.append(block_probs)
        pv = [[jnp.dot(block_probs[h][0], v[:, lanes[h]], preferred_element_type=F32) for h in heads]
              for (_, _, _, _, v), block_probs in zip(blocks, probs)]
        for (_, out_rows, _, _, _), block_probs, block_pv in zip(blocks, probs, pv):
            for h in heads:
                _, l, m = block_probs[h]
                oacc_ref[h, out_rows, :] = block_pv[h] / l
                lse = (m + jnp.log2(l)) * LN_2
                lacc_ref[h, out_rows, :] = jnp.broadcast_to(lse, (qb, ATTN_HEAD_DIM))
        return carry

    lax.fori_loop(0, dilation * nq // per_iter, body, 0)
    for h in heads:
        o_ref[0, :, lanes[h]] = oacc_ref[h].astype(o_ref.dtype)
        l_ref[0, :, lanes[h]] = lacc_ref[h]


def _attention_group(q_arr, k_arr, v_arr, cols, batch, seq, dilation, slopes):
    L = seq // dilation
    gw = ATTN_GROUP_WIDTH
    tc = min(seq, ATTN_TOKENS_PER_STEP)
    lq = tc // dilation
    kern = functools.partial(_attn_kernel, seq_len=L, dilation=dilation, slopes=slopes)
    qc, kc, vc = cols
    o, lse = pl.pallas_call(
        kern,
        grid=(batch, seq // tc),
        in_specs=[pl.BlockSpec((1, dilation, lq, gw), lambda b, c: (b, 0, c, qc)),
                  pl.BlockSpec((1, dilation, L, gw), lambda b, c: (b, 0, 0, kc)),
                  pl.BlockSpec((1, dilation, L, gw), lambda b, c: (b, 0, 0, vc))],
        out_specs=[pl.BlockSpec((1, tc, gw), lambda b, c: (b, c, 0)),
                   pl.BlockSpec((1, tc, gw), lambda b, c: (b, c, 0))],
        out_shape=[jax.ShapeDtypeStruct((batch, seq, gw), BF16),
                   jax.ShapeDtypeStruct((batch, seq, gw), F32)],
        scratch_shapes=[pltpu.VMEM((ATTN_HEADS_PER_GROUP, tc, ATTN_HEAD_DIM), F32),
                        pltpu.VMEM((ATTN_HEADS_PER_GROUP, tc, ATTN_HEAD_DIM), F32),
                        pltpu.VMEM((ATTN_HEADS_PER_GROUP, ATTN_WINDOW_OFFSETS, ATTN_QUERY_BLOCK, ATTN_KEY_WINDOW), F32)],
        compiler_params=_params("arbitrary", "arbitrary"),
        name="attn_dil%d" % dilation,
    )(q_arr, k_arr, v_arr)
    return o.reshape(batch * seq, gw), lse.reshape(batch * seq, gw)


def _running_sum_rows(x, row_index, reverse):
    n = x.shape[0]
    shift = 1
    while shift < n:
        if reverse:
            x = x + jnp.where(row_index < n - shift, pltpu.roll(x, n - shift, 0), 0.0)
        else:
            x = x + jnp.where(row_index >= shift, pltpu.roll(x, shift, 0), 0.0)
        shift *= 2
    return x


def _hgrn_kernel(qf_ref, vf_ref, lf_ref, kf_ref, qb_ref, vb_ref, lb_ref, kb_ref, g_ref, nw_ref, o_ref,
                 accf_ref, accb_ref, sf_ref, sb_ref):
    c = pl.program_id(2)
    nc = pl.num_programs(2)
    chunk = qf_ref.shape[1]
    seq = o_ref.shape[1]
    hp = HGRN_HEADS_PER_STEP
    blk = HGRN_BLOCK
    half = blk // 2
    nblk = chunk // blk
    dh = HGRN_HEAD_DIM
    pw = 2 * dh

    @pl.when(c == 0)
    def _():
        sf_ref[...] = jnp.zeros_like(sf_ref)
        sb_ref[...] = jnp.zeros_like(sb_ref)

    row_index = lax.broadcasted_iota(jnp.int32, (blk, pw), 0)
    ti = lax.broadcasted_iota(jnp.int32, (blk, 2 * blk), 0)
    si = lax.broadcasted_iota(jnp.int32, (blk, 2 * blk), 1) % blk
    causal = si <= ti
    anticausal = si >= ti
    zeros_kv = jnp.zeros((blk, dh), BF16)
    zeros_st = jnp.zeros((dh, dh), BF16)

    def block_diag(a, b, z):
        return jnp.concatenate([jnp.concatenate([a, z], axis=1), jnp.concatenate([z, b], axis=1)], axis=0)

    def load_unit(pair, r0, q_ref, v_ref, logf_ref, key_ref, s_ref):
        lanes = slice(pair * pw, (pair + 1) * pw)
        rows = pl.ds(r0, blk)
        return (q_ref[0, rows, lanes], v_ref[0, rows, lanes], logf_ref[0, rows, lanes], key_ref[0, rows, lanes],
                s_ref[2 * pair], s_ref[2 * pair + 1])

    def prepare_unit(loaded, backward):
        q, v, logf, k, st0, st1 = loaded
        q = q.astype(F32)
        k = k.astype(F32)
        a = _running_sum_rows(logf, row_index, backward)
        if backward:
            total = a[0:1, :]
            mid = a[half:half + 1, :]
        else:
            total = a[blk - 1:blk, :]
            mid = a[half - 1:half, :]
        q_mid = q * jnp.exp(a - mid)
        k_mid = k * jnp.exp(mid - a)
        k_mid_b = k_mid.astype(BF16)
        q_in = (q_mid * jnp.exp(mid)).astype(BF16)
        k_out = (k_mid * jnp.exp(total - mid)).astype(BF16)
        return dict(q_mid=q_mid.astype(BF16), k_diag=block_diag(k_mid_b[:, :dh], k_mid_b[:, dh:], zeros_kv),
                    q_in=q_in, s_diag=block_diag(st0.astype(BF16), st1.astype(BF16), zeros_st),
                    v=v, v_diag=block_diag(v[:, :dh], v[:, dh:], zeros_kv), k_out=k_out,
                    decay=jnp.exp(total), states=(st0, st1), mask=anticausal if backward else causal)

    nt = (((1,), (1,)), ((), ()))
    tn = (((0,), (0,)), ((), ()))

    def body(n, carry):
        rf = pl.multiple_of(n * blk, blk)
        rb = pl.multiple_of((nblk - 1 - n) * blk, blk)
        units = []
        for pair in range(hp // 2):
            units.append((pair, False, load_unit(pair, rf, qf_ref, vf_ref, lf_ref, kf_ref, sf_ref)))
            units.append((pair, True, load_unit(pair, rb, qb_ref, vb_ref, lb_ref, kb_ref, sb_ref)))
        prepared = [prepare_unit(loaded, backward) for _, backward, loaded in units]
        att = [lax.dot_general(u["q_mid"], u["k_diag"], nt, preferred_element_type=F32) for u in prepared]
        inter = [lax.dot_general(u["q_in"], u["s_diag"], nt, preferred_element_type=F32) for u in prepared]
        upd = [[lax.dot_general(u["v"][:, j * dh:(j + 1) * dh], u["k_out"][:, j * dh:(j + 1) * dh], tn,
                                preferred_element_type=F32) for j in range(2)] for u in prepared]
        att = [jnp.where(u["mask"], s, 0.0).astype(BF16) for u, s in zip(prepared, att)]
        intra = [jnp.dot(s, u["v_diag"], preferred_element_type=F32) for u, s in zip(prepared, att)]
        for i, (pair, backward, _) in enumerate(units):
            u = prepared[i]
            lanes = slice(pair * pw, (pair + 1) * pw)
            o = intra[i] + inter[i]
            states = [st * u["decay"][:, j * dh:(j + 1) * dh] + upd[i][j] for j, st in enumerate(u["states"])]
            if backward:
                accb_ref[pl.ds(pl.multiple_of((nc - 1 - c) * chunk + rb, blk), blk), lanes] = o
                sb_ref[2 * pair], sb_ref[2 * pair + 1] = states
            else:
                accf_ref[pl.ds(pl.multiple_of(c * chunk + rf, blk), blk), lanes] = o
                sf_ref[2 * pair], sf_ref[2 * pair + 1] = states
        return carry

    lax.fori_loop(0, nblk, body, 0)

    @pl.when(c == nc - 1)
    def _():
        fin = min(seq, 256)

        def finish(t, carry):
            rows = pl.ds(pl.multiple_of(t * fin, fin), fin)
            for h in range(hp):
                lanes = slice(h * dh, (h + 1) * dh)
                o = accf_ref[rows, lanes] + accb_ref[rows, lanes]
                o = o * lax.rsqrt(jnp.mean(o * o, axis=-1, keepdims=True) + RMS_EPS)
                o = o * nw_ref[:, lanes] * g_ref[0, rows, lanes].astype(F32)
                o_ref[0, rows, lanes] = o.astype(o_ref.dtype)
            return carry

        lax.fori_loop(0, seq // fin, finish, 0)


def _hgrn(qg3, plain3, logf3, key3, norm_w):
    batch, seq, _ = qg3.shape
    hp = HGRN_HEADS_PER_STEP
    w = hp * HGRN_HEAD_DIM
    chunk = min(seq, HGRN_CHUNK)
    nc = seq // chunk

    def fwd(offset):
        base = offset // w
        return pl.BlockSpec((1, chunk, w), lambda b, h, c: (b, c, base + h))

    def bwd(offset):
        base = offset // w
        return pl.BlockSpec((1, chunk, w), lambda b, h, c: (b, nc - 1 - c, base + h))

    gbase = HGRN_WIDTH // w
    return pl.pallas_call(
        _hgrn_kernel,
        grid=(batch, HGRN_HEADS // hp, nc),
        in_specs=[fwd(0), fwd(PLAIN_HI), fwd(0), fwd(0),
                  bwd(0), bwd(PLAIN_HI), bwd(HGRN_WIDTH), bwd(HGRN_WIDTH),
                  pl.BlockSpec((1, seq, w), lambda b, h, c: (b, 0, gbase + h)),
                  pl.BlockSpec((1, w), lambda b, h, c: (0, h))],
        out_specs=pl.BlockSpec((1, seq, w), lambda b, h, c: (b, 0, h)),
        out_shape=jax.ShapeDtypeStruct((batch, seq, HGRN_WIDTH), BF16),
        scratch_shapes=[pltpu.VMEM((seq, w), F32), pltpu.VMEM((seq, w), F32),
                        pltpu.VMEM((hp, HGRN_HEAD_DIM, HGRN_HEAD_DIM), F32),
                        pltpu.VMEM((hp, HGRN_HEAD_DIM, HGRN_HEAD_DIM), F32)],
        compiler_params=_params("parallel", "parallel", "arbitrary"),
        name="hgrn",
    )(qg3, plain3, logf3, key3, qg3, plain3, logf3, key3, qg3, norm_w)


def _pool_kernel(prev_ref, cur_ref, next_ref, pw_ref, ps_ref, o_ref, *, seq_len):
    i = pl.program_id(1)
    last = pl.num_programs(1) - 1
    ts = cur_ref.shape[1]
    n = ts + 2 * POOL_HALO
    pos = i * ts + lax.broadcasted_iota(jnp.int32, (ts, 1), 0)
    for g in range(POOL_GROUPS):
        lanes = slice(g * POOL_GROUP_DIM, (g + 1) * POOL_GROUP_DIM)
        half = POOL_WINDOWS[g] // 2
        cur = cur_ref[0, :, lanes].astype(F32)
        prev = jnp.where(i > 0, prev_ref[0, :, lanes].astype(F32), 0.0)
        nxt = jnp.where(i < last, next_ref[0, :, lanes].astype(F32), 0.0)
        ext = jnp.concatenate([prev, cur, nxt], axis=0)
        w = ext + pltpu.roll(ext, 1, 0)
        step = 1
        while step < half:
            w = pltpu.roll(w, step, 0) + pltpu.roll(w, n - step, 0)
            step *= 2
        wsum = w[POOL_HALO:POOL_HALO + ts]
        count = (jnp.minimum(pos + half, seq_len) - jnp.maximum(pos - half, 0)).astype(F32)
        mixed = wsum / count - cur
        y = jnp.dot(mixed.astype(BF16), pw_ref[g], preferred_element_type=F32)
        o_ref[0, :, lanes] = (y * ps_ref[:, lanes]).astype(o_ref.dtype)


def _pool(plain3, pool_w, pool_scale):
    batch, seq, _ = plain3.shape
    ts = min(seq, 512)
    hb = ts // POOL_HALO
    nhalo = seq // POOL_HALO
    kern = functools.partial(_pool_kernel, seq_len=seq)
    return pl.pallas_call(
        kern,
        grid=(batch, seq // ts),
        in_specs=[pl.BlockSpec((1, POOL_HALO, D_MODEL), lambda b, i: (b, jnp.maximum(i * hb - 1, 0), 0)),
                  pl.BlockSpec((1, ts, D_MODEL), lambda b, i: (b, i, 0)),
                  pl.BlockSpec((1, POOL_HALO, D_MODEL), lambda b, i: (b, jnp.minimum((i + 1) * hb, nhalo - 1), 0)),
                  pl.BlockSpec((POOL_GROUPS, POOL_GROUP_DIM, POOL_GROUP_DIM), lambda b, i: (0, 0, 0)),
                  pl.BlockSpec((1, D_MODEL), lambda b, i: (0, 0))],
        out_specs=pl.BlockSpec((1, ts, D_MODEL), lambda b, i: (b, i, 0)),
        out_shape=jax.ShapeDtypeStruct((batch, seq, D_MODEL), BF16),
        compiler_params=_params("parallel", "parallel"),
        name="pool",
    )(plain3, plain3, plain3, pool_w, pool_scale)


def _merge_kernel(ya_ref, o1_ref, o2_ref, o3_ref, l1_ref, l2_ref, l3_ref, hc_ref, gate_ref, x_ref,
                  wa_ref, wb_ref, wc_ref, wo_ref, g_ref, b_ref, xo_ref, xb_ref, *, alpha):
    l1, l2, l3 = l1_ref[...], l2_ref[...], l3_ref[...]
    m = jnp.maximum(jnp.maximum(l1, l2), l3)
    e1, e2, e3 = jnp.exp(l1 - m), jnp.exp(l2 - m), jnp.exp(l3 - m)
    att = (e1 * o1_ref[...].astype(F32) + e2 * o2_ref[...].astype(F32)
           + e3 * o3_ref[...].astype(F32)) / (e1 + e2 + e3)
    y_a = jnp.dot(ya_ref[...], wa_ref[...], preferred_element_type=F32)
    y_b = jnp.dot(att.astype(BF16), wb_ref[...], preferred_element_type=F32)
    y_c = jnp.dot(hc_ref[...], wc_ref[...], preferred_element_type=F32)
    d = D_MODEL
    merged = (gate_ref[:, 0:d].astype(F32) * y_a + gate_ref[:, d:2 * d].astype(F32) * y_b
              + gate_ref[:, 2 * d:3 * d].astype(F32) * y_c)
    mix = jnp.dot(merged.astype(BF16), wo_ref[...], preferred_element_type=F32)
    x1 = _layer_norm(alpha * x_ref[...] + mix, g_ref[...], b_ref[...])
    xo_ref[...] = x1
    xb_ref[...] = x1.astype(BF16)


def _merge(y_pool, o_list, l_list, y_hgrn, gates, x, wa, wb, wc, wo, ln_g, ln_b, alpha):
    T = x.shape[0]
    tm = min(T, 256)
    gw = ATTN_GROUP_WIDTH
    d = D_MODEL
    row = lambda w: pl.BlockSpec((tm, w), lambda i: (i, 0))
    const = lambda a: pl.BlockSpec(a.shape, lambda i: (0,) * a.ndim, pipeline_mode=pl.Buffered(1))
    kern = functools.partial(_merge_kernel, alpha=alpha)
    return pl.pallas_call(
        kern,
        grid=(T // tm,),
        in_specs=[row(d), row(gw), row(gw), row(gw), row(gw), row(gw), row(gw), row(d), row(3 * d),
                  row(d), const(wa), const(wb), const(wc), const(wo), const(ln_g), const(ln_b)],
        out_specs=[row(d), row(d)],
        out_shape=[jax.ShapeDtypeStruct((T, d), F32), jax.ShapeDtypeStruct((T, d), BF16)],
        compiler_params=_params("parallel"),
        name="merge",
    )(y_pool, *o_list, *l_list, y_hgrn, gates, x, wa, wb, wc, wo, ln_g, ln_b)


def _ffn_kernel(xb_ref, x_ref, p_ref, wg_ref, wu_ref, wd_ref, wpp_ref, wpg_ref, g_ref, b_ref,
                xo_ref, xbo_ref, *, alpha):
    xb = xb_ref[...]
    gate = jnp.dot(xb, wg_ref[...], preferred_element_type=F32)
    up = jnp.dot(xb, wu_ref[...], preferred_element_type=F32)
    hidden = (gate * _sigmoid(gate) * up).astype(BF16)
    ffn = jnp.dot(hidden, wd_ref[...], preferred_element_type=F32)
    ple = jnp.dot(p_ref[...].astype(BF16), wpp_ref[...], preferred_element_type=F32)
    ple = ple * _sigmoid(jnp.dot(xb, wpg_ref[...], preferred_element_type=F32))
    x2 = _layer_norm(alpha * x_ref[...] + ffn + ple, g_ref[...], b_ref[...])
    xo_ref[...] = x2
    xbo_ref[...] = x2.astype(BF16)


def _ffn(x1_bf, x1, p, wg, wu, wd, wpp, wpg, ln_g, ln_b, alpha):
    T = x1.shape[0]
    tm = min(T, 256)
    d = D_MODEL
    row = lambda w: pl.BlockSpec((tm, w), lambda i: (i, 0))
    const = lambda a: pl.BlockSpec(a.shape, lambda i: (0,) * a.ndim, pipeline_mode=pl.Buffered(1))
    kern = functools.partial(_ffn_kernel, alpha=alpha)
    return pl.pallas_call(
        kern,
        grid=(T // tm,),
        in_specs=[row(d), row(d), row(PLE_DIM), const(wg), const(wu), const(wd), const(wpp),
                  const(wpg), const(ln_g), const(ln_b)],
        out_specs=[row(d), row(d)],
        out_shape=[jax.ShapeDtypeStruct((T, d), F32), jax.ShapeDtypeStruct((T, d), BF16)],
        compiler_params=_params("parallel"),
        name="ffn",
    )(x1_bf, x1, p, wg, wu, wd, wpp, wpg, ln_g, ln_b)


def _split_in_weights(w_in):
    d, aw, gw, hw = D_MODEL, ATTN_HEADS * ATTN_HEAD_DIM, ATTN_GROUP_WIDTH, HGRN_WIDTH
    pool = w_in[:, :, 0:d]
    aq = w_in[:, :, d:d + aw]
    ak = w_in[:, :, d + aw:d + 2 * aw]
    av = w_in[:, :, d + 2 * aw:d + 3 * aw]
    h0 = d + 3 * aw
    hq, hi, hff, hfb, hg = [w_in[:, :, h0 + n * hw:h0 + (n + 1) * hw] for n in range(5)]
    gates = w_in[:, :, h0 + 5 * hw:]
    qkv = lambda g: [t[:, :, g * gw:(g + 1) * gw] for t in (aq, ak, av)]
    cat = lambda parts: jnp.concatenate(parts, axis=-1).astype(BF16)
    return (cat([pool, hi] + qkv(0)), cat([hq, hg]), cat([hff, hfb]), gates.astype(BF16),
            cat(qkv(1)), cat(qkv(2)))


def kernel(x, p, w_in, pool_w, pool_scale, w_branch_a, w_branch_b, w_branch_c, hgrn_lb_logits, hgrn_norm_w, w_out, ln1_g, ln1_b, w_ffn_gate, w_ffn_up, w_ffn_down, w_ple_proj, w_ple_gate, ln2_g, ln2_b):
    batch, seq, d = x.shape
    depth = w_in.shape[0]
    T = batch * seq
    alpha = float((2 * depth) ** 0.25)

    lb = jnp.cumsum(jax.nn.softmax(hgrn_lb_logits.astype(F32), axis=0), axis=0)
    lb = lb - lb[:1]

    w_plain, w_silu, w_forget, w_gates, w_g2, w_g3 = _split_in_weights(w_in)
    bf = lambda a: a.astype(BF16)
    pool_w_b, wa, wb, wc, wo = bf(pool_w), bf(w_branch_a), bf(w_branch_b), bf(w_branch_c), bf(w_out)
    wg, wu, wd, wpp, wpg = bf(w_ffn_gate), bf(w_ffn_up), bf(w_ffn_down), bf(w_ple_proj), bf(w_ple_gate)

    slopes = [2.0 ** (-ALIBI_MAX_BIAS * (i + 1) / ATTN_HEADS) for i in range(ATTN_HEADS)]
    hpg = ATTN_HEADS_PER_GROUP
    gw = ATTN_GROUP_WIDTH

    xf = x.reshape(T, d).astype(F32)
    xb = xf.astype(BF16)
    for i in range(depth):
        plain = _inproj(xb, w_plain[i], "plain", PLAIN_WIDTH // 2)
        qg = _inproj(xb, w_silu[i], "silu", HGRN_WIDTH)
        logf, key = _inproj(xb, w_forget[i], "forget", HGRN_WIDTH, lb=lb[i:i + 1])
        gates = _inproj(xb, w_gates[i], "sigmoid", D_MODEL)
        plain3 = plain.reshape(batch, seq, PLAIN_WIDTH)
        o_list, l_list = [], []
        for g, (_, dil) in enumerate(ATTN_GROUPS):
            gs = tuple(slopes[g * hpg:(g + 1) * hpg])
            if dil == 1:
                qkv = plain.reshape(batch, 1, seq, PLAIN_WIDTH)
                cols = (PLAIN_AQ // gw, PLAIN_AK // gw, PLAIN_AV // gw)
            else:
                qkv = _inproj_dilated(xf, (w_g2, w_g3)[g - 1][i], batch, seq, dil)
                cols = (0, 1, 2)
            o, l = _attention_group(qkv, qkv, qkv, cols, batch, seq, dil, gs)
            o_list.append(o)
            l_list.append(l)
        y_hgrn = _hgrn(qg.reshape(batch, seq, 2 * HGRN_WIDTH), plain3,
                       logf.reshape(batch, seq, 2 * HGRN_WIDTH), key.reshape(batch, seq, 2 * HGRN_WIDTH),
                       hgrn_norm_w[i:i + 1])
        y_pool = _pool(plain3, pool_w_b[i], pool_scale[i:i + 1])
        x1, x1b = _merge(y_pool.reshape(T, d), o_list, l_list, y_hgrn.reshape(T, d), gates, xf,
                         wa[i], wb[i], wc[i], wo[i], ln1_g[i:i + 1], ln1_b[i:i + 1], alpha)
        xf, xb = _ffn(x1b, x1, p[i].reshape(T, PLE_DIM), wg[i], wu[i], wd[i], wpp[i], wpg[i],
                      ln2_g[i:i + 1], ln2_b[i:i + 1], alpha)
    return xf.reshape(batch, seq, d).astype(x.dtype)
```

```python
import functools

import jax
import jax.numpy as jnp
from jax import lax
from jax.experimental import pallas as pl
from jax.experimental.pallas import tpu as pltpu

F32 = jnp.float32
BF16 = jnp.bfloat16
LANES = 128
LOG2_E = 1.4426950408889634
LN_2 = 0.6931471805599453

D_MODEL = 1024
PLE_DIM = 256
POOL_WINDOWS = (2, 4, 8, 16)
POOL_GROUPS = 4
POOL_GROUP_DIM = D_MODEL // POOL_GROUPS
ATTN_GROUPS = ((128, 1), (512, 4), (2048, 16))
ATTN_HEADS_PER_GROUP = 4
ATTN_HEADS = ATTN_HEADS_PER_GROUP * len(ATTN_GROUPS)
ATTN_HEAD_DIM = 128
ATTN_GROUP_WIDTH = ATTN_HEADS_PER_GROUP * ATTN_HEAD_DIM
ATTN_SIDE = 64
ALIBI_MAX_BIAS = 8.0
NEG_INF = -1e30
HGRN_HEADS = 8
HGRN_HEAD_DIM = 128
HGRN_WIDTH = HGRN_HEADS * HGRN_HEAD_DIM
FFN_HIDDEN = 2816
LN_EPS = 1e-5
RMS_EPS = 1e-6

PLAIN_POOL = 0
PLAIN_HI = 1024
PLAIN_AQ = 2048
PLAIN_AK = 2560
PLAIN_AV = 3072
PLAIN_WIDTH = 3584
QKV_WIDTH = 3 * ATTN_GROUP_WIDTH

ATTN_QUERY_BLOCK = 128
ATTN_KEY_WINDOW = ATTN_QUERY_BLOCK + 2 * ATTN_SIDE
ATTN_TOKENS_PER_STEP = 2048
ATTN_BLOCKS_PER_ITER = 2
ATTN_WINDOW_OFFSETS = 3
HGRN_BLOCK = 64
HGRN_HEADS_PER_STEP = 4
HGRN_CHUNK = 512
POOL_HALO = 16
VMEM_LIMIT = 52 * 1024 * 1024


def _params(*semantics):
    return pltpu.CompilerParams(dimension_semantics=semantics, vmem_limit_bytes=VMEM_LIMIT)


def _sigmoid(x):
    return 0.5 * jnp.tanh(0.5 * x) + 0.5


def _layer_norm(h, g, b):
    mu = jnp.mean(h, axis=-1, keepdims=True)
    d = h - mu
    var = jnp.mean(d * d, axis=-1, keepdims=True)
    return d * lax.rsqrt(var + LN_EPS) * g + b


def _inproj_kernel(x_ref, w_ref, *rest, epilogue):
    acc = jnp.dot(x_ref[...], w_ref[...], preferred_element_type=F32)
    if epilogue == "forget":
        lb_ref, logf_ref, key_ref = rest
        lb = lb_ref[...]
        sig = _sigmoid(acc)
        logf_ref[...] = jnp.log(lb + (1.0 - lb) * sig)
        key_ref[...] = ((1.0 - lb) * (1.0 - sig)).astype(key_ref.dtype)
        return
    (o_ref,) = rest
    if epilogue == "silu":
        acc = acc * _sigmoid(acc)
    o_ref[...] = acc.astype(o_ref.dtype)


def _inproj(x_bf, w, epilogue, tn, lb=None):
    T, D = x_bf.shape
    N = w.shape[1]
    tm = min(T, 1024)
    in_specs = [pl.BlockSpec((tm, D), lambda j, i: (i, 0)),
                pl.BlockSpec((D, tn), lambda j, i: (0, j))]
    out_block = pl.BlockSpec((tm, tn), lambda j, i: (i, j))
    args = [x_bf, w]
    if epilogue == "forget":
        in_specs.append(pl.BlockSpec((1, tn), lambda j, i: (0, j)))
        args.append(lb)
        out_specs = [out_block, out_block]
        out_shape = [jax.ShapeDtypeStruct((T, N), F32), jax.ShapeDtypeStruct((T, N), BF16)]
    else:
        out_specs = out_block
        out_shape = jax.ShapeDtypeStruct((T, N), BF16)
    return pl.pallas_call(
        functools.partial(_inproj_kernel, epilogue=epilogue),
        grid=(N // tn, T // tm),
        in_specs=in_specs,
        out_specs=out_specs,
        out_shape=out_shape,
        compiler_params=_params("parallel", "parallel"),
        name="inproj_" + epilogue,
    )(*args)


def _inproj_dilated_kernel(x_ref, w_ref, o_ref, slab_ref, xp_ref, *, dilation):
    n = x_ref.shape[0] // dilation
    for c in range(D_MODEL // LANES):
        slab_ref[c] = x_ref[:, c * LANES:(c + 1) * LANES]
    for r in range(dilation):
        for c in range(D_MODEL // LANES):
            xp_ref[r * n:(r + 1) * n, c * LANES:(c + 1) * LANES] = (
                slab_ref[c, pl.ds(r, n, stride=dilation), :].astype(BF16))
    res = jnp.dot(xp_ref[...], w_ref[...], preferred_element_type=F32).astype(o_ref.dtype)
    for r in range(dilation):
        o_ref[0, r] = res[r * n:(r + 1) * n]


def _inproj_dilated(x_f32, w, batch, seq, dilation):
    N = w.shape[1]
    tb = min(seq, 1024)
    steps = seq // tb
    n = tb // dilation
    return pl.pallas_call(
        functools.partial(_inproj_dilated_kernel, dilation=dilation),
        grid=(batch * steps,),
        in_specs=[pl.BlockSpec((tb, D_MODEL), lambda i: (i, 0)),
                  pl.BlockSpec((D_MODEL, N), lambda i: (0, 0))],
        out_specs=pl.BlockSpec((1, dilation, n, N), lambda i: (i // steps, 0, i % steps, 0)),
        out_shape=jax.ShapeDtypeStruct((batch, dilation, seq // dilation, N), BF16),
        scratch_shapes=[pltpu.VMEM((D_MODEL // LANES, tb, LANES), F32), pltpu.VMEM((tb, D_MODEL), BF16)],
        compiler_params=_params("parallel"),
        name="inproj_dilated%d" % dilation,
    )(x_f32, w)


def _attn_kernel(q_ref, k_ref, v_ref, o_ref, l_ref, oacc_ref, lacc_ref, bias_ref, *, seq_len, dilation, slopes):
    c = pl.program_id(1)
    lq = q_ref.shape[2]
    qb, kw = ATTN_QUERY_BLOCK, ATTN_KEY_WINDOW
    nq = lq // qb
    per_iter = ATTN_BLOCKS_PER_ITER
    scale2 = ATTN_HEAD_DIM ** -0.5 * LOG2_E
    heads = range(ATTN_HEADS_PER_GROUP)
    lanes = [slice(h * ATTN_HEAD_DIM, (h + 1) * ATTN_HEAD_DIM) for h in heads]

    @pl.when((pl.program_id(0) == 0) & (c == 0))
    def _():
        row = lax.broadcasted_iota(jnp.int32, (qb, kw), 0)
        col = lax.broadcasted_iota(jnp.int32, (qb, kw), 1)
        for w in range(ATTN_WINDOW_OFFSETS):
            dist = jnp.abs(col - row - w * ATTN_SIDE)
            distf = dist.astype(F32) * float(dilation)
            for h in heads:
                bias_ref[h, w] = jnp.where(dist <= ATTN_SIDE, distf * (-slopes[h] * LOG2_E), NEG_INF * LOG2_E)

    def body(it, carry):
        blocks = []
        for u in range(per_iter):
            idx = it * per_iter + u
            r = idx // nq
            r0 = pl.multiple_of((idx % nq) * qb, qb)
            t0 = c * lq + r0
            ks = pl.multiple_of(jnp.clip(t0 - ATTN_SIDE, 0, seq_len - kw), ATTN_SIDE)
            if dilation == 1:
                out_rows = pl.ds(r0, qb)
            else:
                out_rows = pl.ds(r0 * dilation + r, qb, stride=dilation)
            blocks.append(((t0 - ks) // ATTN_SIDE, out_rows, q_ref[0, r, pl.ds(r0, qb), :],
                           k_ref[0, r, pl.ds(ks, kw), :], v_ref[0, r, pl.ds(ks, kw), :]))
        scores = [[lax.dot_general(q[:, lanes[h]], k[:, lanes[h]], (((1,), (1,)), ((), ())),
                                   preferred_element_type=F32) for h in heads]
                  for _, _, q, k, _ in blocks]
        probs = []
        for (w, _, _, _, _), block_scores in zip(blocks, scores):
            block_probs = []
            for h in heads:
                s = block_scores[h] * scale2 + bias_ref[h, w]
                m = jnp.max(s, axis=1, keepdims=True)
                p = jnp.exp2(s - m)
                block_probs.append((p.astype(BF16), jnp.sum(p, axis=1, keepdims=True), m))
            probs.append(block_probs)
        pv = [[jnp.dot(block_probs[h][0], v[:, lanes[h]], preferred_element_type=F32) for h in heads]
              for (_, _, _, _, v), block_probs in zip(blocks, probs)]
        for (_, out_rows, _, _, _), block_probs, block_pv in zip(blocks, probs, pv):
            for h in heads:
                _, l, m = block_probs[h]
                oacc_ref[h, out_rows, :] = block_pv[h] / l
                lse = (m + jnp.log2(l)) * LN_2
                lacc_ref[h, out_rows, :] = jnp.broadcast_to(lse, (qb, ATTN_HEAD_DIM))
        return carry

    lax.fori_loop(0, dilation * nq // per_iter, body, 0)
    for h in heads:
        o_ref[0, :, lanes[h]] = oacc_ref[h].astype(o_ref.dtype)
        l_ref[0, :, lanes[h]] = lacc_ref[h]


def _attention_group(q_arr, k_arr, v_arr, cols, batch, seq, dilation, slopes):
    L = seq // dilation
    gw = ATTN_GROUP_WIDTH
    tc = min(seq, ATTN_TOKENS_PER_STEP)
    lq = tc // dilation
    kern = functools.partial(_attn_kernel, seq_len=L, dilation=dilation, slopes=slopes)
    qc, kc, vc = cols
    o, lse = pl.pallas_call(
        kern,
        grid=(batch, seq // tc),
        in_specs=[pl.BlockSpec((1, dilation, lq, gw), lambda b, c: (b, 0, c, qc)),
                  pl.BlockSpec((1, dilation, L, gw), lambda b, c: (b, 0, 0, kc)),
                  pl.BlockSpec((1, dilation, L, gw), lambda b, c: (b, 0, 0, vc))],
        out_specs=[pl.BlockSpec((1, tc, gw), lambda b, c: (b, c, 0)),
                   pl.BlockSpec((1, tc, gw), lambda b, c: (b, c, 0))],
        out_shape=[jax.ShapeDtypeStruct((batch, seq, gw), BF16),
                   jax.ShapeDtypeStruct((batch, seq, gw), F32)],
        scratch_shapes=[pltpu.VMEM((ATTN_HEADS_PER_GROUP, tc, ATTN_HEAD_DIM), F32),
                        pltpu.VMEM((ATTN_HEADS_PER_GROUP, tc, ATTN_HEAD_DIM), F32),
                        pltpu.VMEM((ATTN_HEADS_PER_GROUP, ATTN_WINDOW_OFFSETS, ATTN_QUERY_BLOCK, ATTN_KEY_WINDOW), F32)],
        compiler_params=_params("arbitrary", "arbitrary"),
        name="attn_dil%d" % dilation,
    )(q_arr, k_arr, v_arr)
    return o.reshape(batch * seq, gw), lse.reshape(batch * seq, gw)


def _running_sum_rows(x, row_index, reverse):
    n = x.shape[0]
    shift = 1
    while shift < n:
        if reverse:
            x = x + jnp.where(row_index < n - shift, pltpu.roll(x, n - shift, 0), 0.0)
        else:
            x = x + jnp.where(row_index >= shift, pltpu.roll(x, shift, 0), 0.0)
        shift *= 2
    return x


def _hgrn_kernel(qf_ref, vf_ref, lf_ref, kf_ref, qb_ref, vb_ref, lb_ref, kb_ref, g_ref, nw_ref, o_ref,
                 accf_ref, accb_ref, sf_ref, sb_ref):
    c = pl.program_id(2)
    nc = pl.num_programs(2)
    chunk = qf_ref.shape[1]
    seq = o_ref.shape[1]
    hp = HGRN_HEADS_PER_STEP
    blk = HGRN_BLOCK
    half = blk // 2
    nblk = chunk // blk
    dh = HGRN_HEAD_DIM
    pw = 2 * dh

    @pl.when(c == 0)
    def _():
        sf_ref[...] = jnp.zeros_like(sf_ref)
        sb_ref[...] = jnp.zeros_like(sb_ref)

    row_index = lax.broadcasted_iota(jnp.int32, (blk, pw), 0)
    ti = lax.broadcasted_iota(jnp.int32, (blk, 2 * blk), 0)
    si = lax.broadcasted_iota(jnp.int32, (blk, 2 * blk), 1) % blk
    causal = si <= ti
    anticausal = si >= ti
    zeros_kv = jnp.zeros((blk, dh), BF16)
    zeros_st = jnp.zeros((dh, dh), BF16)

    def block_diag(a, b, z):
        return jnp.concatenate([jnp.concatenate([a, z], axis=1), jnp.concatenate([z, b], axis=1)], axis=0)

    def load_unit(pair, r0, q_ref, v_ref, logf_ref, key_ref, s_ref):
        lanes = slice(pair * pw, (pair + 1) * pw)
        rows = pl.ds(r0, blk)
        return (q_ref[0, rows, lanes], v_ref[0, rows, lanes], logf_ref[0, rows, lanes], key_ref[0, rows, lanes],
                s_ref[2 * pair], s_ref[2 * pair + 1])

    def prepare_unit(loaded, backward):
        q, v, logf, k, st0, st1 = loaded
        q = q.astype(F32)
        k = k.astype(F32)
        a = _running_sum_rows(logf, row_index, backward)
        if backward:
            total = a[0:1, :]
            mid = a[half:half + 1, :]
        else:
            total = a[blk - 1:blk, :]
            mid = a[half - 1:half, :]
        q_mid = q * jnp.exp(a - mid)
        k_mid = k * jnp.exp(mid - a)
        k_mid_b = k_mid.astype(BF16)
        q_in = (q_mid * jnp.exp(mid)).astype(BF16)
        k_out = (k_mid * jnp.exp(total - mid)).astype(BF16)
        return dict(q_mid=q_mid.astype(BF16), k_diag=block_diag(k_mid_b[:, :dh], k_mid_b[:, dh:], zeros_kv),
                    q_in=q_in, s_diag=block_diag(st0.astype(BF16), st1.astype(BF16), zeros_st),
                    v=v, v_diag=block_diag(v[:, :dh], v[:, dh:], zeros_kv), k_out=k_out,
                    decay=jnp.exp(total), states=(st0, st1), mask=anticausal if backward else causal)

    nt = (((1,), (1,)), ((), ()))
    tn = (((0,), (0,)), ((), ()))

    def body(n, carry):
        rf = pl.multiple_of(n * blk, blk)
        rb = pl.multiple_of((nblk - 1 - n) * blk, blk)
        units = []
        for pair in range(hp // 2):
            units.append((pair, False, load_unit(pair, rf, qf_ref, vf_ref, lf_ref, kf_ref, sf_ref)))
            units.append((pair, True, load_unit(pair, rb, qb_ref, vb_ref, lb_ref, kb_ref, sb_ref)))
        prepared = [prepare_unit(loaded, backward) for _, backward, loaded in units]
        att = [lax.dot_general(u["q_mid"], u["k_diag"], nt, preferred_element_type=F32) for u in prepared]
        inter = [lax.dot_general(u["q_in"], u["s_diag"], nt, preferred_element_type=F32) for u in prepared]
        upd = [[lax.dot_general(u["v"][:, j * dh:(j + 1) * dh], u["k_out"][:, j * dh:(j + 1) * dh], tn,
                                preferred_element_type=F32) for j in range(2)] for u in prepared]
        att = [jnp.where(u["mask"], s, 0.0).astype(BF16) for u, s in zip(prepared, att)]
        intra = [jnp.dot(s, u["v_diag"], preferred_element_type=F32) for u, s in zip(prepared, att)]
        for i, (pair, backward, _) in enumerate(units):
            u = prepared[i]
            lanes = slice(pair * pw, (pair + 1) * pw)
            o = intra[i] + inter[i]
            states = [st * u["decay"][:, j * dh:(j + 1) * dh] + upd[i][j] for j, st in enumerate(u["states"])]
            if backward:
                accb_ref[pl.ds(pl.multiple_of((nc - 1 - c) * chunk + rb, blk), blk), lanes] = o
                sb_ref[2 * pair], sb_ref[2 * pair + 1] = states
            else:
                accf_ref[pl.ds(pl.multiple_of(c * chunk + rf, blk), blk), lanes] = o
                sf_ref[2 * pair], sf_ref[2 * pair + 1] = states
        return carry

    lax.fori_loop(0, nblk, body, 0)

    @pl.when(c == nc - 1)
    def _():
        fin = min(seq, 256)

        def finish(t, carry):
            rows = pl.ds(pl.multiple_of(t * fin, fin), fin)
            for h in range(hp):
                lanes = slice(h * dh, (h + 1) * dh)
                o = accf_ref[rows, lanes] + accb_ref[rows, lanes]
                o = o * lax.rsqrt(jnp.mean(o * o, axis=-1, keepdims=True) + RMS_EPS)
                o = o * nw_ref[:, lanes] * g_ref[0, rows, lanes].astype(F32)
                o_ref[0, rows, lanes] = o.astype(o_ref.dtype)
            return carry

        lax.fori_loop(0, seq // fin, finish, 0)


def _hgrn(qg3, plain3, logf3, key3, norm_w):
    batch, seq, _ = qg3.shape
    hp = HGRN_HEADS_PER_STEP
    w = hp * HGRN_HEAD_DIM
    chunk = min(seq, HGRN_CHUNK)
    nc = seq // chunk

    def fwd(offset):
        base = offset // w
        return pl.BlockSpec((1, chunk, w), lambda b, h, c: (b, c, base + h))

    def bwd(offset):
        base = offset // w
        return pl.BlockSpec((1, chunk, w), lambda b, h, c: (b, nc - 1 - c, base + h))

    gbase = HGRN_WIDTH // w
    return pl.pallas_call(
        _hgrn_kernel,
        grid=(batch, HGRN_HEADS // hp, nc),
        in_specs=[fwd(0), fwd(PLAIN_HI), fwd(0), fwd(0),
                  bwd(0), bwd(PLAIN_HI), bwd(HGRN_WIDTH), bwd(HGRN_WIDTH),
                  pl.BlockSpec((1, seq, w), lambda b, h, c: (b, 0, gbase + h)),
                  pl.BlockSpec((1, w), lambda b, h, c: (0, h))],
        out_specs=pl.BlockSpec((1, seq, w), lambda b, h, c: (b, 0, h)),
        out_shape=jax.ShapeDtypeStruct((batch, seq, HGRN_WIDTH), BF16),
        scratch_shapes=[pltpu.VMEM((seq, w), F32), pltpu.VMEM((seq, w), F32),
                        pltpu.VMEM((hp, HGRN_HEAD_DIM, HGRN_HEAD_DIM), F32),
                        pltpu.VMEM((hp, HGRN_HEAD_DIM, HGRN_HEAD_DIM), F32)],
        compiler_params=_params("parallel", "parallel", "arbitrary"),
        name="hgrn",
    )(qg3, plain3, logf3, key3, qg3, plain3, logf3, key3, qg3, norm_w)


def _pool_kernel(prev_ref, cur_ref, next_ref, pw_ref, ps_ref, o_ref, *, seq_len):
    i = pl.program_id(1)
    last = pl.num_programs(1) - 1
    ts = cur_ref.shape[1]
    n = ts + 2 * POOL_HALO
    pos = i * ts + lax.broadcasted_iota(jnp.int32, (ts, 1), 0)
    for g in range(POOL_GROUPS):
        lanes = slice(g * POOL_GROUP_DIM, (g + 1) * POOL_GROUP_DIM)
        half = POOL_WINDOWS[g] // 2
        cur = cur_ref[0, :, lanes].astype(F32)
        prev = jnp.where(i > 0, prev_ref[0, :, lanes].astype(F32), 0.0)
        nxt = jnp.where(i < last, next_ref[0, :, lanes].astype(F32), 0.0)
        ext = jnp.concatenate([prev, cur, nxt], axis=0)
        w = ext + pltpu.roll(ext, 1, 0)
        step = 1
        while step < half:
            w = pltpu.roll(w, step, 0) + pltpu.roll(w, n - step, 0)
            step *= 2
        wsum = w[POOL_HALO:POOL_HALO + ts]
        count = (jnp.minimum(pos + half, seq_len) - jnp.maximum(pos - half, 0)).astype(F32)
        mixed = wsum / count - cur
        y = jnp.dot(mixed.astype(BF16), pw_ref[g], preferred_element_type=F32)
        o_ref[0, :, lanes] = (y * ps_ref[:, lanes]).astype(o_ref.dtype)


def _pool(plain3, pool_w, pool_scale):
    batch, seq, _ = plain3.shape
    ts = min(seq, 512)
    hb = ts // POOL_HALO
    nhalo = seq // POOL_HALO
    kern = functools.partial(_pool_kernel, seq_len=seq)
    return pl.pallas_call(
        kern,
        grid=(batch, seq // ts),
        in_specs=[pl.BlockSpec((1, POOL_HALO, D_MODEL), lambda b, i: (b, jnp.maximum(i * hb - 1, 0), 0)),
                  pl.BlockSpec((1, ts, D_MODEL), lambda b, i: (b, i, 0)),
                  pl.BlockSpec((1, POOL_HALO, D_MODEL), lambda b, i: (b, jnp.minimum((i + 1) * hb, nhalo - 1), 0)),
                  pl.BlockSpec((POOL_GROUPS, POOL_GROUP_DIM, POOL_GROUP_DIM), lambda b, i: (0, 0, 0)),
                  pl.BlockSpec((1, D_MODEL), lambda b, i: (0, 0))],
        out_specs=pl.BlockSpec((1, ts, D_MODEL), lambda b, i: (b, i, 0)),
        out_shape=jax.ShapeDtypeStruct((batch, seq, D_MODEL), BF16),
        compiler_params=_params("parallel", "parallel"),
        name="pool",
    )(plain3, plain3, plain3, pool_w, pool_scale)


def _merge_kernel(ya_ref, o1_ref, o2_ref, o3_ref, l1_ref, l2_ref, l3_ref, hc_ref, xin_ref, x_ref,
                  wa_ref, wb_ref, wc_ref, wo_ref, wgate_ref, g_ref, b_ref, xo_ref, xb_ref, *, alpha):
    d = D_MODEL
    xin = xin_ref[...]
    y_a = jnp.dot(ya_ref[...], wa_ref[...], preferred_element_type=F32)
    y_c = jnp.dot(hc_ref[...], wc_ref[...], preferred_element_type=F32)
    gates = [_sigmoid(jnp.dot(xin, wgate_ref[:, n * d:(n + 1) * d], preferred_element_type=F32))
             for n in range(3)]
    l1, l2, l3 = l1_ref[...], l2_ref[...], l3_ref[...]
    m = jnp.maximum(jnp.maximum(l1, l2), l3)
    e1, e2, e3 = jnp.exp(l1 - m), jnp.exp(l2 - m), jnp.exp(l3 - m)
    att = (e1 * o1_ref[...].astype(F32) + e2 * o2_ref[...].astype(F32)
           + e3 * o3_ref[...].astype(F32)) / (e1 + e2 + e3)
    y_b = jnp.dot(att.astype(BF16), wb_ref[...], preferred_element_type=F32)
    merged = gates[0] * y_a + gates[1] * y_b + gates[2] * y_c
    mix = jnp.dot(merged.astype(BF16), wo_ref[...], preferred_element_type=F32)
    x1 = _layer_norm(alpha * x_ref[...] + mix, g_ref[...], b_ref[...])
    xo_ref[...] = x1
    xb_ref[...] = x1.astype(BF16)


def _merge(y_pool, o_list, l_list, y_hgrn, x_bf, x, wa, wb, wc, wo, wgate, ln_g, ln_b, alpha):
    T = x.shape[0]
    tm = min(T, 256)
    gw = ATTN_GROUP_WIDTH
    d = D_MODEL
    row = lambda w: pl.BlockSpec((tm, w), lambda i: (i, 0))
    const = lambda a: pl.BlockSpec(a.shape, lambda i: (0,) * a.ndim, pipeline_mode=pl.Buffered(1))
    kern = functools.partial(_merge_kernel, alpha=alpha)
    return pl.pallas_call(
        kern,
        grid=(T // tm,),
        in_specs=[row(d), row(gw), row(gw), row(gw), row(gw), row(gw), row(gw), row(d), row(d),
                  row(d), const(wa), const(wb), const(wc), const(wo), const(wgate), const(ln_g), const(ln_b)],
        out_specs=[row(d), row(d)],
        out_shape=[jax.ShapeDtypeStruct((T, d), F32), jax.ShapeDtypeStruct((T, d), BF16)],
        compiler_params=_params("parallel"),
        name="merge",
    )(y_pool, *o_list, *l_list, y_hgrn, x_bf, x, wa, wb, wc, wo, wgate, ln_g, ln_b)


def _ffn_kernel(xb_ref, x_ref, p_ref, wg_ref, wu_ref, wd_ref, wpp_ref, wpg_ref, g_ref, b_ref,
                xo_ref, xbo_ref, *, alpha):
    xb = xb_ref[...]
    gate = jnp.dot(xb, wg_ref[...], preferred_element_type=F32)
    up = jnp.dot(xb, wu_ref[...], preferred_element_type=F32)
    hidden = (gate * _sigmoid(gate) * up).astype(BF16)
    ffn = jnp.dot(hidden, wd_ref[...], preferred_element_type=F32)
    ple = jnp.dot(p_ref[...].astype(BF16), wpp_ref[...], preferred_element_type=F32)
    ple = ple * _sigmoid(jnp.dot(xb, wpg_ref[...], preferred_element_type=F32))
    x2 = _layer_norm(alpha * x_ref[...] + ffn + ple, g_ref[...], b_ref[...])
    xo_ref[...] = x2
    xbo_ref[...] = x2.astype(BF16)


def _ffn(x1_bf, x1, p, wg, wu, wd, wpp, wpg, ln_g, ln_b, alpha):
    T = x1.shape[0]
    tm = min(T, 256)
    d = D_MODEL
    row = lambda w: pl.BlockSpec((tm, w), lambda i: (i, 0))
    const = lambda a: pl.BlockSpec(a.shape, lambda i: (0,) * a.ndim, pipeline_mode=pl.Buffered(1))
    kern = functools.partial(_ffn_kernel, alpha=alpha)
    return pl.pallas_call(
        kern,
        grid=(T // tm,),
        in_specs=[row(d), row(d), row(PLE_DIM), const(wg), const(wu), const(wd), const(wpp),
                  const(wpg), const(ln_g), const(ln_b)],
        out_specs=[row(d), row(d)],
        out_shape=[jax.ShapeDtypeStruct((T, d), F32), jax.ShapeDtypeStruct((T, d), BF16)],
        compiler_params=_params("parallel"),
        name="ffn",
    )(x1_bf, x1, p, wg, wu, wd, wpp, wpg, ln_g, ln_b)


def _split_in_weights(w_in):
    d, aw, gw, hw = D_MODEL, ATTN_HEADS * ATTN_HEAD_DIM, ATTN_GROUP_WIDTH, HGRN_WIDTH
    pool = w_in[:, :, 0:d]
    aq = w_in[:, :, d:d + aw]
    ak = w_in[:, :, d + aw:d + 2 * aw]
    av = w_in[:, :, d + 2 * aw:d + 3 * aw]
    h0 = d + 3 * aw
    hq, hi, hff, hfb, hg = [w_in[:, :, h0 + n * hw:h0 + (n + 1) * hw] for n in range(5)]
    gates = w_in[:, :, h0 + 5 * hw:]
    qkv = lambda g: [t[:, :, g * gw:(g + 1) * gw] for t in (aq, ak, av)]
    cat = lambda parts: jnp.concatenate(parts, axis=-1).astype(BF16)
    return (cat([pool, hi] + qkv(0)), cat([hq, hg]), cat([hff, hfb]), gates.astype(BF16),
            cat(qkv(1)), cat(qkv(2)))


def kernel(x, p, w_in, pool_w, pool_scale, w_branch_a, w_branch_b, w_branch_c, hgrn_lb_logits, hgrn_norm_w, w_out, ln1_g, ln1_b, w_ffn_gate, w_ffn_up, w_ffn_down, w_ple_proj, w_ple_gate, ln2_g, ln2_b):
    batch, seq, d = x.shape
    depth = w_in.shape[0]
    T = batch * seq
    alpha = float((2 * depth) ** 0.25)

    lb = jnp.cumsum(jax.nn.softmax(hgrn_lb_logits.astype(F32), axis=0), axis=0)
    lb = lb - lb[:1]

    w_plain, w_silu, w_forget, w_gates, w_g2, w_g3 = _split_in_weights(w_in)
    bf = lambda a: a.astype(BF16)
    pool_w_b, wa, wb, wc, wo = bf(pool_w), bf(w_branch_a), bf(w_branch_b), bf(w_branch_c), bf(w_out)
    wg, wu, wd, wpp, wpg = bf(w_ffn_gate), bf(w_ffn_up), bf(w_ffn_down), bf(w_ple_proj), bf(w_ple_gate)

    slopes = [2.0 ** (-ALIBI_MAX_BIAS * (i + 1) / ATTN_HEADS) for i in range(ATTN_HEADS)]
    hpg = ATTN_HEADS_PER_GROUP
    gw = ATTN_GROUP_WIDTH

    xf = x.reshape(T, d).astype(F32)
    xb = xf.astype(BF16)
    for i in range(depth):
        plain = _inproj(xb, w_plain[i], "plain", PLAIN_WIDTH // 2)
        qg = _inproj(xb, w_silu[i], "silu", HGRN_WIDTH)
        logf, key = _inproj(xb, w_forget[i], "forget", HGRN_WIDTH, lb=lb[i:i + 1])
        plain3 = plain.reshape(batch, seq, PLAIN_WIDTH)
        o_list, l_list = [], []
        for g, (_, dil) in enumerate(ATTN_GROUPS):
            gs = tuple(slopes[g * hpg:(g + 1) * hpg])
            if dil == 1:
                qkv = plain.reshape(batch, 1, seq, PLAIN_WIDTH)
                cols = (PLAIN_AQ // gw, PLAIN_AK // gw, PLAIN_AV // gw)
            else:
                qkv = _inproj_dilated(xf, (w_g2, w_g3)[g - 1][i], batch, seq, dil)
                cols = (0, 1, 2)
            o, l = _attention_group(qkv, qkv, qkv, cols, batch, seq, dil, gs)
            o_list.append(o)
            l_list.append(l)
        y_hgrn = _hgrn(qg.reshape(batch, seq, 2 * HGRN_WIDTH), plain3,
                       logf.reshape(batch, seq, 2 * HGRN_WIDTH), key.reshape(batch, seq, 2 * HGRN_WIDTH),
                       hgrn_norm_w[i:i + 1])
        y_pool = _pool(plain3, pool_w_b[i], pool_scale[i:i + 1])
        x1, x1b = _merge(y_pool.reshape(T, d), o_list, l_list, y_hgrn.reshape(T, d), xb, xf,
                         wa[i], wb[i], wc[i], wo[i], w_gates[i], ln1_g[i:i + 1], ln1_b[i:i + 1], alpha)
        xf, xb = _ffn(x1b, x1, p[i].reshape(T, PLE_DIM), wg[i], wu[i], wd[i], wpp[i], wpg[i],
                      ln2_g[i:i + 1], ln2_b[i:i + 1], alpha)
    return xf.reshape(batch, seq, d).astype(x.dtype)
```

```python
import functools

import jax
import jax.numpy as jnp
from jax import lax
from jax.experimental import pallas as pl
from jax.experimental.pallas import tpu as pltpu

F32 = jnp.float32
BF16 = jnp.bfloat16
LANES = 128
SUBLANES = 8
LOG2_E = 1.4426950408889634
LN_2 = 0.6931471805599453

D_MODEL = 1024
PLE_DIM = 256
POOL_WINDOWS = (2, 4, 8, 16)
POOL_GROUPS = 4
POOL_GROUP_DIM = D_MODEL // POOL_GROUPS
ATTN_GROUPS = ((128, 1), (512, 4), (2048, 16))
ATTN_HEADS_PER_GROUP = 4
ATTN_HEADS = ATTN_HEADS_PER_GROUP * len(ATTN_GROUPS)
ATTN_HEAD_DIM = 128
ATTN_GROUP_WIDTH = ATTN_HEADS_PER_GROUP * ATTN_HEAD_DIM
ATTN_SIDE = 64
ALIBI_MAX_BIAS = 8.0
NEG_INF = -1e30
HGRN_HEADS = 8
HGRN_HEAD_DIM = 128
HGRN_WIDTH = HGRN_HEADS * HGRN_HEAD_DIM
FFN_HIDDEN = 2816
LN_EPS = 1e-5
RMS_EPS = 1e-6

PLAIN_POOL = 0
PLAIN_HI = 1024
PLAIN_AQ = 2048
PLAIN_AK = 2560
PLAIN_AV = 3072
PLAIN_WIDTH = 3584
QKV_WIDTH = 3 * ATTN_GROUP_WIDTH

ATTN_QUERY_BLOCK = 128
ATTN_KEY_WINDOW = ATTN_QUERY_BLOCK + 2 * ATTN_SIDE
ATTN_TOKENS_PER_STEP = 2048
ATTN_BLOCKS_PER_ITER = 2
ATTN_WINDOW_OFFSETS = 3
HGRN_BLOCK = 64
HGRN_HEADS_PER_STEP = 4
HGRN_CHUNK = 512
FFN_ROWS = 512
FFN_SUBTILES = 2
POOL_HALO = 16
VMEM_LIMIT = 52 * 1024 * 1024


def _params(*semantics):
    return pltpu.CompilerParams(dimension_semantics=semantics, vmem_limit_bytes=VMEM_LIMIT)


def _sigmoid(x):
    return 0.5 * jnp.tanh(0.5 * x) + 0.5


def _layer_norm(h, g, b):
    mu = jnp.mean(h, axis=-1, keepdims=True)
    d = h - mu
    var = jnp.mean(d * d, axis=-1, keepdims=True)
    return d * lax.rsqrt(var + LN_EPS) * g + b


def _inproj_kernel(x_ref, w_ref, *rest, epilogue):
    acc = jnp.dot(x_ref[...], w_ref[...], preferred_element_type=F32)
    if epilogue == "forget":
        lb_ref, logf_ref, key_ref = rest
        lb = lb_ref[...]
        sig = _sigmoid(acc)
        logf_ref[...] = jnp.log(lb + (1.0 - lb) * sig)
        key_ref[...] = ((1.0 - lb) * (1.0 - sig)).astype(key_ref.dtype)
        return
    (o_ref,) = rest
    if epilogue == "silu":
        acc = acc * _sigmoid(acc)
    o_ref[...] = acc.astype(o_ref.dtype)


def _inproj(x_bf, w, epilogue, tn, lb=None):
    T, D = x_bf.shape
    N = w.shape[1]
    tm = min(T, 1024)
    in_specs = [pl.BlockSpec((tm, D), lambda j, i: (i, 0)),
                pl.BlockSpec((D, tn), lambda j, i: (0, j))]
    out_block = pl.BlockSpec((tm, tn), lambda j, i: (i, j))
    args = [x_bf, w]
    if epilogue == "forget":
        in_specs.append(pl.BlockSpec((1, tn), lambda j, i: (0, j)))
        args.append(lb)
        out_specs = [out_block, out_block]
        out_shape = [jax.ShapeDtypeStruct((T, N), F32), jax.ShapeDtypeStruct((T, N), BF16)]
    else:
        out_specs = out_block
        out_shape = jax.ShapeDtypeStruct((T, N), BF16)
    return pl.pallas_call(
        functools.partial(_inproj_kernel, epilogue=epilogue),
        grid=(N // tn, T // tm),
        in_specs=in_specs,
        out_specs=out_specs,
        out_shape=out_shape,
        compiler_params=_params("parallel", "parallel"),
        name="inproj_" + epilogue,
    )(*args)


def _inproj_dilated_kernel(x_ref, w_ref, o_ref, slab_ref, xp_ref, *, dilation):
    n = x_ref.shape[0] // dilation
    for c in range(D_MODEL // LANES):
        slab_ref[c] = x_ref[:, c * LANES:(c + 1) * LANES]
    for r in range(dilation):
        for c in range(D_MODEL // LANES):
            xp_ref[r * n:(r + 1) * n, c * LANES:(c + 1) * LANES] = (
                slab_ref[c, pl.ds(r, n, stride=dilation), :].astype(BF16))
    res = jnp.dot(xp_ref[...], w_ref[...], preferred_element_type=F32).astype(o_ref.dtype)
    for r in range(dilation):
        o_ref[0, r] = res[r * n:(r + 1) * n]


def _inproj_dilated(x_f32, w, batch, seq, dilation):
    N = w.shape[1]
    tb = min(seq, 1024)
    steps = seq // tb
    n = tb // dilation
    return pl.pallas_call(
        functools.partial(_inproj_dilated_kernel, dilation=dilation),
        grid=(batch * steps,),
        in_specs=[pl.BlockSpec((tb, D_MODEL), lambda i: (i, 0)),
                  pl.BlockSpec((D_MODEL, N), lambda i: (0, 0))],
        out_specs=pl.BlockSpec((1, dilation, n, N), lambda i: (i // steps, 0, i % steps, 0)),
        out_shape=jax.ShapeDtypeStruct((batch, dilation, seq // dilation, N), BF16),
        scratch_shapes=[pltpu.VMEM((D_MODEL // LANES, tb, LANES), F32), pltpu.VMEM((tb, D_MODEL), BF16)],
        compiler_params=_params("parallel"),
        name="inproj_dilated%d" % dilation,
    )(x_f32, w)


def _attn_kernel(q_ref, k_ref, v_ref, o_ref, l_ref, oacc_ref, lacc_ref, bias_ref, *, seq_len, dilation, slopes):
    c = pl.program_id(1)
    lq = q_ref.shape[2]
    qb, kw = ATTN_QUERY_BLOCK, ATTN_KEY_WINDOW
    nq = lq // qb
    per_iter = ATTN_BLOCKS_PER_ITER
    scale2 = ATTN_HEAD_DIM ** -0.5 * LOG2_E
    heads = range(ATTN_HEADS_PER_GROUP)
    lanes = [slice(h * ATTN_HEAD_DIM, (h + 1) * ATTN_HEAD_DIM) for h in heads]

    @pl.when((pl.program_id(0) == 0) & (c == 0))
    def _():
        row = lax.broadcasted_iota(jnp.int32, (qb, kw), 0)
        col = lax.broadcasted_iota(jnp.int32, (qb, kw), 1)
        for w in range(ATTN_WINDOW_OFFSETS):
            dist = jnp.abs(col - row - w * ATTN_SIDE)
            distf = dist.astype(F32) * float(dilation)
            for h in heads:
                bias_ref[h, w] = jnp.where(dist <= ATTN_SIDE, distf * (-slopes[h] * LOG2_E), NEG_INF * LOG2_E)

    def body(it, carry):
        blocks = []
        for u in range(per_iter):
            idx = it * per_iter + u
            r = idx // nq
            r0 = pl.multiple_of((idx % nq) * qb, qb)
            t0 = c * lq + r0
            ks = pl.multiple_of(jnp.clip(t0 - ATTN_SIDE, 0, seq_len - kw), ATTN_SIDE)
            if dilation == 1:
                out_rows = pl.ds(r0, qb)
            else:
                out_rows = pl.ds(r0 * dilation + r, qb, stride=dilation)
            blocks.append(((t0 - ks) // ATTN_SIDE, out_rows, q_ref[0, r, pl.ds(r0, qb), :],
                           k_ref[0, r, pl.ds(ks, kw), :], v_ref[0, r, pl.ds(ks, kw), :]))
        scores = [[lax.dot_general(q[:, lanes[h]], k[:, lanes[h]], (((1,), (1,)), ((), ())),
                                   preferred_element_type=F32) for h in heads]
                  for _, _, q, k, _ in blocks]
        probs = []
        for (w, _, _, _, _), block_scores in zip(blocks, scores):
            block_probs = []
            for h in heads:
                s = block_scores[h] * scale2 + bias_ref[h, w]
                m = jnp.max(s, axis=1, keepdims=True)
                p = jnp.exp2(s - m)
                block_probs.append((p.astype(BF16), jnp.sum(p, axis=1, keepdims=True), m))
            probs.append(block_probs)
        pv = [[jnp.dot(block_probs[h][0], v[:, lanes[h]], preferred_element_type=F32) for h in heads]
              for (_, _, _, _, v), block_probs in zip(blocks, probs)]
        for (_, out_rows, _, _, _), block_probs, block_pv in zip(blocks, probs, pv):
            for h in heads:
                _, l, m = block_probs[h]
                oacc_ref[h, out_rows, :] = block_pv[h] / l
                lse = (m + jnp.log2(l)) * LN_2
                lacc_ref[h, out_rows, :] = jnp.broadcast_to(lse, (qb, ATTN_HEAD_DIM))
        return carry

    lax.fori_loop(0, dilation * nq // per_iter, body, 0)
    for h in heads:
        o_ref[0, :, lanes[h]] = oacc_ref[h].astype(o_ref.dtype)
        l_ref[0, :, lanes[h]] = lacc_ref[h]


def _attention_group(q_arr, k_arr, v_arr, cols, batch, seq, dilation, slopes):
    L = seq // dilation
    gw = ATTN_GROUP_WIDTH
    tc = min(seq, ATTN_TOKENS_PER_STEP)
    lq = tc // dilation
    kern = functools.partial(_attn_kernel, seq_len=L, dilation=dilation, slopes=slopes)
    qc, kc, vc = cols
    o, lse = pl.pallas_call(
        kern,
        grid=(batch, seq // tc),
        in_specs=[pl.BlockSpec((1, dilation, lq, gw), lambda b, c: (b, 0, c, qc)),
                  pl.BlockSpec((1, dilation, L, gw), lambda b, c: (b, 0, 0, kc)),
                  pl.BlockSpec((1, dilation, L, gw), lambda b, c: (b, 0, 0, vc))],
        out_specs=[pl.BlockSpec((1, tc, gw), lambda b, c: (b, c, 0)),
                   pl.BlockSpec((1, tc, gw), lambda b, c: (b, c, 0))],
        out_shape=[jax.ShapeDtypeStruct((batch, seq, gw), BF16),
                   jax.ShapeDtypeStruct((batch, seq, gw), F32)],
        scratch_shapes=[pltpu.VMEM((ATTN_HEADS_PER_GROUP, tc, ATTN_HEAD_DIM), F32),
                        pltpu.VMEM((ATTN_HEADS_PER_GROUP, tc, ATTN_HEAD_DIM), F32),
                        pltpu.VMEM((ATTN_HEADS_PER_GROUP, ATTN_WINDOW_OFFSETS, ATTN_QUERY_BLOCK, ATTN_KEY_WINDOW), F32)],
        compiler_params=_params("arbitrary", "arbitrary"),
        name="attn_dil%d" % dilation,
    )(q_arr, k_arr, v_arr)
    return o.reshape(batch * seq, gw), lse.reshape(batch * seq, gw)


def _running_sum_rows(x, row_index, reverse):
    n = x.shape[0]
    shift = 1
    while shift < n:
        if reverse:
            x = x + jnp.where(row_index < n - shift, pltpu.roll(x, n - shift, 0), 0.0)
        else:
            x = x + jnp.where(row_index >= shift, pltpu.roll(x, shift, 0), 0.0)
        shift *= 2
    return x


def _hgrn_kernel(qf_ref, vf_ref, lf_ref, kf_ref, qb_ref, vb_ref, lb_ref, kb_ref, g_ref, nw_ref, o_ref,
                 accf_ref, accb_ref, sf_ref, sb_ref):
    c = pl.program_id(2)
    nc = pl.num_programs(2)
    chunk = qf_ref.shape[1]
    seq = o_ref.shape[1]
    hp = HGRN_HEADS_PER_STEP
    blk = HGRN_BLOCK
    half = blk // 2
    nblk = chunk // blk
    dh = HGRN_HEAD_DIM
    pw = 2 * dh

    @pl.when(c == 0)
    def _():
        sf_ref[...] = jnp.zeros_like(sf_ref)
        sb_ref[...] = jnp.zeros_like(sb_ref)

    row_index = lax.broadcasted_iota(jnp.int32, (blk, pw), 0)
    ti = lax.broadcasted_iota(jnp.int32, (blk, 2 * blk), 0)
    si = lax.broadcasted_iota(jnp.int32, (blk, 2 * blk), 1) % blk
    causal = si <= ti
    anticausal = si >= ti
    zeros_kv = jnp.zeros((blk, dh), BF16)
    zeros_st = jnp.zeros((dh, dh), BF16)

    def block_diag(a, b, z):
        return jnp.concatenate([jnp.concatenate([a, z], axis=1), jnp.concatenate([z, b], axis=1)], axis=0)

    def load_unit(pair, r0, q_ref, v_ref, logf_ref, key_ref, s_ref):
        lanes = slice(pair * pw, (pair + 1) * pw)
        rows = pl.ds(r0, blk)
        return (q_ref[0, rows, lanes], v_ref[0, rows, lanes], logf_ref[0, rows, lanes], key_ref[0, rows, lanes],
                s_ref[2 * pair], s_ref[2 * pair + 1])

    def prepare_unit(loaded, backward):
        q, v, logf, k, st0, st1 = loaded
        q = q.astype(F32)
        k = k.astype(F32)
        a = _running_sum_rows(logf, row_index, backward)
        if backward:
            total = a[0:1, :]
            mid = a[half:half + 1, :]
        else:
            total = a[blk - 1:blk, :]
            mid = a[half - 1:half, :]
        q_mid = q * jnp.exp(a - mid)
        k_mid = k * jnp.exp(mid - a)
        k_mid_b = k_mid.astype(BF16)
        q_in = (q_mid * jnp.exp(mid)).astype(BF16)
        k_out = (k_mid * jnp.exp(total - mid)).astype(BF16)
        decay = jnp.broadcast_to(jnp.exp(total), (SUBLANES, pw)).T[:, 0:1]
        return dict(q_mid=q_mid.astype(BF16), k_diag=block_diag(k_mid_b[:, :dh], k_mid_b[:, dh:], zeros_kv),
                    q_in=q_in, s_diag=block_diag(st0.astype(BF16), st1.astype(BF16), zeros_st),
                    v=v, v_diag=block_diag(v[:, :dh], v[:, dh:], zeros_kv), k_out=k_out,
                    decay=decay, states=(st0, st1), mask=anticausal if backward else causal)

    nt = (((1,), (1,)), ((), ()))
    tn = (((0,), (0,)), ((), ()))

    def body(n, carry):
        rf = pl.multiple_of(n * blk, blk)
        rb = pl.multiple_of((nblk - 1 - n) * blk, blk)
        units = []
        for pair in range(hp // 2):
            units.append((pair, False, load_unit(pair, rf, qf_ref, vf_ref, lf_ref, kf_ref, sf_ref)))
            units.append((pair, True, load_unit(pair, rb, qb_ref, vb_ref, lb_ref, kb_ref, sb_ref)))
        prepared = [prepare_unit(loaded, backward) for _, backward, loaded in units]
        att = [lax.dot_general(u["q_mid"], u["k_diag"], nt, preferred_element_type=F32) for u in prepared]
        inter = [jnp.dot(u["q_in"], u["s_diag"], preferred_element_type=F32) for u in prepared]
        upd = [[lax.dot_general(u["k_out"][:, j * dh:(j + 1) * dh], u["v"][:, j * dh:(j + 1) * dh], tn,
                                preferred_element_type=F32) for j in range(2)] for u in prepared]
        att = [jnp.where(u["mask"], s, 0.0).astype(BF16) for u, s in zip(prepared, att)]
        intra = [jnp.dot(s, u["v_diag"], preferred_element_type=F32) for u, s in zip(prepared, att)]
        for i, (pair, backward, _) in enumerate(units):
            u = prepared[i]
            lanes = slice(pair * pw, (pair + 1) * pw)
            o = intra[i] + inter[i]
            states = [st * u["decay"][j * dh:(j + 1) * dh, :] + upd[i][j] for j, st in enumerate(u["states"])]
            if backward:
                accb_ref[pl.ds(pl.multiple_of((nc - 1 - c) * chunk + rb, blk), blk), lanes] = o
                sb_ref[2 * pair], sb_ref[2 * pair + 1] = states
            else:
                accf_ref[pl.ds(pl.multiple_of(c * chunk + rf, blk), blk), lanes] = o
                sf_ref[2 * pair], sf_ref[2 * pair + 1] = states
        return carry

    lax.fori_loop(0, nblk, body, 0)

    @pl.when(c == nc - 1)
    def _():
        fin = min(seq, 256)

        def finish(t, carry):
            rows = pl.ds(pl.multiple_of(t * fin, fin), fin)
            for h in range(hp):
                lanes = slice(h * dh, (h + 1) * dh)
                o = accf_ref[rows, lanes] + accb_ref[rows, lanes]
                o = o * lax.rsqrt(jnp.mean(o * o, axis=-1, keepdims=True) + RMS_EPS)
                o = o * nw_ref[:, lanes] * g_ref[0, rows, lanes].astype(F32)
                o_ref[0, rows, lanes] = o.astype(o_ref.dtype)
            return carry

        lax.fori_loop(0, seq // fin, finish, 0)


def _hgrn(qg3, plain3, logf3, key3, norm_w):
    batch, seq, _ = qg3.shape
    hp = HGRN_HEADS_PER_STEP
    w = hp * HGRN_HEAD_DIM
    chunk = min(seq, HGRN_CHUNK)
    nc = seq // chunk

    def fwd(offset):
        base = offset // w
        return pl.BlockSpec((1, chunk, w), lambda b, h, c: (b, c, base + h))

    def bwd(offset):
        base = offset // w
        return pl.BlockSpec((1, chunk, w), lambda b, h, c: (b, nc - 1 - c, base + h))

    gbase = HGRN_WIDTH // w
    return pl.pallas_call(
        _hgrn_kernel,
        grid=(batch, HGRN_HEADS // hp, nc),
        in_specs=[fwd(0), fwd(PLAIN_HI), fwd(0), fwd(0),
                  bwd(0), bwd(PLAIN_HI), bwd(HGRN_WIDTH), bwd(HGRN_WIDTH),
                  pl.BlockSpec((1, seq, w), lambda b, h, c: (b, 0, gbase + h)),
                  pl.BlockSpec((1, w), lambda b, h, c: (0, h))],
        out_specs=pl.BlockSpec((1, seq, w), lambda b, h, c: (b, 0, h)),
        out_shape=jax.ShapeDtypeStruct((batch, seq, HGRN_WIDTH), BF16),
        scratch_shapes=[pltpu.VMEM((seq, w), F32), pltpu.VMEM((seq, w), F32),
                        pltpu.VMEM((hp, HGRN_HEAD_DIM, HGRN_HEAD_DIM), F32),
                        pltpu.VMEM((hp, HGRN_HEAD_DIM, HGRN_HEAD_DIM), F32)],
        compiler_params=_params("parallel", "parallel", "arbitrary"),
        name="hgrn",
    )(qg3, plain3, logf3, key3, qg3, plain3, logf3, key3, qg3, norm_w)


def _pool_kernel(prev_ref, cur_ref, next_ref, pw_ref, ps_ref, o_ref, *, seq_len):
    i = pl.program_id(1)
    last = pl.num_programs(1) - 1
    ts = cur_ref.shape[1]
    n = ts + 2 * POOL_HALO
    pos = i * ts + lax.broadcasted_iota(jnp.int32, (ts, 1), 0)
    for g in range(POOL_GROUPS):
        lanes = slice(g * POOL_GROUP_DIM, (g + 1) * POOL_GROUP_DIM)
        half = POOL_WINDOWS[g] // 2
        cur = cur_ref[0, :, lanes].astype(F32)
        prev = jnp.where(i > 0, prev_ref[0, :, lanes].astype(F32), 0.0)
        nxt = jnp.where(i < last, next_ref[0, :, lanes].astype(F32), 0.0)
        ext = jnp.concatenate([prev, cur, nxt], axis=0)
        w = ext + pltpu.roll(ext, 1, 0)
        step = 1
        while step < half:
            w = pltpu.roll(w, step, 0) + pltpu.roll(w, n - step, 0)
            step *= 2
        wsum = w[POOL_HALO:POOL_HALO + ts]
        count = (jnp.minimum(pos + half, seq_len) - jnp.maximum(pos - half, 0)).astype(F32)
        mixed = wsum / count - cur
        y = jnp.dot(mixed.astype(BF16), pw_ref[g], preferred_element_type=F32)
        o_ref[0, :, lanes] = (y * ps_ref[:, lanes]).astype(o_ref.dtype)


def _pool(plain3, pool_w, pool_scale):
    batch, seq, _ = plain3.shape
    ts = min(seq, 512)
    hb = ts // POOL_HALO
    nhalo = seq // POOL_HALO
    kern = functools.partial(_pool_kernel, seq_len=seq)
    return pl.pallas_call(
        kern,
        grid=(batch, seq // ts),
        in_specs=[pl.BlockSpec((1, POOL_HALO, D_MODEL), lambda b, i: (b, jnp.maximum(i * hb - 1, 0), 0)),
                  pl.BlockSpec((1, ts, D_MODEL), lambda b, i: (b, i, 0)),
                  pl.BlockSpec((1, POOL_HALO, D_MODEL), lambda b, i: (b, jnp.minimum((i + 1) * hb, nhalo - 1), 0)),
                  pl.BlockSpec((POOL_GROUPS, POOL_GROUP_DIM, POOL_GROUP_DIM), lambda b, i: (0, 0, 0)),
                  pl.BlockSpec((1, D_MODEL), lambda b, i: (0, 0))],
        out_specs=pl.BlockSpec((1, ts, D_MODEL), lambda b, i: (b, i, 0)),
        out_shape=jax.ShapeDtypeStruct((batch, seq, D_MODEL), BF16),
        compiler_params=_params("parallel", "parallel"),
        name="pool",
    )(plain3, plain3, plain3, pool_w, pool_scale)


def _merge_kernel(ya_ref, o1_ref, o2_ref, o3_ref, l1_ref, l2_ref, l3_ref, hc_ref, xin_ref, x_ref,
                  wa_ref, wb_ref, wc_ref, wo_ref, wgate_ref, g_ref, b_ref, xo_ref, xb_ref, *, alpha):
    d = D_MODEL
    xin = xin_ref[...]
    y_a = jnp.dot(ya_ref[...], wa_ref[...], preferred_element_type=F32)
    y_c = jnp.dot(hc_ref[...], wc_ref[...], preferred_element_type=F32)
    gates = [_sigmoid(jnp.dot(xin, wgate_ref[:, n * d:(n + 1) * d], preferred_element_type=F32))
             for n in range(3)]
    l1, l2, l3 = l1_ref[...], l2_ref[...], l3_ref[...]
    m = jnp.maximum(jnp.maximum(l1, l2), l3)
    e1, e2, e3 = jnp.exp(l1 - m), jnp.exp(l2 - m), jnp.exp(l3 - m)
    att = (e1 * o1_ref[...].astype(F32) + e2 * o2_ref[...].astype(F32)
           + e3 * o3_ref[...].astype(F32)) / (e1 + e2 + e3)
    y_b = jnp.dot(att.astype(BF16), wb_ref[...], preferred_element_type=F32)
    merged = gates[0] * y_a + gates[1] * y_b + gates[2] * y_c
    mix = jnp.dot(merged.astype(BF16), wo_ref[...], preferred_element_type=F32)
    x1 = _layer_norm(alpha * x_ref[...] + mix, g_ref[...], b_ref[...])
    xo_ref[...] = x1
    xb_ref[...] = x1.astype(BF16)


def _merge(y_pool, o_list, l_list, y_hgrn, x_bf, x, wa, wb, wc, wo, wgate, ln_g, ln_b, alpha):
    T = x.shape[0]
    tm = min(T, 256)
    gw = ATTN_GROUP_WIDTH
    d = D_MODEL
    row = lambda w: pl.BlockSpec((tm, w), lambda i: (i, 0))
    const = lambda a: pl.BlockSpec(a.shape, lambda i: (0,) * a.ndim, pipeline_mode=pl.Buffered(1))
    kern = functools.partial(_merge_kernel, alpha=alpha)
    return pl.pallas_call(
        kern,
        grid=(T // tm,),
        in_specs=[row(d), row(gw), row(gw), row(gw), row(gw), row(gw), row(gw), row(d), row(d),
                  row(d), const(wa), const(wb), const(wc), const(wo), const(wgate), const(ln_g), const(ln_b)],
        out_specs=[row(d), row(d)],
        out_shape=[jax.ShapeDtypeStruct((T, d), F32), jax.ShapeDtypeStruct((T, d), BF16)],
        compiler_params=_params("parallel"),
        name="merge",
    )(y_pool, *o_list, *l_list, y_hgrn, x_bf, x, wa, wb, wc, wo, wgate, ln_g, ln_b)


def _ffn_kernel(xb_ref, x_ref, p_ref, wg_ref, wu_ref, wd_ref, wpp_ref, wpg_ref, g_ref, b_ref,
                xo_ref, xbo_ref, *, alpha, nsub):
    sub = xb_ref.shape[0] // nsub
    rows = [slice(s * sub, (s + 1) * sub) for s in range(nsub)]
    xb = [xb_ref[r, :] for r in rows]
    gate = [jnp.dot(x, wg_ref[...], preferred_element_type=F32) for x in xb]
    up = [jnp.dot(x, wu_ref[...], preferred_element_type=F32) for x in xb]
    pgate = [jnp.dot(x, wpg_ref[...], preferred_element_type=F32) for x in xb]
    ple = [jnp.dot(p_ref[r, :].astype(BF16), wpp_ref[...], preferred_element_type=F32) for r in rows]
    hidden = [(g * _sigmoid(g) * u).astype(BF16) for g, u in zip(gate, up)]
    ffn = [jnp.dot(h, wd_ref[...], preferred_element_type=F32) for h in hidden]
    for s, r in enumerate(rows):
        x2 = _layer_norm(alpha * x_ref[r, :] + ffn[s] + ple[s] * _sigmoid(pgate[s]), g_ref[...], b_ref[...])
        xo_ref[r, :] = x2
        xbo_ref[r, :] = x2.astype(BF16)


def _ffn(x1_bf, x1, p, wg, wu, wd, wpp, wpg, ln_g, ln_b, alpha):
    T = x1.shape[0]
    tm = min(T, FFN_ROWS)
    d = D_MODEL
    row = lambda w: pl.BlockSpec((tm, w), lambda i: (i, 0))
    const = lambda a: pl.BlockSpec(a.shape, lambda i: (0,) * a.ndim, pipeline_mode=pl.Buffered(1))
    kern = functools.partial(_ffn_kernel, alpha=alpha, nsub=FFN_SUBTILES)
    return pl.pallas_call(
        kern,
        grid=(T // tm,),
        in_specs=[row(d), row(d), row(PLE_DIM), const(wg), const(wu), const(wd), const(wpp),
                  const(wpg), const(ln_g), const(ln_b)],
        out_specs=[row(d), row(d)],
        out_shape=[jax.ShapeDtypeStruct((T, d), F32), jax.ShapeDtypeStruct((T, d), BF16)],
        compiler_params=_params("parallel"),
        name="ffn",
    )(x1_bf, x1, p, wg, wu, wd, wpp, wpg, ln_g, ln_b)


def _split_in_weights(w_in):
    d, aw, gw, hw = D_MODEL, ATTN_HEADS * ATTN_HEAD_DIM, ATTN_GROUP_WIDTH, HGRN_WIDTH
    pool = w_in[:, :, 0:d]
    aq = w_in[:, :, d:d + aw]
    ak = w_in[:, :, d + aw:d + 2 * aw]
    av = w_in[:, :, d + 2 * aw:d + 3 * aw]
    h0 = d + 3 * aw
    hq, hi, hff, hfb, hg = [w_in[:, :, h0 + n * hw:h0 + (n + 1) * hw] for n in range(5)]
    gates = w_in[:, :, h0 + 5 * hw:]
    qkv = lambda g: [t[:, :, g * gw:(g + 1) * gw] for t in (aq, ak, av)]
    cat = lambda parts: jnp.concatenate(parts, axis=-1).astype(BF16)
    return (cat([pool, hi] + qkv(0)), cat([hq, hg]), cat([hff, hfb]), gates.astype(BF16),
            cat(qkv(1)), cat(qkv(2)))


def kernel(x, p, w_in, pool_w, pool_scale, w_branch_a, w_branch_b, w_branch_c, hgrn_lb_logits, hgrn_norm_w, w_out, ln1_g, ln1_b, w_ffn_gate, w_ffn_up, w_ffn_down, w_ple_proj, w_ple_gate, ln2_g, ln2_b):
    batch, seq, d = x.shape
    depth = w_in.shape[0]
    T = batch * seq
    alpha = float((2 * depth) ** 0.25)

    lb = jnp.cumsum(jax.nn.softmax(hgrn_lb_logits.astype(F32), axis=0), axis=0)
    lb = lb - lb[:1]

    w_plain, w_silu, w_forget, w_gates, w_g2, w_g3 = _split_in_weights(w_in)
    bf = lambda a: a.astype(BF16)
    pool_w_b, wa, wb, wc, wo = bf(pool_w), bf(w_branch_a), bf(w_branch_b), bf(w_branch_c), bf(w_out)
    wg, wu, wd, wpp, wpg = bf(w_ffn_gate), bf(w_ffn_up), bf(w_ffn_down), bf(w_ple_proj), bf(w_ple_gate)

    slopes = [2.0 ** (-ALIBI_MAX_BIAS * (i + 1) / ATTN_HEADS) for i in range(ATTN_HEADS)]
    hpg = ATTN_HEADS_PER_GROUP
    gw = ATTN_GROUP_WIDTH

    xf = x.reshape(T, d).astype(F32)
    xb = xf.astype(BF16)
    for i in range(depth):
        plain = _inproj(xb, w_plain[i], "plain", PLAIN_WIDTH // 2)
        qg = _inproj(xb, w_silu[i], "silu", HGRN_WIDTH)
        logf, key = _inproj(xb, w_forget[i], "forget", HGRN_WIDTH, lb=lb[i:i + 1])
        plain3 = plain.reshape(batch, seq, PLAIN_WIDTH)
        o_list, l_list = [], []
        for g, (_, dil) in enumerate(ATTN_GROUPS):
            gs = tuple(slopes[g * hpg:(g + 1) * hpg])
            if dil == 1:
                qkv = plain.reshape(batch, 1, seq, PLAIN_WIDTH)
                cols = (PLAIN_AQ // gw, PLAIN_AK // gw, PLAIN_AV // gw)
            else:
                qkv = _inproj_dilated(xf, (w_g2, w_g3)[g - 1][i], batch, seq, dil)
                cols = (0, 1, 2)
            o, l = _attention_group(qkv, qkv, qkv, cols, batch, seq, dil, gs)
            o_list.append(o)
            l_list.append(l)
        y_hgrn = _hgrn(qg.reshape(batch, seq, 2 * HGRN_WIDTH), plain3,
                       logf.reshape(batch, seq, 2 * HGRN_WIDTH), key.reshape(batch, seq, 2 * HGRN_WIDTH),
                       hgrn_norm_w[i:i + 1])
        y_pool = _pool(plain3, pool_w_b[i], pool_scale[i:i + 1])
        x1, x1b = _merge(y_pool.reshape(T, d), o_list, l_list, y_hgrn.reshape(T, d), xb, xf,
                         wa[i], wb[i], wc[i], wo[i], w_gates[i], ln1_g[i:i + 1], ln1_b[i:i + 1], alpha)
        xf, xb = _ffn(x1b, x1, p[i].reshape(T, PLE_DIM), wg[i], wu[i], wd[i], wpp[i], wpg[i],
                      ln2_g[i:i + 1], ln2_b[i:i + 1], alpha)
    return xf.reshape(batch, seq, d).astype(x.dtype)
```

```python
import functools

import jax
import jax.numpy as jnp
from jax import lax
from jax.experimental import pallas as pl
from jax.experimental.pallas import tpu as pltpu

F32 = jnp.float32
BF16 = jnp.bfloat16
LANES = 128
SUBLANES = 8
LOG2_E = 1.4426950408889634
LN_2 = 0.6931471805599453

D_MODEL = 1024
PLE_DIM = 256
POOL_WINDOWS = (2, 4, 8, 16)
POOL_GROUPS = 4
POOL_GROUP_DIM = D_MODEL // POOL_GROUPS
ATTN_GROUPS = ((128, 1), (512, 4), (2048, 16))
ATTN_HEADS_PER_GROUP = 4
ATTN_HEADS = ATTN_HEADS_PER_GROUP * len(ATTN_GROUPS)
ATTN_HEAD_DIM = 128
ATTN_GROUP_WIDTH = ATTN_HEADS_PER_GROUP * ATTN_HEAD_DIM
ATTN_SIDE = 64
ALIBI_MAX_BIAS = 8.0
NEG_INF = -1e30
HGRN_HEADS = 8
HGRN_HEAD_DIM = 128
HGRN_WIDTH = HGRN_HEADS * HGRN_HEAD_DIM
FFN_HIDDEN = 2816
LN_EPS = 1e-5
RMS_EPS = 1e-6

PLAIN_POOL = 0
PLAIN_HI = 1024
PLAIN_AQ = 2048
PLAIN_AK = 2560
PLAIN_AV = 3072
PLAIN_WIDTH = 3584
QKV_WIDTH = 3 * ATTN_GROUP_WIDTH

INPROJ_COLUMN_STEPS = 2
INPROJ_ROWS = 512
ATTN_QUERY_BLOCK = 128
ATTN_KEY_WINDOW = ATTN_QUERY_BLOCK + 2 * ATTN_SIDE
ATTN_TOKENS_PER_STEP = 2048
ATTN_BLOCKS_PER_ITER = 2
ATTN_WINDOW_OFFSETS = 3
HGRN_BLOCK = 64
HGRN_HEADS_PER_STEP = 4
HGRN_CHUNK = 512
MERGE_ROWS = 512
MERGE_SUBTILES = 2
FFN_ROWS = 512
FFN_SUBTILES = 2
POOL_HALO = 16
VMEM_LIMIT = 52 * 1024 * 1024


def _params(*semantics):
    return pltpu.CompilerParams(dimension_semantics=semantics, vmem_limit_bytes=VMEM_LIMIT)


def _sigmoid(x):
    return 0.5 * jnp.tanh(0.5 * x) + 0.5


def _layer_norm(h, g, b):
    mu = jnp.mean(h, axis=-1, keepdims=True)
    d = h - mu
    var = jnp.mean(d * d, axis=-1, keepdims=True)
    return d * lax.rsqrt(var + LN_EPS) * g + b


def _inproj_kernel(x_ref, ws_ref, wf_ref, wp_ref, lb_ref, act_ref, logf_ref, key_ref, plain_ref):
    x = x_ref[...]
    zs = jnp.dot(x, ws_ref[...], preferred_element_type=F32)
    zf = jnp.dot(x, wf_ref[...], preferred_element_type=F32)
    plain_ref[...] = jnp.dot(x, wp_ref[...], preferred_element_type=F32).astype(plain_ref.dtype)
    act_ref[...] = (zs * _sigmoid(zs)).astype(act_ref.dtype)
    lb = lb_ref[...]
    sig = _sigmoid(zf)
    logf_ref[...] = jnp.log(lb + (1.0 - lb) * sig)
    key_ref[...] = ((1.0 - lb) * (1.0 - sig)).astype(key_ref.dtype)


def _inproj(x_bf, w_silu, w_forget, w_plain, lb):
    T, D = x_bf.shape
    tm = min(T, INPROJ_ROWS)
    steps = INPROJ_COLUMN_STEPS
    ns, nf, np_ = w_silu.shape[1] // steps, w_forget.shape[1] // steps, w_plain.shape[1] // steps
    wspec = lambda n: pl.BlockSpec((D, n), lambda j, i: (0, j))
    ospec = lambda n: pl.BlockSpec((tm, n), lambda j, i: (i, j))
    oshape = lambda w, dt: jax.ShapeDtypeStruct((T, w.shape[1]), dt)
    return pl.pallas_call(
        _inproj_kernel,
        grid=(steps, T // tm),
        in_specs=[pl.BlockSpec((tm, D), lambda j, i: (i, 0)), wspec(ns), wspec(nf), wspec(np_),
                  pl.BlockSpec((1, nf), lambda j, i: (0, j))],
        out_specs=[ospec(ns), ospec(nf), ospec(nf), ospec(np_)],
        out_shape=[oshape(w_silu, BF16), oshape(w_forget, F32), oshape(w_forget, BF16), oshape(w_plain, BF16)],
        compiler_params=_params("parallel", "parallel"),
        name="inproj",
    )(x_bf, w_silu, w_forget, w_plain, lb)


def _inproj_dilated_kernel(x_ref, w_ref, o_ref, slab_ref, xp_ref, *, dilation):
    n = x_ref.shape[0] // dilation
    for c in range(D_MODEL // LANES):
        slab_ref[c] = x_ref[:, c * LANES:(c + 1) * LANES]
    for r in range(dilation):
        for c in range(D_MODEL // LANES):
            xp_ref[r * n:(r + 1) * n, c * LANES:(c + 1) * LANES] = (
                slab_ref[c, pl.ds(r, n, stride=dilation), :].astype(BF16))
    res = jnp.dot(xp_ref[...], w_ref[...], preferred_element_type=F32).astype(o_ref.dtype)
    for r in range(dilation):
        o_ref[0, r] = res[r * n:(r + 1) * n]


def _inproj_dilated(x_f32, w, batch, seq, dilation):
    N = w.shape[1]
    tb = min(seq, 1024)
    steps = seq // tb
    n = tb // dilation
    return pl.pallas_call(
        functools.partial(_inproj_dilated_kernel, dilation=dilation),
        grid=(batch * steps,),
        in_specs=[pl.BlockSpec((tb, D_MODEL), lambda i: (i, 0)),
                  pl.BlockSpec((D_MODEL, N), lambda i: (0, 0))],
        out_specs=pl.BlockSpec((1, dilation, n, N), lambda i: (i // steps, 0, i % steps, 0)),
        out_shape=jax.ShapeDtypeStruct((batch, dilation, seq // dilation, N), BF16),
        scratch_shapes=[pltpu.VMEM((D_MODEL // LANES, tb, LANES), F32), pltpu.VMEM((tb, D_MODEL), BF16)],
        compiler_params=_params("parallel"),
        name="inproj_dilated%d" % dilation,
    )(x_f32, w)


def _attn_kernel(q_ref, k_ref, v_ref, o_ref, l_ref, oacc_ref, lacc_ref, bias_ref, *, seq_len, dilation, slopes):
    c = pl.program_id(1)
    lq = q_ref.shape[2]
    qb, kw = ATTN_QUERY_BLOCK, ATTN_KEY_WINDOW
    nq = lq // qb
    per_iter = ATTN_BLOCKS_PER_ITER
    scale2 = ATTN_HEAD_DIM ** -0.5 * LOG2_E
    heads = range(ATTN_HEADS_PER_GROUP)
    lanes = [slice(h * ATTN_HEAD_DIM, (h + 1) * ATTN_HEAD_DIM) for h in heads]

    @pl.when((pl.program_id(0) == 0) & (c == 0))
    def _():
        row = lax.broadcasted_iota(jnp.int32, (qb, kw), 0)
        col = lax.broadcasted_iota(jnp.int32, (qb, kw), 1)
        for w in range(ATTN_WINDOW_OFFSETS):
            dist = jnp.abs(col - row - w * ATTN_SIDE)
            distf = dist.astype(F32) * float(dilation)
            for h in heads:
                bias_ref[h, w] = jnp.where(dist <= ATTN_SIDE, distf * (-slopes[h] * LOG2_E), NEG_INF * LOG2_E)

    def body(it, carry):
        blocks = []
        for u in range(per_iter):
            idx = it * per_iter + u
            r = idx // nq
            r0 = pl.multiple_of((idx % nq) * qb, qb)
            t0 = c * lq + r0
            ks = pl.multiple_of(jnp.clip(t0 - ATTN_SIDE, 0, seq_len - kw), ATTN_SIDE)
            if dilation == 1:
                out_rows = pl.ds(r0, qb)
            else:
                out_rows = pl.ds(r0 * dilation + r, qb, stride=dilation)
            blocks.append(((t0 - ks) // ATTN_SIDE, out_rows, q_ref[0, r, pl.ds(r0, qb), :],
                           k_ref[0, r, pl.ds(ks, kw), :], v_ref[0, r, pl.ds(ks, kw), :]))
        scores = [[lax.dot_general(q[:, lanes[h]], k[:, lanes[h]], (((1,), (1,)), ((), ())),
                                   preferred_element_type=F32) for h in heads]
                  for _, _, q, k, _ in blocks]
        probs = []
        for (w, _, _, _, _), block_scores in zip(blocks, scores):
            block_probs = []
            for h in heads:
                s = block_scores[h] * scale2 + bias_ref[h, w]
                m = jnp.max(s, axis=1, keepdims=True)
                p = jnp.exp2(s - m)
                block_probs.append((p.astype(BF16), jnp.sum(p, axis=1, keepdims=True), m))
            probs.append(block_probs)
        pv = [[jnp.dot(block_probs[h][0], v[:, lanes[h]], preferred_element_type=F32) for h in heads]
              for (_, _, _, _, v), block_probs in zip(blocks, probs)]
        for (_, out_rows, _, _, _), block_probs, block_pv in zip(blocks, probs, pv):
            for h in heads:
                _, l, m = block_probs[h]
                oacc_ref[h, out_rows, :] = block_pv[h] / l
                lse = (m + jnp.log2(l)) * LN_2
                lacc_ref[h, out_rows, :] = jnp.broadcast_to(lse, (qb, ATTN_HEAD_DIM))
        return carry

    lax.fori_loop(0, dilation * nq // per_iter, body, 0)
    for h in heads:
        o_ref[0, :, lanes[h]] = oacc_ref[h].astype(o_ref.dtype)
        l_ref[0, :, lanes[h]] = lacc_ref[h]


def _attention_group(q_arr, k_arr, v_arr, cols, batch, seq, dilation, slopes):
    L = seq // dilation
    gw = ATTN_GROUP_WIDTH
    tc = min(seq, ATTN_TOKENS_PER_STEP)
    lq = tc // dilation
    kern = functools.partial(_attn_kernel, seq_len=L, dilation=dilation, slopes=slopes)
    qc, kc, vc = cols
    o, lse = pl.pallas_call(
        kern,
        grid=(batch, seq // tc),
        in_specs=[pl.BlockSpec((1, dilation, lq, gw), lambda b, c: (b, 0, c, qc)),
                  pl.BlockSpec((1, dilation, L, gw), lambda b, c: (b, 0, 0, kc)),
                  pl.BlockSpec((1, dilation, L, gw), lambda b, c: (b, 0, 0, vc))],
        out_specs=[pl.BlockSpec((1, tc, gw), lambda b, c: (b, c, 0)),
                   pl.BlockSpec((1, tc, gw), lambda b, c: (b, c, 0))],
        out_shape=[jax.ShapeDtypeStruct((batch, seq, gw), BF16),
                   jax.ShapeDtypeStruct((batch, seq, gw), F32)],
        scratch_shapes=[pltpu.VMEM((ATTN_HEADS_PER_GROUP, tc, ATTN_HEAD_DIM), F32),
                        pltpu.VMEM((ATTN_HEADS_PER_GROUP, tc, ATTN_HEAD_DIM), F32),
                        pltpu.VMEM((ATTN_HEADS_PER_GROUP, ATTN_WINDOW_OFFSETS, ATTN_QUERY_BLOCK, ATTN_KEY_WINDOW), F32)],
        compiler_params=_params("arbitrary", "arbitrary"),
        name="attn_dil%d" % dilation,
    )(q_arr, k_arr, v_arr)
    return o.reshape(batch * seq, gw), lse.reshape(batch * seq, gw)


def _running_sum_rows(x, row_index, reverse):
    n = x.shape[0]
    shift = 1
    while shift < n:
        if reverse:
            x = x + jnp.where(row_index < n - shift, pltpu.roll(x, n - shift, 0), 0.0)
        else:
            x = x + jnp.where(row_index >= shift, pltpu.roll(x, shift, 0), 0.0)
        shift *= 2
    return x


def _hgrn_kernel(qf_ref, vf_ref, lf_ref, kf_ref, qb_ref, vb_ref, lb_ref, kb_ref, g_ref, nw_ref, o_ref,
                 accf_ref, accb_ref, sf_ref, sb_ref):
    c = pl.program_id(2)
    nc = pl.num_programs(2)
    chunk = qf_ref.shape[1]
    seq = o_ref.shape[1]
    hp = HGRN_HEADS_PER_STEP
    blk = HGRN_BLOCK
    half = blk // 2
    nblk = chunk // blk
    dh = HGRN_HEAD_DIM
    pw = 2 * dh

    @pl.when(c == 0)
    def _():
        sf_ref[...] = jnp.zeros_like(sf_ref)
        sb_ref[...] = jnp.zeros_like(sb_ref)

    row_index = lax.broadcasted_iota(jnp.int32, (blk, pw), 0)
    ti = lax.broadcasted_iota(jnp.int32, (blk, 2 * blk), 0)
    si = lax.broadcasted_iota(jnp.int32, (blk, 2 * blk), 1) % blk
    causal = si <= ti
    anticausal = si >= ti
    zeros_kv = jnp.zeros((blk, dh), BF16)
    zeros_st = jnp.zeros((dh, dh), BF16)

    def block_diag(a, b, z):
        return jnp.concatenate([jnp.concatenate([a, z], axis=1), jnp.concatenate([z, b], axis=1)], axis=0)

    def load_unit(pair, r0, q_ref, v_ref, logf_ref, key_ref, s_ref):
        lanes = slice(pair * pw, (pair + 1) * pw)
        rows = pl.ds(r0, blk)
        return (q_ref[0, rows, lanes], v_ref[0, rows, lanes], logf_ref[0, rows, lanes], key_ref[0, rows, lanes],
                s_ref[2 * pair], s_ref[2 * pair + 1])

    def prepare_unit(loaded, backward):
        q, v, logf, k, st0, st1 = loaded
        a = _running_sum_rows(logf, row_index, backward)
        if backward:
            total = a[0:1, :]
            mid = a[half:half + 1, :]
        else:
            total = a[blk - 1:blk, :]
            mid = a[half - 1:half, :]
        q_mid = q * jnp.exp(a - mid).astype(BF16)
        k_mid_b = k * jnp.exp(mid - a).astype(BF16)
        q_in = q_mid * jnp.exp(mid).astype(BF16)
        k_out = k_mid_b * jnp.exp(total - mid).astype(BF16)
        decay = jnp.broadcast_to(jnp.exp(total), (SUBLANES, pw)).T[:, 0:1]
        return dict(q_mid=q_mid, k_diag=block_diag(k_mid_b[:, :dh], k_mid_b[:, dh:], zeros_kv),
                    q_in=q_in, s_diag=block_diag(st0.astype(BF16), st1.astype(BF16), zeros_st),
                    v=v, v_diag=block_diag(v[:, :dh], v[:, dh:], zeros_kv), k_out=k_out,
                    decay=decay, states=(st0, st1), mask=anticausal if backward else causal)

    nt = (((1,), (1,)), ((), ()))
    tn = (((0,), (0,)), ((), ()))

    def body(n, carry):
        rf = pl.multiple_of(n * blk, blk)
        rb = pl.multiple_of((nblk - 1 - n) * blk, blk)
        units = []
        for pair in range(hp // 2):
            units.append((pair, False, load_unit(pair, rf, qf_ref, vf_ref, lf_ref, kf_ref, sf_ref)))
            units.append((pair, True, load_unit(pair, rb, qb_ref, vb_ref, lb_ref, kb_ref, sb_ref)))
        prepared = [prepare_unit(loaded, backward) for _, backward, loaded in units]
        att = [lax.dot_general(u["q_mid"], u["k_diag"], nt, preferred_element_type=F32) for u in prepared]
        inter = [jnp.dot(u["q_in"], u["s_diag"], preferred_element_type=F32) for u in prepared]
        upd = [[lax.dot_general(u["k_out"][:, j * dh:(j + 1) * dh], u["v"][:, j * dh:(j + 1) * dh], tn,
                                preferred_element_type=F32) for j in range(2)] for u in prepared]
        att = [jnp.where(u["mask"], s, 0.0).astype(BF16) for u, s in zip(prepared, att)]
        intra = [jnp.dot(s, u["v_diag"], preferred_element_type=F32) for u, s in zip(prepared, att)]
        for i, (pair, backward, _) in enumerate(units):
            u = prepared[i]
            lanes = slice(pair * pw, (pair + 1) * pw)
            o = intra[i] + inter[i]
            states = [st * u["decay"][j * dh:(j + 1) * dh, :] + upd[i][j] for j, st in enumerate(u["states"])]
            if backward:
                accb_ref[pl.ds(pl.multiple_of((nc - 1 - c) * chunk + rb, blk), blk), lanes] = o
                sb_ref[2 * pair], sb_ref[2 * pair + 1] = states
            else:
                accf_ref[pl.ds(pl.multiple_of(c * chunk + rf, blk), blk), lanes] = o
                sf_ref[2 * pair], sf_ref[2 * pair + 1] = states
        return carry

    lax.fori_loop(0, nblk, body, 0)

    @pl.when(c == nc - 1)
    def _():
        fin = min(seq, 256)

        def finish(t, carry):
            rows = pl.ds(pl.multiple_of(t * fin, fin), fin)
            for h in range(hp):
                lanes = slice(h * dh, (h + 1) * dh)
                o = accf_ref[rows, lanes] + accb_ref[rows, lanes]
                o = o * lax.rsqrt(jnp.mean(o * o, axis=-1, keepdims=True) + RMS_EPS)
                o = o * nw_ref[:, lanes] * g_ref[0, rows, lanes].astype(F32)
                o_ref[0, rows, lanes] = o.astype(o_ref.dtype)
            return carry

        lax.fori_loop(0, seq // fin, finish, 0)


def _hgrn(qg3, plain3, logf3, key3, norm_w):
    batch, seq, _ = qg3.shape
    hp = HGRN_HEADS_PER_STEP
    w = hp * HGRN_HEAD_DIM
    chunk = min(seq, HGRN_CHUNK)
    nc = seq // chunk

    def fwd(offset):
        base = offset // w
        return pl.BlockSpec((1, chunk, w), lambda b, h, c: (b, c, base + h))

    def bwd(offset):
        base = offset // w
        return pl.BlockSpec((1, chunk, w), lambda b, h, c: (b, nc - 1 - c, base + h))

    gbase = HGRN_WIDTH // w
    return pl.pallas_call(
        _hgrn_kernel,
        grid=(batch, HGRN_HEADS // hp, nc),
        in_specs=[fwd(0), fwd(PLAIN_HI), fwd(0), fwd(0),
                  bwd(0), bwd(PLAIN_HI), bwd(HGRN_WIDTH), bwd(HGRN_WIDTH),
                  pl.BlockSpec((1, seq, w), lambda b, h, c: (b, 0, gbase + h)),
                  pl.BlockSpec((1, w), lambda b, h, c: (0, h))],
        out_specs=pl.BlockSpec((1, seq, w), lambda b, h, c: (b, 0, h)),
        out_shape=jax.ShapeDtypeStruct((batch, seq, HGRN_WIDTH), BF16),
        scratch_shapes=[pltpu.VMEM((seq, w), F32), pltpu.VMEM((seq, w), F32),
                        pltpu.VMEM((hp, HGRN_HEAD_DIM, HGRN_HEAD_DIM), F32),
                        pltpu.VMEM((hp, HGRN_HEAD_DIM, HGRN_HEAD_DIM), F32)],
        compiler_params=_params("parallel", "parallel", "arbitrary"),
        name="hgrn",
    )(qg3, plain3, logf3, key3, qg3, plain3, logf3, key3, qg3, norm_w)


def _pool_kernel(prev_ref, cur_ref, next_ref, pw_ref, ps_ref, o_ref, *, seq_len):
    i = pl.program_id(1)
    last = pl.num_programs(1) - 1
    ts = cur_ref.shape[1]
    n = ts + 2 * POOL_HALO
    pos = i * ts + lax.broadcasted_iota(jnp.int32, (ts, 1), 0)
    for g in range(POOL_GROUPS):
        lanes = slice(g * POOL_GROUP_DIM, (g + 1) * POOL_GROUP_DIM)
        half = POOL_WINDOWS[g] // 2
        cur = cur_ref[0, :, lanes].astype(F32)
        prev = jnp.where(i > 0, prev_ref[0, :, lanes].astype(F32), 0.0)
        nxt = jnp.where(i < last, next_ref[0, :, lanes].astype(F32), 0.0)
        ext = jnp.concatenate([prev, cur, nxt], axis=0)
        w = ext + pltpu.roll(ext, 1, 0)
        step = 1
        while step < half:
            w = pltpu.roll(w, step, 0) + pltpu.roll(w, n - step, 0)
            step *= 2
        wsum = w[POOL_HALO:POOL_HALO + ts]
        count = (jnp.minimum(pos + half, seq_len) - jnp.maximum(pos - half, 0)).astype(F32)
        mixed = wsum / count - cur
        y = jnp.dot(mixed.astype(BF16), pw_ref[g], preferred_element_type=F32)
        o_ref[0, :, lanes] = (y * ps_ref[:, lanes]).astype(o_ref.dtype)


def _pool(plain3, pool_w, pool_scale):
    batch, seq, _ = plain3.shape
    ts = min(seq, 512)
    hb = ts // POOL_HALO
    nhalo = seq // POOL_HALO
    kern = functools.partial(_pool_kernel, seq_len=seq)
    return pl.pallas_call(
        kern,
        grid=(batch, seq // ts),
        in_specs=[pl.BlockSpec((1, POOL_HALO, D_MODEL), lambda b, i: (b, jnp.maximum(i * hb - 1, 0), 0)),
                  pl.BlockSpec((1, ts, D_MODEL), lambda b, i: (b, i, 0)),
                  pl.BlockSpec((1, POOL_HALO, D_MODEL), lambda b, i: (b, jnp.minimum((i + 1) * hb, nhalo - 1), 0)),
                  pl.BlockSpec((POOL_GROUPS, POOL_GROUP_DIM, POOL_GROUP_DIM), lambda b, i: (0, 0, 0)),
                  pl.BlockSpec((1, D_MODEL), lambda b, i: (0, 0))],
        out_specs=pl.BlockSpec((1, ts, D_MODEL), lambda b, i: (b, i, 0)),
        out_shape=jax.ShapeDtypeStruct((batch, seq, D_MODEL), BF16),
        compiler_params=_params("parallel", "parallel"),
        name="pool",
    )(plain3, plain3, plain3, pool_w, pool_scale)


def _merge_kernel(ya_ref, o1_ref, o2_ref, o3_ref, l1_ref, l2_ref, l3_ref, hc_ref, xin_ref, x_ref,
                  wa_ref, wb_ref, wc_ref, wo_ref, wgate_ref, g_ref, b_ref, xo_ref, xb_ref, *, alpha, nsub):
    d = D_MODEL
    sub = x_ref.shape[0] // nsub
    rows = [slice(s * sub, (s + 1) * sub) for s in range(nsub)]
    xin = [xin_ref[r, :] for r in rows]
    y_a = [jnp.dot(ya_ref[r, :], wa_ref[...], preferred_element_type=F32) for r in rows]
    y_c = [jnp.dot(hc_ref[r, :], wc_ref[...], preferred_element_type=F32) for r in rows]
    gates = [[jnp.dot(x, wgate_ref[:, n * d:(n + 1) * d], preferred_element_type=F32) for n in range(3)]
             for x in xin]
    att = []
    for r in rows:
        l1, l2, l3 = l1_ref[r, :], l2_ref[r, :], l3_ref[r, :]
        m = jnp.maximum(jnp.maximum(l1, l2), l3)
        e1, e2, e3 = jnp.exp(l1 - m), jnp.exp(l2 - m), jnp.exp(l3 - m)
        merged_heads = (e1 * o1_ref[r, :].astype(F32) + e2 * o2_ref[r, :].astype(F32)
                        + e3 * o3_ref[r, :].astype(F32)) / (e1 + e2 + e3)
        att.append(merged_heads.astype(BF16))
    y_b = [jnp.dot(a, wb_ref[...], preferred_element_type=F32) for a in att]
    merged = [(_sigmoid(gates[s][0]) * y_a[s] + _sigmoid(gates[s][1]) * y_b[s]
               + _sigmoid(gates[s][2]) * y_c[s]).astype(BF16) for s in range(nsub)]
    mix = [jnp.dot(mg, wo_ref[...], preferred_element_type=F32) for mg in merged]
    for s, r in enumerate(rows):
        x1 = _layer_norm(alpha * x_ref[r, :] + mix[s], g_ref[...], b_ref[...])
        xo_ref[r, :] = x1
        xb_ref[r, :] = x1.astype(BF16)


def _merge(y_pool, o_list, l_list, y_hgrn, x_bf, x, wa, wb, wc, wo, wgate, ln_g, ln_b, alpha):
    T = x.shape[0]
    tm = min(T, MERGE_ROWS)
    gw = ATTN_GROUP_WIDTH
    d = D_MODEL
    row = lambda w: pl.BlockSpec((tm, w), lambda i: (i, 0))
    const = lambda a: pl.BlockSpec(a.shape, lambda i: (0,) * a.ndim, pipeline_mode=pl.Buffered(1))
    kern = functools.partial(_merge_kernel, alpha=alpha, nsub=MERGE_SUBTILES)
    return pl.pallas_call(
        kern,
        grid=(T // tm,),
        in_specs=[row(d), row(gw), row(gw), row(gw), row(gw), row(gw), row(gw), row(d), row(d),
                  row(d), const(wa), const(wb), const(wc), const(wo), const(wgate), const(ln_g), const(ln_b)],
        out_specs=[row(d), row(d)],
        out_shape=[jax.ShapeDtypeStruct((T, d), F32), jax.ShapeDtypeStruct((T, d), BF16)],
        compiler_params=_params("parallel"),
        name="merge",
    )(y_pool, *o_list, *l_list, y_hgrn, x_bf, x, wa, wb, wc, wo, wgate, ln_g, ln_b)


def _ffn_kernel(xb_ref, x_ref, p_ref, wg_ref, wu_ref, wd_ref, wpp_ref, wpg_ref, g_ref, b_ref,
                xo_ref, xbo_ref, *, alpha, nsub):
    sub = xb_ref.shape[0] // nsub
    rows = [slice(s * sub, (s + 1) * sub) for s in range(nsub)]
    xb = [xb_ref[r, :] for r in rows]
    gate = [jnp.dot(x, wg_ref[...], preferred_element_type=F32) for x in xb]
    up = [jnp.dot(x, wu_ref[...], preferred_element_type=F32) for x in xb]
    pgate = [jnp.dot(x, wpg_ref[...], preferred_element_type=F32) for x in xb]
    ple = [jnp.dot(p_ref[r, :].astype(BF16), wpp_ref[...], preferred_element_type=F32) for r in rows]
    hidden = [(g * _sigmoid(g) * u).astype(BF16) for g, u in zip(gate, up)]
    ffn = [jnp.dot(h, wd_ref[...], preferred_element_type=F32) for h in hidden]
    for s, r in enumerate(rows):
        x2 = _layer_norm(alpha * x_ref[r, :] + ffn[s] + ple[s] * _sigmoid(pgate[s]), g_ref[...], b_ref[...])
        xo_ref[r, :] = x2
        xbo_ref[r, :] = x2.astype(BF16)


def _ffn(x1_bf, x1, p, wg, wu, wd, wpp, wpg, ln_g, ln_b, alpha):
    T = x1.shape[0]
    tm = min(T, FFN_ROWS)
    d = D_MODEL
    row = lambda w: pl.BlockSpec((tm, w), lambda i: (i, 0))
    const = lambda a: pl.BlockSpec(a.shape, lambda i: (0,) * a.ndim, pipeline_mode=pl.Buffered(1))
    kern = functools.partial(_ffn_kernel, alpha=alpha, nsub=FFN_SUBTILES)
    return pl.pallas_call(
        kern,
        grid=(T // tm,),
        in_specs=[row(d), row(d), row(PLE_DIM), const(wg), const(wu), const(wd), const(wpp),
                  const(wpg), const(ln_g), const(ln_b)],
        out_specs=[row(d), row(d)],
        out_shape=[jax.ShapeDtypeStruct((T, d), F32), jax.ShapeDtypeStruct((T, d), BF16)],
        compiler_params=_params("parallel"),
        name="ffn",
    )(x1_bf, x1, p, wg, wu, wd, wpp, wpg, ln_g, ln_b)


def _split_in_weights(w_in):
    d, aw, gw, hw = D_MODEL, ATTN_HEADS * ATTN_HEAD_DIM, ATTN_GROUP_WIDTH, HGRN_WIDTH
    pool = w_in[:, :, 0:d]
    aq = w_in[:, :, d:d + aw]
    ak = w_in[:, :, d + aw:d + 2 * aw]
    av = w_in[:, :, d + 2 * aw:d + 3 * aw]
    h0 = d + 3 * aw
    hq, hi, hff, hfb, hg = [w_in[:, :, h0 + n * hw:h0 + (n + 1) * hw] for n in range(5)]
    gates = w_in[:, :, h0 + 5 * hw:]
    qkv = lambda g: [t[:, :, g * gw:(g + 1) * gw] for t in (aq, ak, av)]
    cat = lambda parts: jnp.concatenate(parts, axis=-1).astype(BF16)
    return (cat([pool, hi] + qkv(0)), cat([hq, hg]), cat([hff, hfb]), gates.astype(BF16),
            cat(qkv(1)), cat(qkv(2)))


def kernel(x, p, w_in, pool_w, pool_scale, w_branch_a, w_branch_b, w_branch_c, hgrn_lb_logits, hgrn_norm_w, w_out, ln1_g, ln1_b, w_ffn_gate, w_ffn_up, w_ffn_down, w_ple_proj, w_ple_gate, ln2_g, ln2_b):
    batch, seq, d = x.shape
    depth = w_in.shape[0]
    T = batch * seq
    alpha = float((2 * depth) ** 0.25)

    lb = jnp.cumsum(jax.nn.softmax(hgrn_lb_logits.astype(F32), axis=0), axis=0)
    lb = lb - lb[:1]

    w_plain, w_silu, w_forget, w_gates, w_g2, w_g3 = _split_in_weights(w_in)
    bf = lambda a: a.astype(BF16)
    pool_w_b, wa, wb, wc, wo = bf(pool_w), bf(w_branch_a), bf(w_branch_b), bf(w_branch_c), bf(w_out)
    wg, wu, wd, wpp, wpg = bf(w_ffn_gate), bf(w_ffn_up), bf(w_ffn_down), bf(w_ple_proj), bf(w_ple_gate)

    slopes = [2.0 ** (-ALIBI_MAX_BIAS * (i + 1) / ATTN_HEADS) for i in range(ATTN_HEADS)]
    hpg = ATTN_HEADS_PER_GROUP
    gw = ATTN_GROUP_WIDTH

    xf = x.reshape(T, d).astype(F32)
    xb = xf.astype(BF16)
    for i in range(depth):
        qg, logf, key, plain = _inproj(xb, w_silu[i], w_forget[i], w_plain[i], lb[i:i + 1])
        plain3 = plain.reshape(batch, seq, PLAIN_WIDTH)
        o_list, l_list = [], []
        for g, (_, dil) in enumerate(ATTN_GROUPS):
            gs = tuple(slopes[g * hpg:(g + 1) * hpg])
            if dil == 1:
                qkv = plain.reshape(batch, 1, seq, PLAIN_WIDTH)
                cols = (PLAIN_AQ // gw, PLAIN_AK // gw, PLAIN_AV // gw)
            else:
                qkv = _inproj_dilated(xf, (w_g2, w_g3)[g - 1][i], batch, seq, dil)
                cols = (0, 1, 2)
            o, l = _attention_group(qkv, qkv, qkv, cols, batch, seq, dil, gs)
            o_list.append(o)
            l_list.append(l)
        y_hgrn = _hgrn(qg.reshape(batch, seq, 2 * HGRN_WIDTH), plain3,
                       logf.reshape(batch, seq, 2 * HGRN_WIDTH), key.reshape(batch, seq, 2 * HGRN_WIDTH),
                       hgrn_norm_w[i:i + 1])
        y_pool = _pool(plain3, pool_w_b[i], pool_scale[i:i + 1])
        x1, x1b = _merge(y_pool.reshape(T, d), o_list, l_list, y_hgrn.reshape(T, d), xb, xf,
                         wa[i], wb[i], wc[i], wo[i], w_gates[i], ln1_g[i:i + 1], ln1_b[i:i + 1], alpha)
        xf, xb = _ffn(x1b, x1, p[i].reshape(T, PLE_DIM), wg[i], wu[i], wd[i], wpp[i], wpg[i],
                      ln2_g[i:i + 1], ln2_b[i:i + 1], alpha)
    return xf.reshape(batch, seq, d).astype(x.dtype)
```

```python
import functools

import jax
import jax.numpy as jnp
from jax import lax
from jax.experimental import pallas as pl
from jax.experimental.pallas import tpu as pltpu

F32 = jnp.float32
BF16 = jnp.bfloat16
LANES = 128
SUBLANES = 8
LOG2_E = 1.4426950408889634
LN_2 = 0.6931471805599453

D_MODEL = 1024
PLE_DIM = 256
POOL_WINDOWS = (2, 4, 8, 16)
POOL_GROUPS = 4
POOL_GROUP_DIM = D_MODEL // POOL_GROUPS
ATTN_GROUPS = ((128, 1), (512, 4), (2048, 16))
ATTN_HEADS_PER_GROUP = 4
ATTN_HEADS = ATTN_HEADS_PER_GROUP * len(ATTN_GROUPS)
ATTN_HEAD_DIM = 128
ATTN_GROUP_WIDTH = ATTN_HEADS_PER_GROUP * ATTN_HEAD_DIM
ATTN_SIDE = 64
ALIBI_MAX_BIAS = 8.0
NEG_INF = -1e30
HGRN_HEADS = 8
HGRN_HEAD_DIM = 128
HGRN_WIDTH = HGRN_HEADS * HGRN_HEAD_DIM
FFN_HIDDEN = 2816
LN_EPS = 1e-5
RMS_EPS = 1e-6

PLAIN_POOL = 0
PLAIN_HI = 1024
PLAIN_AQ = 2048
PLAIN_AK = 2560
PLAIN_AV = 3072
PLAIN_WIDTH = 3584
QKV_WIDTH = 3 * ATTN_GROUP_WIDTH

INPROJ_COLUMN_STEPS = 2
INPROJ_ROWS = 512
ATTN_QUERY_BLOCK = 128
ATTN_KEY_WINDOW = ATTN_QUERY_BLOCK + 2 * ATTN_SIDE
ATTN_TOKENS_PER_STEP = 2048
ATTN_BLOCKS_PER_ITER = 2
ATTN_STAT_LANES = 32
ATTN_WINDOW_OFFSETS = 3
HGRN_BLOCK = 64
HGRN_HEADS_PER_STEP = 4
HGRN_CHUNK = 512
MERGE_ROWS = 512
MERGE_SUBTILES = 2
FFN_ROWS = 512
FFN_SUBTILES = 2
POOL_HALO = 16
VMEM_LIMIT = 52 * 1024 * 1024


def _params(*semantics):
    return pltpu.CompilerParams(dimension_semantics=semantics, vmem_limit_bytes=VMEM_LIMIT)


def _sigmoid(x):
    return 0.5 * jnp.tanh(0.5 * x) + 0.5


def _layer_norm(h, g, b):
    mu = jnp.mean(h, axis=-1, keepdims=True)
    d = h - mu
    var = jnp.mean(d * d, axis=-1, keepdims=True)
    return d * lax.rsqrt(var + LN_EPS) * g + b


def _inproj_kernel(x_ref, ws_ref, wf_ref, wp_ref, lb_ref, act_ref, logf_ref, key_ref, plain_ref):
    x = x_ref[...]
    zs = jnp.dot(x, ws_ref[...], preferred_element_type=F32)
    zf = jnp.dot(x, wf_ref[...], preferred_element_type=F32)
    plain_ref[...] = jnp.dot(x, wp_ref[...], preferred_element_type=F32).astype(plain_ref.dtype)
    act_ref[...] = (zs * _sigmoid(zs)).astype(act_ref.dtype)
    lb = lb_ref[...]
    sig = _sigmoid(zf)
    logf_ref[...] = jnp.log(lb + (1.0 - lb) * sig)
    key_ref[...] = ((1.0 - lb) * (1.0 - sig)).astype(key_ref.dtype)


def _inproj(x_bf, w_silu, w_forget, w_plain, lb):
    T, D = x_bf.shape
    tm = min(T, INPROJ_ROWS)
    steps = INPROJ_COLUMN_STEPS
    ns, nf, np_ = w_silu.shape[1] // steps, w_forget.shape[1] // steps, w_plain.shape[1] // steps
    wspec = lambda n: pl.BlockSpec((D, n), lambda j, i: (0, j))
    ospec = lambda n: pl.BlockSpec((tm, n), lambda j, i: (i, j))
    oshape = lambda w, dt: jax.ShapeDtypeStruct((T, w.shape[1]), dt)
    return pl.pallas_call(
        _inproj_kernel,
        grid=(steps, T // tm),
        in_specs=[pl.BlockSpec((tm, D), lambda j, i: (i, 0)), wspec(ns), wspec(nf), wspec(np_),
                  pl.BlockSpec((1, nf), lambda j, i: (0, j))],
        out_specs=[ospec(ns), ospec(nf), ospec(nf), ospec(np_)],
        out_shape=[oshape(w_silu, BF16), oshape(w_forget, F32), oshape(w_forget, BF16), oshape(w_plain, BF16)],
        compiler_params=_params("parallel", "parallel"),
        name="inproj",
    )(x_bf, w_silu, w_forget, w_plain, lb)


def _inproj_dilated_kernel(x_ref, w_ref, o_ref, slab_ref, xp_ref, *, dilation):
    n = x_ref.shape[0] // dilation
    for c in range(D_MODEL // LANES):
        slab_ref[c] = x_ref[:, c * LANES:(c + 1) * LANES]
    for r in range(dilation):
        for c in range(D_MODEL // LANES):
            xp_ref[r * n:(r + 1) * n, c * LANES:(c + 1) * LANES] = (
                slab_ref[c, pl.ds(r, n, stride=dilation), :].astype(BF16))
    res = jnp.dot(xp_ref[...], w_ref[...], preferred_element_type=F32).astype(o_ref.dtype)
    for r in range(dilation):
        o_ref[0, r] = res[r * n:(r + 1) * n]


def _inproj_dilated(x_f32, w, batch, seq, dilation):
    N = w.shape[1]
    tb = min(seq, 1024)
    steps = seq // tb
    n = tb // dilation
    return pl.pallas_call(
        functools.partial(_inproj_dilated_kernel, dilation=dilation),
        grid=(batch * steps,),
        in_specs=[pl.BlockSpec((tb, D_MODEL), lambda i: (i, 0)),
                  pl.BlockSpec((D_MODEL, N), lambda i: (0, 0))],
        out_specs=pl.BlockSpec((1, dilation, n, N), lambda i: (i // steps, 0, i % steps, 0)),
        out_shape=jax.ShapeDtypeStruct((batch, dilation, seq // dilation, N), BF16),
        scratch_shapes=[pltpu.VMEM((D_MODEL // LANES, tb, LANES), F32), pltpu.VMEM((tb, D_MODEL), BF16)],
        compiler_params=_params("parallel"),
        name="inproj_dilated%d" % dilation,
    )(x_f32, w)


def _attn_kernel(q_ref, k_ref, v_ref, o_ref, m_ref, l_ref, oacc_ref, macc_ref, lacc_ref, bias_ref, sa_ref, sb_ref, *,
                 seq_len, dilation, slopes):
    c = pl.program_id(1)
    lq = q_ref.shape[2]
    qb, kw = ATTN_QUERY_BLOCK, ATTN_KEY_WINDOW
    nq = lq // qb
    per_iter = ATTN_BLOCKS_PER_ITER
    scale2 = ATTN_HEAD_DIM ** -0.5 * LOG2_E
    heads = range(ATTN_HEADS_PER_GROUP)
    lanes = [slice(h * ATTN_HEAD_DIM, (h + 1) * ATTN_HEAD_DIM) for h in heads]

    @pl.when((pl.program_id(0) == 0) & (c == 0))
    def _():
        row = lax.broadcasted_iota(jnp.int32, (qb, kw), 0)
        col = lax.broadcasted_iota(jnp.int32, (qb, kw), 1)
        for w in range(ATTN_WINDOW_OFFSETS):
            dist = jnp.abs(col - row - w * ATTN_SIDE)
            distf = dist.astype(F32) * float(dilation)
            for h in heads:
                bias_ref[h, w] = jnp.where(dist <= ATTN_SIDE, distf * (-slopes[h] * LOG2_E), NEG_INF * LOG2_E)

    stat_group = lax.broadcasted_iota(jnp.int32, (kw, ATTN_HEAD_DIM), 1) // ATTN_STAT_LANES
    ones_cols = [jnp.where(stat_group == h, 1.0, 0.0).astype(BF16) for h in heads]
    out_group = lax.broadcasted_iota(jnp.int32, (qb, ATTN_HEAD_DIM), 1) // ATTN_STAT_LANES

    def block_coords(idx):
        r = idx // nq
        r0 = pl.multiple_of((idx % nq) * qb, qb)
        t0 = c * lq + r0
        ks = pl.multiple_of(jnp.clip(t0 - ATTN_SIDE, 0, seq_len - kw), ATTN_SIDE)
        return r, r0, t0, ks

    def score_products(it, s_ref):
        for u in range(per_iter):
            r, r0, _, ks = block_coords(it * per_iter + u)
            q = q_ref[0, r, pl.ds(r0, qb), :]
            k = k_ref[0, r, pl.ds(ks, kw), :]
            for h in heads:
                s_ref[u * len(heads) + h] = lax.dot_general(
                    q[:, lanes[h]], k[:, lanes[h]], (((1,), (1,)), ((), ())), preferred_element_type=F32)

    n_iter = dilation * nq // per_iter
    score_products(0, sa_ref)

    def half_body(it, s_ref, next_ref):
        score_products(jnp.minimum(it + 1, n_iter - 1), next_ref)
        blocks = []
        for u in range(per_iter):
            r, r0, t0, ks = block_coords(it * per_iter + u)
            if dilation == 1:
                out_rows = pl.ds(r0, qb)
            else:
                out_rows = pl.ds(r0 * dilation + r, qb, stride=dilation)
            blocks.append(((t0 - ks) // ATTN_SIDE, out_rows, v_ref[0, r, pl.ds(ks, kw), :]))
        probs = []
        for u, (w, _, _) in enumerate(blocks):
            block_probs = []
            for h in heads:
                s = s_ref[u * len(heads) + h] * scale2 + bias_ref[h, w]
                m = jnp.max(s, axis=1, keepdims=True)
                block_probs.append((jnp.exp2(s - m).astype(BF16), m))
            probs.append(block_probs)
        pv = [[jnp.dot(block_probs[h][0], jnp.concatenate([v[:, lanes[h]], ones_cols[h]], axis=1),
                       preferred_element_type=F32) for h in heads]
              for (_, _, v), block_probs in zip(blocks, probs)]
        for (_, out_rows, _), block_probs, block_pv in zip(blocks, probs, pv):
            m_all = jnp.broadcast_to(block_probs[-1][1], (qb, ATTN_HEAD_DIM))
            l_all = block_pv[-1][:, ATTN_HEAD_DIM:]
            for h in reversed(heads[:-1]):
                m_all = jnp.where(out_group == h, block_probs[h][1], m_all)
                l_all = l_all + block_pv[h][:, ATTN_HEAD_DIM:]
            for h in heads:
                oacc_ref[h, out_rows, :] = block_pv[h][:, :ATTN_HEAD_DIM]
            macc_ref[out_rows, :] = m_all
            lacc_ref[out_rows, :] = l_all

    def body(it2, carry):
        half_body(2 * it2, sa_ref, sb_ref)
        half_body(2 * it2 + 1, sb_ref, sa_ref)
        return carry

    lax.fori_loop(0, n_iter // 2, body, 0)
    for h in heads:
        o_ref[0, :, lanes[h]] = oacc_ref[h].astype(o_ref.dtype)
    m_ref[0] = macc_ref[...]
    l_ref[0] = lacc_ref[...]


def _attention_group(q_arr, k_arr, v_arr, cols, batch, seq, dilation, slopes):
    L = seq // dilation
    gw = ATTN_GROUP_WIDTH
    tc = min(seq, ATTN_TOKENS_PER_STEP)
    lq = tc // dilation
    kern = functools.partial(_attn_kernel, seq_len=L, dilation=dilation, slopes=slopes)
    qc, kc, vc = cols
    stat = pl.BlockSpec((1, tc, ATTN_HEAD_DIM), lambda b, c: (b, c, 0))
    stat_shape = jax.ShapeDtypeStruct((batch, seq, ATTN_HEAD_DIM), F32)
    o, m, l = pl.pallas_call(
        kern,
        grid=(batch, seq // tc),
        in_specs=[pl.BlockSpec((1, dilation, lq, gw), lambda b, c: (b, 0, c, qc)),
                  pl.BlockSpec((1, dilation, L, gw), lambda b, c: (b, 0, 0, kc)),
                  pl.BlockSpec((1, dilation, L, gw), lambda b, c: (b, 0, 0, vc))],
        out_specs=[pl.BlockSpec((1, tc, gw), lambda b, c: (b, c, 0)), stat, stat],
        out_shape=[jax.ShapeDtypeStruct((batch, seq, gw), BF16), stat_shape, stat_shape],
        scratch_shapes=[pltpu.VMEM((ATTN_HEADS_PER_GROUP, tc, ATTN_HEAD_DIM), F32),
                        pltpu.VMEM((tc, ATTN_HEAD_DIM), F32), pltpu.VMEM((tc, ATTN_HEAD_DIM), F32),
                        pltpu.VMEM((ATTN_HEADS_PER_GROUP, ATTN_WINDOW_OFFSETS, ATTN_QUERY_BLOCK, ATTN_KEY_WINDOW), F32),
                        pltpu.VMEM((ATTN_BLOCKS_PER_ITER * ATTN_HEADS_PER_GROUP, ATTN_QUERY_BLOCK, ATTN_KEY_WINDOW), F32),
                        pltpu.VMEM((ATTN_BLOCKS_PER_ITER * ATTN_HEADS_PER_GROUP, ATTN_QUERY_BLOCK, ATTN_KEY_WINDOW), F32)],
        compiler_params=_params("arbitrary", "arbitrary"),
        name="attn_dil%d" % dilation,
    )(q_arr, k_arr, v_arr)
    T = batch * seq
    return o.reshape(T, gw), m.reshape(T, ATTN_HEAD_DIM), l.reshape(T, ATTN_HEAD_DIM)


def _running_sum_rows(x, row_index, reverse):
    n = x.shape[0]
    shift = 1
    while shift < n:
        if reverse:
            x = x + jnp.where(row_index < n - shift, pltpu.roll(x, n - shift, 0), 0.0)
        else:
            x = x + jnp.where(row_index >= shift, pltpu.roll(x, shift, 0), 0.0)
        shift *= 2
    return x


def _hgrn_kernel(qf_ref, vf_ref, lf_ref, kf_ref, qb_ref, vb_ref, lb_ref, kb_ref, g_ref, nw_ref, o_ref,
                 accf_ref, accb_ref, sf_ref, sb_ref):
    c = pl.program_id(2)
    nc = pl.num_programs(2)
    chunk = qf_ref.shape[1]
    seq = o_ref.shape[1]
    hp = HGRN_HEADS_PER_STEP
    blk = HGRN_BLOCK
    half = blk // 2
    nblk = chunk // blk
    dh = HGRN_HEAD_DIM
    pw = 2 * dh

    @pl.when(c == 0)
    def _():
        sf_ref[...] = jnp.zeros_like(sf_ref)
        sb_ref[...] = jnp.zeros_like(sb_ref)

    row_index = lax.broadcasted_iota(jnp.int32, (blk, pw), 0)
    ti = lax.broadcasted_iota(jnp.int32, (blk, 2 * blk), 0)
    si = lax.broadcasted_iota(jnp.int32, (blk, 2 * blk), 1) % blk
    causal = si <= ti
    anticausal = si >= ti
    zeros_kv = jnp.zeros((blk, dh), BF16)
    zeros_st = jnp.zeros((dh, dh), BF16)

    def block_diag(a, b, z):
        return jnp.concatenate([jnp.concatenate([a, z], axis=1), jnp.concatenate([z, b], axis=1)], axis=0)

    def load_unit(pair, r0, q_ref, v_ref, logf_ref, key_ref, s_ref):
        lanes = slice(pair * pw, (pair + 1) * pw)
        rows = pl.ds(r0, blk)
        return (q_ref[0, rows, lanes], v_ref[0, rows, lanes], logf_ref[0, rows, lanes], key_ref[0, rows, lanes],
                s_ref[2 * pair], s_ref[2 * pair + 1])

    def prepare_unit(loaded, backward):
        q, v, logf, k, st0, st1 = loaded
        a = _running_sum_rows(logf, row_index, backward)
        if backward:
            total = a[0:1, :]
            mid = a[half:half + 1, :]
        else:
            total = a[blk - 1:blk, :]
            mid = a[half - 1:half, :]
        q_mid = q * jnp.exp(a - mid).astype(BF16)
        k_mid_b = k * jnp.exp(mid - a).astype(BF16)
        q_in = q_mid * jnp.exp(mid).astype(BF16)
        k_out = k_mid_b * jnp.exp(total - mid).astype(BF16)
        decay = jnp.broadcast_to(jnp.exp(total), (SUBLANES, pw)).T[:, 0:1]
        return dict(q_mid=q_mid, k_diag=block_diag(k_mid_b[:, :dh], k_mid_b[:, dh:], zeros_kv),
                    q_in=q_in, s_diag=block_diag(st0.astype(BF16), st1.astype(BF16), zeros_st),
                    v=v, v_diag=block_diag(v[:, :dh], v[:, dh:], zeros_kv), k_out=k_out,
                    decay=decay, states=(st0, st1), mask=anticausal if backward else causal)

    nt = (((1,), (1,)), ((), ()))
    tn = (((0,), (0,)), ((), ()))

    def body(n, carry):
        rf = pl.multiple_of(n * blk, blk)
        rb = pl.multiple_of((nblk - 1 - n) * blk, blk)
        units = []
        for pair in range(hp // 2):
            units.append((pair, False, load_unit(pair, rf, qf_ref, vf_ref, lf_ref, kf_ref, sf_ref)))
            units.append((pair, True, load_unit(pair, rb, qb_ref, vb_ref, lb_ref, kb_ref, sb_ref)))
        prepared = [prepare_unit(loaded, backward) for _, backward, loaded in units]
        att = [lax.dot_general(u["q_mid"], u["k_diag"], nt, preferred_element_type=F32) for u in prepared]
        inter = [jnp.dot(u["q_in"], u["s_diag"], preferred_element_type=F32) for u in prepared]
        upd = [[lax.dot_general(u["k_out"][:, j * dh:(j + 1) * dh], u["v"][:, j * dh:(j + 1) * dh], tn,
                                preferred_element_type=F32) for j in range(2)] for u in prepared]
        att = [jnp.where(u["mask"], s, 0.0).astype(BF16) for u, s in zip(prepared, att)]
        intra = [jnp.dot(s, u["v_diag"], preferred_element_type=F32) for u, s in zip(prepared, att)]
        for i, (pair, backward, _) in enumerate(units):
            u = prepared[i]
            lanes = slice(pair * pw, (pair + 1) * pw)
            o = intra[i] + inter[i]
            states = [st * u["decay"][j * dh:(j + 1) * dh, :] + upd[i][j] for j, st in enumerate(u["states"])]
            if backward:
                accb_ref[pl.ds(pl.multiple_of((nc - 1 - c) * chunk + rb, blk), blk), lanes] = o
                sb_ref[2 * pair], sb_ref[2 * pair + 1] = states
            else:
                accf_ref[pl.ds(pl.multiple_of(c * chunk + rf, blk), blk), lanes] = o
                sf_ref[2 * pair], sf_ref[2 * pair + 1] = states
        return carry

    lax.fori_loop(0, nblk, body, 0)

    @pl.when(c == nc - 1)
    def _():
        fin = min(seq, 256)

        def finish(t, carry):
            rows = pl.ds(pl.multiple_of(t * fin, fin), fin)
            for h in range(hp):
                lanes = slice(h * dh, (h + 1) * dh)
                o = accf_ref[rows, lanes] + accb_ref[rows, lanes]
                o = o * lax.rsqrt(jnp.mean(o * o, axis=-1, keepdims=True) + RMS_EPS)
                o = o * nw_ref[:, lanes] * g_ref[0, rows, lanes].astype(F32)
                o_ref[0, rows, lanes] = o.astype(o_ref.dtype)
            return carry

        lax.fori_loop(0, seq // fin, finish, 0)


def _hgrn(qg3, plain3, logf3, key3, norm_w):
    batch, seq, _ = qg3.shape
    hp = HGRN_HEADS_PER_STEP
    w = hp * HGRN_HEAD_DIM
    chunk = min(seq, HGRN_CHUNK)
    nc = seq // chunk

    def fwd(offset):
        base = offset // w
        return pl.BlockSpec((1, chunk, w), lambda b, h, c: (b, c, base + h))

    def bwd(offset):
        base = offset // w
        return pl.BlockSpec((1, chunk, w), lambda b, h, c: (b, nc - 1 - c, base + h))

    gbase = HGRN_WIDTH // w
    return pl.pallas_call(
        _hgrn_kernel,
        grid=(batch, HGRN_HEADS // hp, nc),
        in_specs=[fwd(0), fwd(PLAIN_HI), fwd(0), fwd(0),
                  bwd(0), bwd(PLAIN_HI), bwd(HGRN_WIDTH), bwd(HGRN_WIDTH),
                  pl.BlockSpec((1, seq, w), lambda b, h, c: (b, 0, gbase + h)),
                  pl.BlockSpec((1, w), lambda b, h, c: (0, h))],
        out_specs=pl.BlockSpec((1, seq, w), lambda b, h, c: (b, 0, h)),
        out_shape=jax.ShapeDtypeStruct((batch, seq, HGRN_WIDTH), BF16),
        scratch_shapes=[pltpu.VMEM((seq, w), F32), pltpu.VMEM((seq, w), F32),
                        pltpu.VMEM((hp, HGRN_HEAD_DIM, HGRN_HEAD_DIM), F32),
                        pltpu.VMEM((hp, HGRN_HEAD_DIM, HGRN_HEAD_DIM), F32)],
        compiler_params=_params("parallel", "parallel", "arbitrary"),
        name="hgrn",
    )(qg3, plain3, logf3, key3, qg3, plain3, logf3, key3, qg3, norm_w)


def _pool_kernel(prev_ref, cur_ref, next_ref, pw_ref, ps_ref, o_ref, *, seq_len):
    i = pl.program_id(1)
    last = pl.num_programs(1) - 1
    ts = cur_ref.shape[1]
    n = ts + 2 * POOL_HALO
    pos = i * ts + lax.broadcasted_iota(jnp.int32, (ts, 1), 0)
    for g in range(POOL_GROUPS):
        lanes = slice(g * POOL_GROUP_DIM, (g + 1) * POOL_GROUP_DIM)
        half = POOL_WINDOWS[g] // 2
        cur = cur_ref[0, :, lanes].astype(F32)
        prev = jnp.where(i > 0, prev_ref[0, :, lanes].astype(F32), 0.0)
        nxt = jnp.where(i < last, next_ref[0, :, lanes].astype(F32), 0.0)
        ext = jnp.concatenate([prev, cur, nxt], axis=0)
        w = ext + pltpu.roll(ext, 1, 0)
        step = 1
        while step < half:
            w = pltpu.roll(w, step, 0) + pltpu.roll(w, n - step, 0)
            step *= 2
        wsum = w[POOL_HALO:POOL_HALO + ts]
        count = (jnp.minimum(pos + half, seq_len) - jnp.maximum(pos - half, 0)).astype(F32)
        mixed = wsum / count - cur
        y = jnp.dot(mixed.astype(BF16), pw_ref[g], preferred_element_type=F32)
        o_ref[0, :, lanes] = (y * ps_ref[:, lanes]).astype(o_ref.dtype)


def _pool(plain3, pool_w, pool_scale):
    batch, seq, _ = plain3.shape
    ts = min(seq, 512)
    hb = ts // POOL_HALO
    nhalo = seq // POOL_HALO
    kern = functools.partial(_pool_kernel, seq_len=seq)
    return pl.pallas_call(
        kern,
        grid=(batch, seq // ts),
        in_specs=[pl.BlockSpec((1, POOL_HALO, D_MODEL), lambda b, i: (b, jnp.maximum(i * hb - 1, 0), 0)),
                  pl.BlockSpec((1, ts, D_MODEL), lambda b, i: (b, i, 0)),
                  pl.BlockSpec((1, POOL_HALO, D_MODEL), lambda b, i: (b, jnp.minimum((i + 1) * hb, nhalo - 1), 0)),
                  pl.BlockSpec((POOL_GROUPS, POOL_GROUP_DIM, POOL_GROUP_DIM), lambda b, i: (0, 0, 0)),
                  pl.BlockSpec((1, D_MODEL), lambda b, i: (0, 0))],
        out_specs=pl.BlockSpec((1, ts, D_MODEL), lambda b, i: (b, i, 0)),
        out_shape=jax.ShapeDtypeStruct((batch, seq, D_MODEL), BF16),
        compiler_params=_params("parallel", "parallel"),
        name="pool",
    )(plain3, plain3, plain3, pool_w, pool_scale)


def _merge_kernel(ya_ref, o1_ref, o2_ref, o3_ref, m1_ref, m2_ref, m3_ref, l1_ref, l2_ref, l3_ref, hc_ref,
                  xin_ref, x_ref, wa_ref, wb_ref, wc_ref, wo_ref, wgate_ref, g_ref, b_ref, xo_ref, xb_ref,
                  *, alpha, nsub):
    d = D_MODEL
    dh = ATTN_HEAD_DIM
    sub = x_ref.shape[0] // nsub
    rows = [slice(s * sub, (s + 1) * sub) for s in range(nsub)]
    xin = [xin_ref[r, :] for r in rows]
    y_a = [jnp.dot(ya_ref[r, :], wa_ref[...], preferred_element_type=F32) for r in rows]
    y_c = [jnp.dot(hc_ref[r, :], wc_ref[...], preferred_element_type=F32) for r in rows]
    gates = [[jnp.dot(x, wgate_ref[:, n * d:(n + 1) * d], preferred_element_type=F32) for n in range(3)]
             for x in xin]
    att = []
    for r in rows:
        ms = [m1_ref[r, :], m2_ref[r, :], m3_ref[r, :]]
        ls = [l1_ref[r, :], l2_ref[r, :], l3_ref[r, :]]
        top = jnp.maximum(jnp.maximum(ms[0], ms[1]), ms[2])
        es = [jnp.exp2(m - top) for m in ms]
        inv = 1.0 / (es[0] * ls[0] + es[1] * ls[1] + es[2] * ls[2])
        ws = [e * inv for e in es]
        heads = []
        for h in range(ATTN_HEADS_PER_GROUP):
            lanes = slice(h * dh, (h + 1) * dh)
            stat = slice(h * ATTN_STAT_LANES, h * ATTN_STAT_LANES + 1)
            acc = None
            for w, o_ref in zip(ws, (o1_ref, o2_ref, o3_ref)):
                term = jnp.broadcast_to(w[:, stat], (sub, dh)) * o_ref[r, lanes].astype(F32)
                acc = term if acc is None else acc + term
            heads.append(acc)
        att.append(jnp.concatenate(heads, axis=1).astype(BF16))
    y_b = [jnp.dot(a, wb_ref[...], preferred_element_type=F32) for a in att]
    merged = [(_sigmoid(gates[s][0]) * y_a[s] + _sigmoid(gates[s][1]) * y_b[s]
               + _sigmoid(gates[s][2]) * y_c[s]).astype(BF16) for s in range(nsub)]
    mix = [jnp.dot(mg, wo_ref[...], preferred_element_type=F32) for mg in merged]
    for s, r in enumerate(rows):
        x1 = _layer_norm(alpha * x_ref[r, :] + mix[s], g_ref[...], b_ref[...])
        xo_ref[r, :] = x1
        xb_ref[r, :] = x1.astype(BF16)


def _merge(y_pool, o_list, m_list, l_list, y_hgrn, x_bf, x, wa, wb, wc, wo, wgate, ln_g, ln_b, alpha):
    T = x.shape[0]
    tm = min(T, MERGE_ROWS)
    gw = ATTN_GROUP_WIDTH
    d = D_MODEL
    row = lambda w: pl.BlockSpec((tm, w), lambda i: (i, 0))
    const = lambda a: pl.BlockSpec(a.shape, lambda i: (0,) * a.ndim, pipeline_mode=pl.Buffered(1))
    kern = functools.partial(_merge_kernel, alpha=alpha, nsub=MERGE_SUBTILES)
    stat = row(ATTN_HEAD_DIM)
    return pl.pallas_call(
        kern,
        grid=(T // tm,),
        in_specs=[row(d), row(gw), row(gw), row(gw), stat, stat, stat, stat, stat, stat, row(d), row(d),
                  row(d), const(wa), const(wb), const(wc), const(wo), const(wgate), const(ln_g), const(ln_b)],
        out_specs=[row(d), row(d)],
        out_shape=[jax.ShapeDtypeStruct((T, d), F32), jax.ShapeDtypeStruct((T, d), BF16)],
        compiler_params=_params("parallel"),
        name="merge",
    )(y_pool, *o_list, *m_list, *l_list, y_hgrn, x_bf, x, wa, wb, wc, wo, wgate, ln_g, ln_b)


def _ffn_kernel(xb_ref, x_ref, p_ref, wg_ref, wu_ref, wd_ref, wpp_ref, wpg_ref, g_ref, b_ref,
                xo_ref, xbo_ref, *, alpha, nsub):
    sub = xb_ref.shape[0] // nsub
    rows = [slice(s * sub, (s + 1) * sub) for s in range(nsub)]
    xb = [xb_ref[r, :] for r in rows]
    gate = [jnp.dot(x, wg_ref[...], preferred_element_type=F32) for x in xb]
    up = [jnp.dot(x, wu_ref[...], preferred_element_type=F32) for x in xb]
    pgate = [jnp.dot(x, wpg_ref[...], preferred_element_type=F32) for x in xb]
    ple = [jnp.dot(p_ref[r, :].astype(BF16), wpp_ref[...], preferred_element_type=F32) for r in rows]
    hidden = [(g * _sigmoid(g) * u).astype(BF16) for g, u in zip(gate, up)]
    ffn = [jnp.dot(h, wd_ref[...], preferred_element_type=F32) for h in hidden]
    for s, r in enumerate(rows):
        x2 = _layer_norm(alpha * x_ref[r, :] + ffn[s] + ple[s] * _sigmoid(pgate[s]), g_ref[...], b_ref[...])
        xo_ref[r, :] = x2
        xbo_ref[r, :] = x2.astype(BF16)


def _ffn(x1_bf, x1, p, wg, wu, wd, wpp, wpg, ln_g, ln_b, alpha):
    T = x1.shape[0]
    tm = min(T, FFN_ROWS)
    d = D_MODEL
    row = lambda w: pl.BlockSpec((tm, w), lambda i: (i, 0))
    const = lambda a: pl.BlockSpec(a.shape, lambda i: (0,) * a.ndim, pipeline_mode=pl.Buffered(1))
    kern = functools.partial(_ffn_kernel, alpha=alpha, nsub=FFN_SUBTILES)
    return pl.pallas_call(
        kern,
        grid=(T // tm,),
        in_specs=[row(d), row(d), row(PLE_DIM), const(wg), const(wu), const(wd), const(wpp),
                  const(wpg), const(ln_g), const(ln_b)],
        out_specs=[row(d), row(d)],
        out_shape=[jax.ShapeDtypeStruct((T, d), F32), jax.ShapeDtypeStruct((T, d), BF16)],
        compiler_params=_params("parallel"),
        name="ffn",
    )(x1_bf, x1, p, wg, wu, wd, wpp, wpg, ln_g, ln_b)


def _split_in_weights(w_in):
    d, aw, gw, hw = D_MODEL, ATTN_HEADS * ATTN_HEAD_DIM, ATTN_GROUP_WIDTH, HGRN_WIDTH
    pool = w_in[:, :, 0:d]
    aq = w_in[:, :, d:d + aw]
    ak = w_in[:, :, d + aw:d + 2 * aw]
    av = w_in[:, :, d + 2 * aw:d + 3 * aw]
    h0 = d + 3 * aw
    hq, hi, hff, hfb, hg = [w_in[:, :, h0 + n * hw:h0 + (n + 1) * hw] for n in range(5)]
    gates = w_in[:, :, h0 + 5 * hw:]
    qkv = lambda g: [t[:, :, g * gw:(g + 1) * gw] for t in (aq, ak, av)]
    cat = lambda parts: jnp.concatenate(parts, axis=-1).astype(BF16)
    return (cat([pool, hi] + qkv(0)), cat([hq, hg]), cat([hff, hfb]), gates.astype(BF16),
            cat(qkv(1)), cat(qkv(2)))


def kernel(x, p, w_in, pool_w, pool_scale, w_branch_a, w_branch_b, w_branch_c, hgrn_lb_logits, hgrn_norm_w, w_out, ln1_g, ln1_b, w_ffn_gate, w_ffn_up, w_ffn_down, w_ple_proj, w_ple_gate, ln2_g, ln2_b):
    batch, seq, d = x.shape
    depth = w_in.shape[0]
    T = batch * seq
    alpha = float((2 * depth) ** 0.25)

    lb = jnp.cumsum(jax.nn.softmax(hgrn_lb_logits.astype(F32), axis=0), axis=0)
    lb = lb - lb[:1]

    w_plain, w_silu, w_forget, w_gates, w_g2, w_g3 = _split_in_weights(w_in)
    bf = lambda a: a.astype(BF16)
    pool_w_b, wa, wb, wc, wo = bf(pool_w), bf(w_branch_a), bf(w_branch_b), bf(w_branch_c), bf(w_out)
    wg, wu, wd, wpp, wpg = bf(w_ffn_gate), bf(w_ffn_up), bf(w_ffn_down), bf(w_ple_proj), bf(w_ple_gate)

    slopes = [2.0 ** (-ALIBI_MAX_BIAS * (i + 1) / ATTN_HEADS) for i in range(ATTN_HEADS)]
    hpg = ATTN_HEADS_PER_GROUP
    gw = ATTN_GROUP_WIDTH

    xf = x.reshape(T, d).astype(F32)
    xb = xf.astype(BF16)
    for i in range(depth):
        qg, logf, key, plain = _inproj(xb, w_silu[i], w_forget[i], w_plain[i], lb[i:i + 1])
        plain3 = plain.reshape(batch, seq, PLAIN_WIDTH)
        o_list, m_list, l_list = [], [], []
        for g, (_, dil) in enumerate(ATTN_GROUPS):
            gs = tuple(slopes[g * hpg:(g + 1) * hpg])
            if dil == 1:
                qkv = plain.reshape(batch, 1, seq, PLAIN_WIDTH)
                cols = (PLAIN_AQ // gw, PLAIN_AK // gw, PLAIN_AV // gw)
            else:
                qkv = _inproj_dilated(xf, (w_g2, w_g3)[g - 1][i], batch, seq, dil)
                cols = (0, 1, 2)
            o, m, l = _attention_group(qkv, qkv, qkv, cols, batch, seq, dil, gs)
            o_list.append(o)
            m_list.append(m)
            l_list.append(l)
        y_hgrn = _hgrn(qg.reshape(batch, seq, 2 * HGRN_WIDTH), plain3,
                       logf.reshape(batch, seq, 2 * HGRN_WIDTH), key.reshape(batch, seq, 2 * HGRN_WIDTH),
                       hgrn_norm_w[i:i + 1])
        y_pool = _pool(plain3, pool_w_b[i], pool_scale[i:i + 1])
        x1, x1b = _merge(y_pool.reshape(T, d), o_list, m_list, l_list, y_hgrn.reshape(T, d), xb, xf,
                         wa[i], wb[i], wc[i], wo[i], w_gates[i], ln1_g[i:i + 1], ln1_b[i:i + 1], alpha)
        xf, xb = _ffn(x1b, x1, p[i].reshape(T, PLE_DIM), wg[i], wu[i], wd[i], wpp[i], wpg[i],
                      ln2_g[i:i + 1], ln2_b[i:i + 1], alpha)
    return xf.reshape(batch, seq, d).astype(x.dtype)
```

```python
import functools

import jax
import jax.numpy as jnp
from jax import lax
from jax.experimental import pallas as pl
from jax.experimental.pallas import tpu as pltpu

F32 = jnp.float32
BF16 = jnp.bfloat16
LANES = 128
SUBLANES = 8
LOG2_E = 1.4426950408889634
LN_2 = 0.6931471805599453

D_MODEL = 1024
PLE_DIM = 256
POOL_WINDOWS = (2, 4, 8, 16)
POOL_GROUPS = 4
POOL_GROUP_DIM = D_MODEL // POOL_GROUPS
ATTN_GROUPS = ((128, 1), (512, 4), (2048, 16))
ATTN_HEADS_PER_GROUP = 4
ATTN_HEADS = ATTN_HEADS_PER_GROUP * len(ATTN_GROUPS)
ATTN_HEAD_DIM = 128
ATTN_GROUP_WIDTH = ATTN_HEADS_PER_GROUP * ATTN_HEAD_DIM
ATTN_SIDE = 64
ALIBI_MAX_BIAS = 8.0
NEG_INF = -1e30
HGRN_HEADS = 8
HGRN_HEAD_DIM = 128
HGRN_WIDTH = HGRN_HEADS * HGRN_HEAD_DIM
FFN_HIDDEN = 2816
LN_EPS = 1e-5
RMS_EPS = 1e-6

PLAIN_POOL = 0
PLAIN_HI = 1024
PLAIN_AQ = 2048
PLAIN_AK = 2560
PLAIN_AV = 3072
PLAIN_WIDTH = 3584
QKV_WIDTH = 3 * ATTN_GROUP_WIDTH

INPROJ_COLUMN_STEPS = 2
INPROJ_ROWS = 512
ATTN_QUERY_BLOCK = 128
ATTN_KEY_WINDOW = ATTN_QUERY_BLOCK + 2 * ATTN_SIDE
ATTN_TOKENS_PER_STEP = 2048
ATTN_BLOCKS_PER_ITER = 2
ATTN_STAT_LANES = 32
ATTN_WINDOW_OFFSETS = 3
HGRN_BLOCK = 64
HGRN_HEADS_PER_STEP = 4
HGRN_CHUNK = 512
MERGE_ROWS = 512
MERGE_SUBTILES = 2
FFN_ROWS = 512
FFN_SUBTILES = 2
POOL_HALO = 16
VMEM_LIMIT = 52 * 1024 * 1024


def _params(*semantics):
    return pltpu.CompilerParams(dimension_semantics=semantics, vmem_limit_bytes=VMEM_LIMIT)


def _sigmoid(x):
    return 0.5 * jnp.tanh(0.5 * x) + 0.5


def _layer_norm(h, g, b):
    mu = jnp.mean(h, axis=-1, keepdims=True)
    d = h - mu
    var = jnp.mean(d * d, axis=-1, keepdims=True)
    return d * lax.rsqrt(var + LN_EPS) * g + b


def _inproj_kernel(x_ref, ws_ref, wf_ref, wp_ref, lb_ref, act_ref, logf_ref, key_ref, plain_ref):
    x = x_ref[...].astype(BF16)
    zs = jnp.dot(x, ws_ref[...], preferred_element_type=F32)
    zf = jnp.dot(x, wf_ref[...], preferred_element_type=F32)
    plain_ref[...] = jnp.dot(x, wp_ref[...], preferred_element_type=F32).astype(plain_ref.dtype)
    act_ref[...] = (zs * _sigmoid(zs)).astype(act_ref.dtype)
    lb = lb_ref[...]
    sig = _sigmoid(zf)
    logf_ref[...] = jnp.log(lb + (1.0 - lb) * sig)
    key_ref[...] = ((1.0 - lb) * (1.0 - sig)).astype(key_ref.dtype)


def _inproj(x_bf, w_silu, w_forget, w_plain, lb):
    T, D = x_bf.shape
    tm = min(T, INPROJ_ROWS)
    steps = INPROJ_COLUMN_STEPS
    ns, nf, np_ = w_silu.shape[1] // steps, w_forget.shape[1] // steps, w_plain.shape[1] // steps
    wspec = lambda n: pl.BlockSpec((D, n), lambda j, i: (0, j))
    ospec = lambda n: pl.BlockSpec((tm, n), lambda j, i: (i, j))
    oshape = lambda w, dt: jax.ShapeDtypeStruct((T, w.shape[1]), dt)
    return pl.pallas_call(
        _inproj_kernel,
        grid=(steps, T // tm),
        in_specs=[pl.BlockSpec((tm, D), lambda j, i: (i, 0)), wspec(ns), wspec(nf), wspec(np_),
                  pl.BlockSpec((1, nf), lambda j, i: (0, j))],
        out_specs=[ospec(ns), ospec(nf), ospec(nf), ospec(np_)],
        out_shape=[oshape(w_silu, BF16), oshape(w_forget, F32), oshape(w_forget, BF16), oshape(w_plain, BF16)],
        compiler_params=_params("parallel", "parallel"),
        name="inproj",
    )(x_bf, w_silu, w_forget, w_plain, lb)


def _inproj_dilated_kernel(x_ref, w_ref, o_ref, slab_ref, xp_ref, *, dilation):
    n = x_ref.shape[0] // dilation
    for c in range(D_MODEL // LANES):
        slab_ref[c] = x_ref[:, c * LANES:(c + 1) * LANES]
    for r in range(dilation):
        for c in range(D_MODEL // LANES):
            xp_ref[r * n:(r + 1) * n, c * LANES:(c + 1) * LANES] = (
                slab_ref[c, pl.ds(r, n, stride=dilation), :].astype(BF16))
    res = jnp.dot(xp_ref[...], w_ref[...], preferred_element_type=F32).astype(o_ref.dtype)
    for r in range(dilation):
        o_ref[0, r] = res[r * n:(r + 1) * n]


def _inproj_dilated(x_f32, w, batch, seq, dilation):
    N = w.shape[1]
    tb = min(seq, 1024)
    steps = seq // tb
    n = tb // dilation
    return pl.pallas_call(
        functools.partial(_inproj_dilated_kernel, dilation=dilation),
        grid=(batch * steps,),
        in_specs=[pl.BlockSpec((tb, D_MODEL), lambda i: (i, 0)),
                  pl.BlockSpec((D_MODEL, N), lambda i: (0, 0))],
        out_specs=pl.BlockSpec((1, dilation, n, N), lambda i: (i // steps, 0, i % steps, 0)),
        out_shape=jax.ShapeDtypeStruct((batch, dilation, seq // dilation, N), BF16),
        scratch_shapes=[pltpu.VMEM((D_MODEL // LANES, tb, LANES), F32), pltpu.VMEM((tb, D_MODEL), BF16)],
        compiler_params=_params("parallel"),
        name="inproj_dilated%d" % dilation,
    )(x_f32, w)


def _attn_kernel(q_ref, k_ref, v_ref, o_ref, m_ref, l_ref, oacc_ref, macc_ref, lacc_ref, bias_ref, sa_ref, sb_ref, *,
                 seq_len, dilation, slopes):
    c = pl.program_id(1)
    lq = q_ref.shape[2]
    qb, kw = ATTN_QUERY_BLOCK, ATTN_KEY_WINDOW
    nq = lq // qb
    per_iter = ATTN_BLOCKS_PER_ITER
    scale2 = ATTN_HEAD_DIM ** -0.5 * LOG2_E
    heads = range(ATTN_HEADS_PER_GROUP)
    lanes = [slice(h * ATTN_HEAD_DIM, (h + 1) * ATTN_HEAD_DIM) for h in heads]

    @pl.when((pl.program_id(0) == 0) & (c == 0))
    def _():
        row = lax.broadcasted_iota(jnp.int32, (qb, kw), 0)
        col = lax.broadcasted_iota(jnp.int32, (qb, kw), 1)
        for w in range(ATTN_WINDOW_OFFSETS):
            dist = jnp.abs(col - row - w * ATTN_SIDE)
            distf = dist.astype(F32) * float(dilation)
            for h in heads:
                bias_ref[h, w] = jnp.where(dist <= ATTN_SIDE, distf * (-slopes[h] * LOG2_E), NEG_INF * LOG2_E)

    stat_group = lax.broadcasted_iota(jnp.int32, (kw, ATTN_HEAD_DIM), 1) // ATTN_STAT_LANES
    ones_cols = [jnp.where(stat_group == h, 1.0, 0.0).astype(BF16) for h in heads]
    out_group = lax.broadcasted_iota(jnp.int32, (qb, ATTN_HEAD_DIM), 1) // ATTN_STAT_LANES

    def block_coords(idx):
        r = idx // nq
        r0 = pl.multiple_of((idx % nq) * qb, qb)
        t0 = c * lq + r0
        ks = pl.multiple_of(jnp.clip(t0 - ATTN_SIDE, 0, seq_len - kw), ATTN_SIDE)
        return r, r0, t0, ks

    def score_products(it, s_ref):
        for u in range(per_iter):
            r, r0, _, ks = block_coords(it * per_iter + u)
            q = q_ref[0, r, pl.ds(r0, qb), :]
            k = k_ref[0, r, pl.ds(ks, kw), :]
            for h in heads:
                s_ref[u * len(heads) + h] = lax.dot_general(
                    q[:, lanes[h]], k[:, lanes[h]], (((1,), (1,)), ((), ())), preferred_element_type=F32)

    n_iter = dilation * nq // per_iter
    score_products(0, sa_ref)

    def half_body(it, s_ref, next_ref):
        score_products(jnp.minimum(it + 1, n_iter - 1), next_ref)
        blocks = []
        for u in range(per_iter):
            r, r0, t0, ks = block_coords(it * per_iter + u)
            if dilation == 1:
                out_rows = pl.ds(r0, qb)
            else:
                out_rows = pl.ds(r0 * dilation + r, qb, stride=dilation)
            blocks.append(((t0 - ks) // ATTN_SIDE, out_rows, v_ref[0, r, pl.ds(ks, kw), :]))
        probs = []
        for u, (w, _, _) in enumerate(blocks):
            block_probs = []
            for h in heads:
                s = s_ref[u * len(heads) + h] * scale2 + bias_ref[h, w]
                m = jnp.max(s, axis=1, keepdims=True)
                block_probs.append((jnp.exp2(s - m).astype(BF16), m))
            probs.append(block_probs)
        pv = [[jnp.dot(block_probs[h][0], jnp.concatenate([v[:, lanes[h]], ones_cols[h]], axis=1),
                       preferred_element_type=F32) for h in heads]
              for (_, _, v), block_probs in zip(blocks, probs)]
        for (_, out_rows, _), block_probs, block_pv in zip(blocks, probs, pv):
            m_all = jnp.broadcast_to(block_probs[-1][1], (qb, ATTN_HEAD_DIM))
            l_all = block_pv[-1][:, ATTN_HEAD_DIM:]
            for h in reversed(heads[:-1]):
                m_all = jnp.where(out_group == h, block_probs[h][1], m_all)
                l_all = l_all + block_pv[h][:, ATTN_HEAD_DIM:]
            for h in heads:
                oacc_ref[h, out_rows, :] = block_pv[h][:, :ATTN_HEAD_DIM]
            macc_ref[out_rows, :] = m_all
            lacc_ref[out_rows, :] = l_all

    def body(it2, carry):
        half_body(2 * it2, sa_ref, sb_ref)
        half_body(2 * it2 + 1, sb_ref, sa_ref)
        return carry

    lax.fori_loop(0, n_iter // 2, body, 0)
    for h in heads:
        o_ref[0, :, lanes[h]] = oacc_ref[h].astype(o_ref.dtype)
    m_ref[0] = macc_ref[...]
    l_ref[0] = lacc_ref[...]


def _attention_group(q_arr, k_arr, v_arr, cols, batch, seq, dilation, slopes):
    L = seq // dilation
    gw = ATTN_GROUP_WIDTH
    tc = min(seq, ATTN_TOKENS_PER_STEP)
    lq = tc // dilation
    kern = functools.partial(_attn_kernel, seq_len=L, dilation=dilation, slopes=slopes)
    qc, kc, vc = cols
    stat = pl.BlockSpec((1, tc, ATTN_HEAD_DIM), lambda b, c: (b, c, 0))
    stat_shape = jax.ShapeDtypeStruct((batch, seq, ATTN_HEAD_DIM), F32)
    o, m, l = pl.pallas_call(
        kern,
        grid=(batch, seq // tc),
        in_specs=[pl.BlockSpec((1, dilation, lq, gw), lambda b, c: (b, 0, c, qc)),
                  pl.BlockSpec((1, dilation, L, gw), lambda b, c: (b, 0, 0, kc)),
                  pl.BlockSpec((1, dilation, L, gw), lambda b, c: (b, 0, 0, vc))],
        out_specs=[pl.BlockSpec((1, tc, gw), lambda b, c: (b, c, 0)), stat, stat],
        out_shape=[jax.ShapeDtypeStruct((batch, seq, gw), BF16), stat_shape, stat_shape],
        scratch_shapes=[pltpu.VMEM((ATTN_HEADS_PER_GROUP, tc, ATTN_HEAD_DIM), F32),
                        pltpu.VMEM((tc, ATTN_HEAD_DIM), F32), pltpu.VMEM((tc, ATTN_HEAD_DIM), F32),
                        pltpu.VMEM((ATTN_HEADS_PER_GROUP, ATTN_WINDOW_OFFSETS, ATTN_QUERY_BLOCK, ATTN_KEY_WINDOW), F32),
                        pltpu.VMEM((ATTN_BLOCKS_PER_ITER * ATTN_HEADS_PER_GROUP, ATTN_QUERY_BLOCK, ATTN_KEY_WINDOW), F32),
                        pltpu.VMEM((ATTN_BLOCKS_PER_ITER * ATTN_HEADS_PER_GROUP, ATTN_QUERY_BLOCK, ATTN_KEY_WINDOW), F32)],
        compiler_params=_params("arbitrary", "arbitrary"),
        name="attn_dil%d" % dilation,
    )(q_arr, k_arr, v_arr)
    T = batch * seq
    return o.reshape(T, gw), m.reshape(T, ATTN_HEAD_DIM), l.reshape(T, ATTN_HEAD_DIM)


def _running_sum_rows(x, row_index, reverse):
    n = x.shape[0]
    shift = 1
    while shift < n:
        if reverse:
            x = x + jnp.where(row_index < n - shift, pltpu.roll(x, n - shift, 0), 0.0)
        else:
            x = x + jnp.where(row_index >= shift, pltpu.roll(x, shift, 0), 0.0)
        shift *= 2
    return x


def _hgrn_kernel(qf_ref, vf_ref, lf_ref, kf_ref, qb_ref, vb_ref, lb_ref, kb_ref, g_ref, nw_ref, o_ref,
                 accf_ref, accb_ref, sf_ref, sb_ref):
    c = pl.program_id(2)
    nc = pl.num_programs(2)
    chunk = qf_ref.shape[1]
    seq = o_ref.shape[1]
    hp = HGRN_HEADS_PER_STEP
    blk = HGRN_BLOCK
    half = blk // 2
    nblk = chunk // blk
    dh = HGRN_HEAD_DIM
    pw = 2 * dh

    @pl.when(c == 0)
    def _():
        sf_ref[...] = jnp.zeros_like(sf_ref)
        sb_ref[...] = jnp.zeros_like(sb_ref)

    row_index = lax.broadcasted_iota(jnp.int32, (blk, pw), 0)
    ti = lax.broadcasted_iota(jnp.int32, (blk, 2 * blk), 0)
    si = lax.broadcasted_iota(jnp.int32, (blk, 2 * blk), 1) % blk
    causal = si <= ti
    anticausal = si >= ti
    zeros_kv = jnp.zeros((blk, dh), BF16)
    zeros_st = jnp.zeros((dh, dh), BF16)

    def block_diag(a, b, z):
        return jnp.concatenate([jnp.concatenate([a, z], axis=1), jnp.concatenate([z, b], axis=1)], axis=0)

    def load_unit(pair, r0, q_ref, v_ref, logf_ref, key_ref, s_ref):
        lanes = slice(pair * pw, (pair + 1) * pw)
        rows = pl.ds(r0, blk)
        return (q_ref[0, rows, lanes], v_ref[0, rows, lanes], logf_ref[0, rows, lanes], key_ref[0, rows, lanes],
                s_ref[2 * pair], s_ref[2 * pair + 1])

    def prepare_unit(loaded, backward):
        q, v, logf, k, st0, st1 = loaded
        a = _running_sum_rows(logf, row_index, backward)
        if backward:
            total = a[0:1, :]
            mid = a[half:half + 1, :]
        else:
            total = a[blk - 1:blk, :]
            mid = a[half - 1:half, :]
        q_mid = q * jnp.exp(a - mid).astype(BF16)
        k_mid_b = k * jnp.exp(mid - a).astype(BF16)
        q_in = q_mid * jnp.exp(mid).astype(BF16)
        k_out = k_mid_b * jnp.exp(total - mid).astype(BF16)
        decay = jnp.broadcast_to(jnp.exp(total), (SUBLANES, pw)).T[:, 0:1]
        return dict(q_mid=q_mid, k_diag=block_diag(k_mid_b[:, :dh], k_mid_b[:, dh:], zeros_kv),
                    q_in=q_in, s_diag=block_diag(st0.astype(BF16), st1.astype(BF16), zeros_st),
                    v=v, v_diag=block_diag(v[:, :dh], v[:, dh:], zeros_kv), k_out=k_out,
                    decay=decay, states=(st0, st1), mask=anticausal if backward else causal)

    nt = (((1,), (1,)), ((), ()))
    tn = (((0,), (0,)), ((), ()))

    def body(n, carry):
        rf = pl.multiple_of(n * blk, blk)
        rb = pl.multiple_of((nblk - 1 - n) * blk, blk)
        units = []
        for pair in range(hp // 2):
            units.append((pair, False, load_unit(pair, rf, qf_ref, vf_ref, lf_ref, kf_ref, sf_ref)))
            units.append((pair, True, load_unit(pair, rb, qb_ref, vb_ref, lb_ref, kb_ref, sb_ref)))
        prepared = [prepare_unit(loaded, backward) for _, backward, loaded in units]
        att = [lax.dot_general(u["q_mid"], u["k_diag"], nt, preferred_element_type=F32) for u in prepared]
        inter = [jnp.dot(u["q_in"], u["s_diag"], preferred_element_type=F32) for u in prepared]
        upd = [[lax.dot_general(u["k_out"][:, j * dh:(j + 1) * dh], u["v"][:, j * dh:(j + 1) * dh], tn,
                                preferred_element_type=F32) for j in range(2)] for u in prepared]
        att = [jnp.where(u["mask"], s, 0.0).astype(BF16) for u, s in zip(prepared, att)]
        intra = [jnp.dot(s, u["v_diag"], preferred_element_type=F32) for u, s in zip(prepared, att)]
        for i, (pair, backward, _) in enumerate(units):
            u = prepared[i]
            lanes = slice(pair * pw, (pair + 1) * pw)
            o = intra[i] + inter[i]
            states = [st * u["decay"][j * dh:(j + 1) * dh, :] + upd[i][j] for j, st in enumerate(u["states"])]
            if backward:
                accb_ref[pl.ds(pl.multiple_of((nc - 1 - c) * chunk + rb, blk), blk), lanes] = o
                sb_ref[2 * pair], sb_ref[2 * pair + 1] = states
            else:
                accf_ref[pl.ds(pl.multiple_of(c * chunk + rf, blk), blk), lanes] = o
                sf_ref[2 * pair], sf_ref[2 * pair + 1] = states
        return carry

    lax.fori_loop(0, nblk, body, 0)

    @pl.when(c == nc - 1)
    def _():
        fin = min(seq, 256)

        def finish(t, carry):
            rows = pl.ds(pl.multiple_of(t * fin, fin), fin)
            for h in range(hp):
                lanes = slice(h * dh, (h + 1) * dh)
                o = accf_ref[rows, lanes] + accb_ref[rows, lanes]
                o = o * lax.rsqrt(jnp.mean(o * o, axis=-1, keepdims=True) + RMS_EPS)
                o = o * nw_ref[:, lanes] * g_ref[0, rows, lanes].astype(F32)
                o_ref[0, rows, lanes] = o.astype(o_ref.dtype)
            return carry

        lax.fori_loop(0, seq // fin, finish, 0)


def _hgrn(qg3, plain3, logf3, key3, norm_w):
    batch, seq, _ = qg3.shape
    hp = HGRN_HEADS_PER_STEP
    w = hp * HGRN_HEAD_DIM
    chunk = min(seq, HGRN_CHUNK)
    nc = seq // chunk

    def fwd(offset):
        base = offset // w
        return pl.BlockSpec((1, chunk, w), lambda b, h, c: (b, c, base + h))

    def bwd(offset):
        base = offset // w
        return pl.BlockSpec((1, chunk, w), lambda b, h, c: (b, nc - 1 - c, base + h))

    gbase = HGRN_WIDTH // w
    return pl.pallas_call(
        _hgrn_kernel,
        grid=(batch, HGRN_HEADS // hp, nc),
        in_specs=[fwd(0), fwd(PLAIN_HI), fwd(0), fwd(0),
                  bwd(0), bwd(PLAIN_HI), bwd(HGRN_WIDTH), bwd(HGRN_WIDTH),
                  pl.BlockSpec((1, seq, w), lambda b, h, c: (b, 0, gbase + h)),
                  pl.BlockSpec((1, w), lambda b, h, c: (0, h))],
        out_specs=pl.BlockSpec((1, seq, w), lambda b, h, c: (b, 0, h)),
        out_shape=jax.ShapeDtypeStruct((batch, seq, HGRN_WIDTH), BF16),
        scratch_shapes=[pltpu.VMEM((seq, w), F32), pltpu.VMEM((seq, w), F32),
                        pltpu.VMEM((hp, HGRN_HEAD_DIM, HGRN_HEAD_DIM), F32),
                        pltpu.VMEM((hp, HGRN_HEAD_DIM, HGRN_HEAD_DIM), F32)],
        compiler_params=_params("parallel", "parallel", "arbitrary"),
        name="hgrn",
    )(qg3, plain3, logf3, key3, qg3, plain3, logf3, key3, qg3, norm_w)


def _pool_kernel(prev_ref, cur_ref, next_ref, pw_ref, ps_ref, o_ref, *, seq_len):
    i = pl.program_id(1)
    last = pl.num_programs(1) - 1
    ts = cur_ref.shape[1]
    n = ts + 2 * POOL_HALO
    pos = i * ts + lax.broadcasted_iota(jnp.int32, (ts, 1), 0)
    for g in range(POOL_GROUPS):
        lanes = slice(g * POOL_GROUP_DIM, (g + 1) * POOL_GROUP_DIM)
        half = POOL_WINDOWS[g] // 2
        cur = cur_ref[0, :, lanes].astype(F32)
        prev = jnp.where(i > 0, prev_ref[0, :, lanes].astype(F32), 0.0)
        nxt = jnp.where(i < last, next_ref[0, :, lanes].astype(F32), 0.0)
        ext = jnp.concatenate([prev, cur, nxt], axis=0)
        w = ext + pltpu.roll(ext, 1, 0)
        step = 1
        while step < half:
            w = pltpu.roll(w, step, 0) + pltpu.roll(w, n - step, 0)
            step *= 2
        wsum = w[POOL_HALO:POOL_HALO + ts]
        count = (jnp.minimum(pos + half, seq_len) - jnp.maximum(pos - half, 0)).astype(F32)
        mixed = wsum / count - cur
        y = jnp.dot(mixed.astype(BF16), pw_ref[g], preferred_element_type=F32)
        o_ref[0, :, lanes] = (y * ps_ref[:, lanes]).astype(o_ref.dtype)


def _pool(plain3, pool_w, pool_scale):
    batch, seq, _ = plain3.shape
    ts = min(seq, 512)
    hb = ts // POOL_HALO
    nhalo = seq // POOL_HALO
    kern = functools.partial(_pool_kernel, seq_len=seq)
    return pl.pallas_call(
        kern,
        grid=(batch, seq // ts),
        in_specs=[pl.BlockSpec((1, POOL_HALO, D_MODEL), lambda b, i: (b, jnp.maximum(i * hb - 1, 0), 0)),
                  pl.BlockSpec((1, ts, D_MODEL), lambda b, i: (b, i, 0)),
                  pl.BlockSpec((1, POOL_HALO, D_MODEL), lambda b, i: (b, jnp.minimum((i + 1) * hb, nhalo - 1), 0)),
                  pl.BlockSpec((POOL_GROUPS, POOL_GROUP_DIM, POOL_GROUP_DIM), lambda b, i: (0, 0, 0)),
                  pl.BlockSpec((1, D_MODEL), lambda b, i: (0, 0))],
        out_specs=pl.BlockSpec((1, ts, D_MODEL), lambda b, i: (b, i, 0)),
        out_shape=jax.ShapeDtypeStruct((batch, seq, D_MODEL), BF16),
        compiler_params=_params("parallel", "parallel"),
        name="pool",
    )(plain3, plain3, plain3, pool_w, pool_scale)


def _merge_kernel(ya_ref, o1_ref, o2_ref, o3_ref, m1_ref, m2_ref, m3_ref, l1_ref, l2_ref, l3_ref, hc_ref,
                  xin_ref, x_ref, wa_ref, wb_ref, wc_ref, wo_ref, wgate_ref, g_ref, b_ref, xo_ref, xb_ref,
                  *, alpha, nsub):
    d = D_MODEL
    dh = ATTN_HEAD_DIM
    sub = x_ref.shape[0] // nsub
    rows = [slice(s * sub, (s + 1) * sub) for s in range(nsub)]
    xin = [xin_ref[r, :].astype(BF16) for r in rows]
    y_a = [jnp.dot(ya_ref[r, :], wa_ref[...], preferred_element_type=F32) for r in rows]
    y_c = [jnp.dot(hc_ref[r, :], wc_ref[...], preferred_element_type=F32) for r in rows]
    gates = [[jnp.dot(x, wgate_ref[:, n * d:(n + 1) * d], preferred_element_type=F32) for n in range(3)]
             for x in xin]
    att = []
    for r in rows:
        ms = [m1_ref[r, :], m2_ref[r, :], m3_ref[r, :]]
        ls = [l1_ref[r, :], l2_ref[r, :], l3_ref[r, :]]
        top = jnp.maximum(jnp.maximum(ms[0], ms[1]), ms[2])
        es = [jnp.exp2(m - top) for m in ms]
        inv = 1.0 / (es[0] * ls[0] + es[1] * ls[1] + es[2] * ls[2])
        ws = [e * inv for e in es]
        heads = []
        for h in range(ATTN_HEADS_PER_GROUP):
            lanes = slice(h * dh, (h + 1) * dh)
            stat = slice(h * ATTN_STAT_LANES, h * ATTN_STAT_LANES + 1)
            acc = None
            for w, o_ref in zip(ws, (o1_ref, o2_ref, o3_ref)):
                term = jnp.broadcast_to(w[:, stat], (sub, dh)) * o_ref[r, lanes].astype(F32)
                acc = term if acc is None else acc + term
            heads.append(acc)
        att.append(jnp.concatenate(heads, axis=1).astype(BF16))
    y_b = [jnp.dot(a, wb_ref[...], preferred_element_type=F32) for a in att]
    merged = [(_sigmoid(gates[s][0]) * y_a[s] + _sigmoid(gates[s][1]) * y_b[s]
               + _sigmoid(gates[s][2]) * y_c[s]).astype(BF16) for s in range(nsub)]
    mix = [jnp.dot(mg, wo_ref[...], preferred_element_type=F32) for mg in merged]
    for s, r in enumerate(rows):
        x1 = _layer_norm(alpha * x_ref[r, :] + mix[s], g_ref[...], b_ref[...])
        xo_ref[r, :] = x1
        xb_ref[r, :] = x1.astype(BF16)


def _merge(y_pool, o_list, m_list, l_list, y_hgrn, x_bf, x, wa, wb, wc, wo, wgate, ln_g, ln_b, alpha):
    T = x.shape[0]
    tm = min(T, MERGE_ROWS)
    gw = ATTN_GROUP_WIDTH
    d = D_MODEL
    row = lambda w: pl.BlockSpec((tm, w), lambda i: (i, 0))
    const = lambda a: pl.BlockSpec(a.shape, lambda i: (0,) * a.ndim, pipeline_mode=pl.Buffered(1))
    kern = functools.partial(_merge_kernel, alpha=alpha, nsub=MERGE_SUBTILES)
    stat = row(ATTN_HEAD_DIM)
    return pl.pallas_call(
        kern,
        grid=(T // tm,),
        in_specs=[row(d), row(gw), row(gw), row(gw), stat, stat, stat, stat, stat, stat, row(d), row(d),
                  row(d), const(wa), const(wb), const(wc), const(wo), const(wgate), const(ln_g), const(ln_b)],
        out_specs=[row(d), row(d)],
        out_shape=[jax.ShapeDtypeStruct((T, d), F32), jax.ShapeDtypeStruct((T, d), BF16)],
        compiler_params=_params("parallel"),
        name="merge",
    )(y_pool, *o_list, *m_list, *l_list, y_hgrn, x_bf, x, wa, wb, wc, wo, wgate, ln_g, ln_b)


def _ffn_kernel(xb_ref, x_ref, p_ref, wg_ref, wu_ref, wd_ref, wpp_ref, wpg_ref, g_ref, b_ref,
                xo_ref, xbo_ref, *, alpha, nsub):
    sub = xb_ref.shape[0] // nsub
    rows = [slice(s * sub, (s + 1) * sub) for s in range(nsub)]
    xb = [xb_ref[r, :] for r in rows]
    gate = [jnp.dot(x, wg_ref[...], preferred_element_type=F32) for x in xb]
    up = [jnp.dot(x, wu_ref[...], preferred_element_type=F32) for x in xb]
    pgate = [jnp.dot(x, wpg_ref[...], preferred_element_type=F32) for x in xb]
    ple = [jnp.dot(p_ref[r, :].astype(BF16), wpp_ref[...], preferred_element_type=F32) for r in rows]
    hidden = [(g * _sigmoid(g) * u).astype(BF16) for g, u in zip(gate, up)]
    ffn = [jnp.dot(h, wd_ref[...], preferred_element_type=F32) for h in hidden]
    for s, r in enumerate(rows):
        x2 = _layer_norm(alpha * x_ref[r, :] + ffn[s] + ple[s] * _sigmoid(pgate[s]), g_ref[...], b_ref[...])
        xo_ref[r, :] = x2
        xbo_ref[r, :] = x2.astype(BF16)


def _ffn(x1_bf, x1, p, wg, wu, wd, wpp, wpg, ln_g, ln_b, alpha):
    T = x1.shape[0]
    tm = min(T, FFN_ROWS)
    d = D_MODEL
    row = lambda w: pl.BlockSpec((tm, w), lambda i: (i, 0))
    const = lambda a: pl.BlockSpec(a.shape, lambda i: (0,) * a.ndim, pipeline_mode=pl.Buffered(1))
    kern = functools.partial(_ffn_kernel, alpha=alpha, nsub=FFN_SUBTILES)
    return pl.pallas_call(
        kern,
        grid=(T // tm,),
        in_specs=[row(d), row(d), row(PLE_DIM), const(wg), const(wu), const(wd), const(wpp),
                  const(wpg), const(ln_g), const(ln_b)],
        out_specs=[row(d), row(d)],
        out_shape=[jax.ShapeDtypeStruct((T, d), F32), jax.ShapeDtypeStruct((T, d), BF16)],
        compiler_params=_params("parallel"),
        name="ffn",
    )(x1_bf, x1, p, wg, wu, wd, wpp, wpg, ln_g, ln_b)


def _split_in_weights(w_in):
    d, aw, gw, hw = D_MODEL, ATTN_HEADS * ATTN_HEAD_DIM, ATTN_GROUP_WIDTH, HGRN_WIDTH
    pool = w_in[:, :, 0:d]
    aq = w_in[:, :, d:d + aw]
    ak = w_in[:, :, d + aw:d + 2 * aw]
    av = w_in[:, :, d + 2 * aw:d + 3 * aw]
    h0 = d + 3 * aw
    hq, hi, hff, hfb, hg = [w_in[:, :, h0 + n * hw:h0 + (n + 1) * hw] for n in range(5)]
    gates = w_in[:, :, h0 + 5 * hw:]
    qkv = lambda g: [t[:, :, g * gw:(g + 1) * gw] for t in (aq, ak, av)]
    cat = lambda parts: jnp.concatenate(parts, axis=-1).astype(BF16)
    return (cat([pool, hi] + qkv(0)), cat([hq, hg]), cat([hff, hfb]), gates.astype(BF16),
            cat(qkv(1)), cat(qkv(2)))


def kernel(x, p, w_in, pool_w, pool_scale, w_branch_a, w_branch_b, w_branch_c, hgrn_lb_logits, hgrn_norm_w, w_out, ln1_g, ln1_b, w_ffn_gate, w_ffn_up, w_ffn_down, w_ple_proj, w_ple_gate, ln2_g, ln2_b):
    batch, seq, d = x.shape
    depth = w_in.shape[0]
    T = batch * seq
    alpha = float((2 * depth) ** 0.25)

    lb = jnp.cumsum(jax.nn.softmax(hgrn_lb_logits.astype(F32), axis=0), axis=0)
    lb = lb - lb[:1]

    w_plain, w_silu, w_forget, w_gates, w_g2, w_g3 = _split_in_weights(w_in)
    bf = lambda a: a.astype(BF16)
    pool_w_b, wa, wb, wc, wo = bf(pool_w), bf(w_branch_a), bf(w_branch_b), bf(w_branch_c), bf(w_out)
    wg, wu, wd, wpp, wpg = bf(w_ffn_gate), bf(w_ffn_up), bf(w_ffn_down), bf(w_ple_proj), bf(w_ple_gate)

    slopes = [2.0 ** (-ALIBI_MAX_BIAS * (i + 1) / ATTN_HEADS) for i in range(ATTN_HEADS)]
    hpg = ATTN_HEADS_PER_GROUP
    gw = ATTN_GROUP_WIDTH

    xf = x.reshape(T, d).astype(F32)
    xb = xf
    for i in range(depth):
        qg, logf, key, plain = _inproj(xb, w_silu[i], w_forget[i], w_plain[i], lb[i:i + 1])
        plain3 = plain.reshape(batch, seq, PLAIN_WIDTH)
        o_list, m_list, l_list = [], [], []
        for g, (_, dil) in enumerate(ATTN_GROUPS):
            gs = tuple(slopes[g * hpg:(g + 1) * hpg])
            if dil == 1:
                qkv = plain.reshape(batch, 1, seq, PLAIN_WIDTH)
                cols = (PLAIN_AQ // gw, PLAIN_AK // gw, PLAIN_AV // gw)
            else:
                qkv = _inproj_dilated(xf, (w_g2, w_g3)[g - 1][i], batch, seq, dil)
                cols = (0, 1, 2)
            o, m, l = _attention_group(qkv, qkv, qkv, cols, batch, seq, dil, gs)
            o_list.append(o)
            m_list.append(m)
            l_list.append(l)
        y_hgrn = _hgrn(qg.reshape(batch, seq, 2 * HGRN_WIDTH), plain3,
                       logf.reshape(batch, seq, 2 * HGRN_WIDTH), key.reshape(batch, seq, 2 * HGRN_WIDTH),
                       hgrn_norm_w[i:i + 1])
        y_pool = _pool(plain3, pool_w_b[i], pool_scale[i:i + 1])
        x1, x1b = _merge(y_pool.reshape(T, d), o_list, m_list, l_list, y_hgrn.reshape(T, d), xb, xf,
                         wa[i], wb[i], wc[i], wo[i], w_gates[i], ln1_g[i:i + 1], ln1_b[i:i + 1], alpha)
        xf, xb = _ffn(x1b, x1, p[i].reshape(T, PLE_DIM), wg[i], wu[i], wd[i], wpp[i], wpg[i],
                      ln2_g[i:i + 1], ln2_b[i:i + 1], alpha)
    return xf.reshape(batch, seq, d).astype(x.dtype)
```

```python
import functools

import jax
import jax.numpy as jnp
from jax import lax
from jax.experimental import pallas as pl
from jax.experimental.pallas import tpu as pltpu

F32 = jnp.float32
BF16 = jnp.bfloat16
LANES = 128
SUBLANES = 8
LOG2_E = 1.4426950408889634
LN_2 = 0.6931471805599453

D_MODEL = 1024
PLE_DIM = 256
POOL_WINDOWS = (2, 4, 8, 16)
POOL_GROUPS = 4
POOL_GROUP_DIM = D_MODEL // POOL_GROUPS
ATTN_GROUPS = ((128, 1), (512, 4), (2048, 16))
ATTN_HEADS_PER_GROUP = 4
ATTN_HEADS = ATTN_HEADS_PER_GROUP * len(ATTN_GROUPS)
ATTN_HEAD_DIM = 128
ATTN_GROUP_WIDTH = ATTN_HEADS_PER_GROUP * ATTN_HEAD_DIM
ATTN_SIDE = 64
ALIBI_MAX_BIAS = 8.0
NEG_INF = -1e30
HGRN_HEADS = 8
HGRN_HEAD_DIM = 128
HGRN_WIDTH = HGRN_HEADS * HGRN_HEAD_DIM
FFN_HIDDEN = 2816
LN_EPS = 1e-5
RMS_EPS = 1e-6

PLAIN_POOL = 0
PLAIN_HI = 1024
PLAIN_AQ = 2048
PLAIN_AK = 2560
PLAIN_AV = 3072
PLAIN_WIDTH = 3584
QKV_WIDTH = 3 * ATTN_GROUP_WIDTH

INPROJ_COLUMN_STEPS = 2
INPROJ_ROWS = 512
ATTN_QUERY_BLOCK = 128
ATTN_KEY_WINDOW = ATTN_QUERY_BLOCK + 2 * ATTN_SIDE
ATTN_TOKENS_PER_STEP = 2048
ATTN_BLOCKS_PER_ITER = 2
ATTN_STAT_LANES = 32
ATTN_WINDOW_OFFSETS = 3
HGRN_BLOCK = 64
HGRN_HEADS_PER_STEP = 4
HGRN_CHUNK = 512
MERGE_ROWS = 512
MERGE_SUBTILES = 2
FFN_ROWS = 512
FFN_SUBTILES = 2
POOL_HALO = 16
VMEM_LIMIT = 52 * 1024 * 1024


def _params(*semantics):
    return pltpu.CompilerParams(dimension_semantics=semantics, vmem_limit_bytes=VMEM_LIMIT)


def _sigmoid(x):
    return 0.5 * jnp.tanh(0.5 * x) + 0.5


def _layer_norm(h, g, b):
    mu = jnp.mean(h, axis=-1, keepdims=True)
    d = h - mu
    var = jnp.mean(d * d, axis=-1, keepdims=True)
    return d * lax.rsqrt(var + LN_EPS) * g + b


def _inproj_kernel(x_ref, ws_ref, wf_ref, wp_ref, lb_ref, act_ref, logf_ref, key_ref, plain_ref):
    x = x_ref[...].astype(BF16)
    zs = jnp.dot(x, ws_ref[...], preferred_element_type=F32)
    zf = jnp.dot(x, wf_ref[...], preferred_element_type=F32)
    plain_ref[...] = jnp.dot(x, wp_ref[...], preferred_element_type=F32).astype(plain_ref.dtype)
    act_ref[...] = (zs * _sigmoid(zs)).astype(act_ref.dtype)
    lb = lb_ref[...]
    sig = _sigmoid(zf)
    logf_ref[...] = jnp.log(lb + (1.0 - lb) * sig)
    key_ref[...] = ((1.0 - lb) * (1.0 - sig)).astype(key_ref.dtype)


def _inproj(x_bf, w_silu, w_forget, w_plain, lb):
    T, D = x_bf.shape
    tm = min(T, INPROJ_ROWS)
    steps = INPROJ_COLUMN_STEPS
    ns, nf, np_ = w_silu.shape[1] // steps, w_forget.shape[1] // steps, w_plain.shape[1] // steps
    wspec = lambda n: pl.BlockSpec((D, n), lambda j, i: (0, j))
    ospec = lambda n: pl.BlockSpec((tm, n), lambda j, i: (i, j))
    oshape = lambda w, dt: jax.ShapeDtypeStruct((T, w.shape[1]), dt)
    return pl.pallas_call(
        _inproj_kernel,
        grid=(steps, T // tm),
        in_specs=[pl.BlockSpec((tm, D), lambda j, i: (i, 0)), wspec(ns), wspec(nf), wspec(np_),
                  pl.BlockSpec((1, nf), lambda j, i: (0, j))],
        out_specs=[ospec(ns), ospec(nf), ospec(nf), ospec(np_)],
        out_shape=[oshape(w_silu, BF16), oshape(w_forget, F32), oshape(w_forget, BF16), oshape(w_plain, BF16)],
        compiler_params=_params("parallel", "parallel"),
        name="inproj",
    )(x_bf, w_silu, w_forget, w_plain, lb)


def _inproj_dilated_kernel(x_ref, w_ref, o_ref, slab_ref, xp_ref, *, dilation):
    n = x_ref.shape[0] // dilation
    for c in range(D_MODEL // LANES):
        slab_ref[c] = x_ref[:, c * LANES:(c + 1) * LANES]
    for r in range(dilation):
        for c in range(D_MODEL // LANES):
            xp_ref[r * n:(r + 1) * n, c * LANES:(c + 1) * LANES] = (
                slab_ref[c, pl.ds(r, n, stride=dilation), :].astype(BF16))
    res = jnp.dot(xp_ref[...], w_ref[...], preferred_element_type=F32).astype(o_ref.dtype)
    for r in range(dilation):
        o_ref[0, r] = res[r * n:(r + 1) * n]


def _inproj_dilated(x_f32, w, batch, seq, dilation):
    N = w.shape[1]
    tb = min(seq, 1024)
    steps = seq // tb
    n = tb // dilation
    return pl.pallas_call(
        functools.partial(_inproj_dilated_kernel, dilation=dilation),
        grid=(batch * steps,),
        in_specs=[pl.BlockSpec((tb, D_MODEL), lambda i: (i, 0)),
                  pl.BlockSpec((D_MODEL, N), lambda i: (0, 0))],
        out_specs=pl.BlockSpec((1, dilation, n, N), lambda i: (i // steps, 0, i % steps, 0)),
        out_shape=jax.ShapeDtypeStruct((batch, dilation, seq // dilation, N), BF16),
        scratch_shapes=[pltpu.VMEM((D_MODEL // LANES, tb, LANES), F32), pltpu.VMEM((tb, D_MODEL), BF16)],
        compiler_params=_params("parallel"),
        name="inproj_dilated%d" % dilation,
    )(x_f32, w)


def _attn_kernel(q_ref, k_ref, v_ref, o_ref, m_ref, l_ref, oacc_ref, macc_ref, lacc_ref, bias_ref, sa_ref, sb_ref, *,
                 seq_len, dilation, slopes):
    c = pl.program_id(1)
    lq = q_ref.shape[2]
    qb, kw = ATTN_QUERY_BLOCK, ATTN_KEY_WINDOW
    nq = lq // qb
    per_iter = ATTN_BLOCKS_PER_ITER
    scale2 = ATTN_HEAD_DIM ** -0.5 * LOG2_E
    heads = range(ATTN_HEADS_PER_GROUP)
    lanes = [slice(h * ATTN_HEAD_DIM, (h + 1) * ATTN_HEAD_DIM) for h in heads]

    @pl.when((pl.program_id(0) == 0) & (c == 0))
    def _():
        row = lax.broadcasted_iota(jnp.int32, (qb, kw), 0)
        col = lax.broadcasted_iota(jnp.int32, (qb, kw), 1)
        for w in range(ATTN_WINDOW_OFFSETS):
            dist = jnp.abs(col - row - w * ATTN_SIDE)
            distf = dist.astype(F32) * float(dilation)
            for h in heads:
                bias_ref[h, w] = jnp.where(dist <= ATTN_SIDE, distf * (-slopes[h] * LOG2_E), NEG_INF * LOG2_E)

    stat_group = lax.broadcasted_iota(jnp.int32, (kw, ATTN_HEAD_DIM), 1) // ATTN_STAT_LANES
    ones_cols = [jnp.where(stat_group == h, 1.0, 0.0).astype(BF16) for h in heads]
    out_group = lax.broadcasted_iota(jnp.int32, (qb, ATTN_HEAD_DIM), 1) // ATTN_STAT_LANES

    def block_coords(idx):
        r = idx // nq
        r0 = pl.multiple_of((idx % nq) * qb, qb)
        t0 = c * lq + r0
        ks = pl.multiple_of(jnp.clip(t0 - ATTN_SIDE, 0, seq_len - kw), ATTN_SIDE)
        return r, r0, t0, ks

    def score_products(it, s_ref):
        for u in range(per_iter):
            r, r0, _, ks = block_coords(it * per_iter + u)
            q = q_ref[0, r, pl.ds(r0, qb), :]
            k = k_ref[0, r, pl.ds(ks, kw), :]
            for h in heads:
                s_ref[u * len(heads) + h] = lax.dot_general(
                    q[:, lanes[h]], k[:, lanes[h]], (((1,), (1,)), ((), ())), preferred_element_type=F32)

    n_iter = dilation * nq // per_iter
    score_products(0, sa_ref)

    def half_body(it, s_ref, next_ref):
        score_products(jnp.minimum(it + 1, n_iter - 1), next_ref)
        blocks = []
        for u in range(per_iter):
            r, r0, t0, ks = block_coords(it * per_iter + u)
            if dilation == 1:
                out_rows = pl.ds(r0, qb)
            else:
                out_rows = pl.ds(r0 * dilation + r, qb, stride=dilation)
            blocks.append(((t0 - ks) // ATTN_SIDE, out_rows, v_ref[0, r, pl.ds(ks, kw), :]))
        probs = []
        for u, (w, _, _) in enumerate(blocks):
            block_probs = []
            for h in heads:
                s = s_ref[u * len(heads) + h] * scale2 + bias_ref[h, w]
                m = jnp.max(s, axis=1, keepdims=True)
                block_probs.append((jnp.exp2(s - m).astype(BF16), m))
            probs.append(block_probs)
        pv = [[jnp.dot(block_probs[h][0], jnp.concatenate([v[:, lanes[h]], ones_cols[h]], axis=1),
                       preferred_element_type=F32) for h in heads]
              for (_, _, v), block_probs in zip(blocks, probs)]
        for (_, out_rows, _), block_probs, block_pv in zip(blocks, probs, pv):
            m_all = jnp.broadcast_to(block_probs[-1][1], (qb, ATTN_HEAD_DIM))
            l_all = block_pv[-1][:, ATTN_HEAD_DIM:]
            for h in reversed(heads[:-1]):
                m_all = jnp.where(out_group == h, block_probs[h][1], m_all)
                l_all = l_all + block_pv[h][:, ATTN_HEAD_DIM:]
            for h in heads:
                oacc_ref[h, out_rows, :] = block_pv[h][:, :ATTN_HEAD_DIM]
            macc_ref[out_rows, :] = m_all
            lacc_ref[out_rows, :] = l_all

    def body(it2, carry):
        half_body(2 * it2, sa_ref, sb_ref)
        half_body(2 * it2 + 1, sb_ref, sa_ref)
        return carry

    lax.fori_loop(0, n_iter // 2, body, 0, unroll=True)
    for h in heads:
        o_ref[0, :, lanes[h]] = oacc_ref[h].astype(o_ref.dtype)
    m_ref[0] = macc_ref[...]
    l_ref[0] = lacc_ref[...]


def _attention_group(q_arr, k_arr, v_arr, cols, batch, seq, dilation, slopes):
    L = seq // dilation
    gw = ATTN_GROUP_WIDTH
    tc = min(seq, ATTN_TOKENS_PER_STEP)
    lq = tc // dilation
    kern = functools.partial(_attn_kernel, seq_len=L, dilation=dilation, slopes=slopes)
    qc, kc, vc = cols
    stat = pl.BlockSpec((1, tc, ATTN_HEAD_DIM), lambda b, c: (b, c, 0))
    stat_shape = jax.ShapeDtypeStruct((batch, seq, ATTN_HEAD_DIM), F32)
    o, m, l = pl.pallas_call(
        kern,
        grid=(batch, seq // tc),
        in_specs=[pl.BlockSpec((1, dilation, lq, gw), lambda b, c: (b, 0, c, qc)),
                  pl.BlockSpec((1, dilation, L, gw), lambda b, c: (b, 0, 0, kc)),
                  pl.BlockSpec((1, dilation, L, gw), lambda b, c: (b, 0, 0, vc))],
        out_specs=[pl.BlockSpec((1, tc, gw), lambda b, c: (b, c, 0)), stat, stat],
        out_shape=[jax.ShapeDtypeStruct((batch, seq, gw), BF16), stat_shape, stat_shape],
        scratch_shapes=[pltpu.VMEM((ATTN_HEADS_PER_GROUP, tc, ATTN_HEAD_DIM), F32),
                        pltpu.VMEM((tc, ATTN_HEAD_DIM), F32), pltpu.VMEM((tc, ATTN_HEAD_DIM), F32),
                        pltpu.VMEM((ATTN_HEADS_PER_GROUP, ATTN_WINDOW_OFFSETS, ATTN_QUERY_BLOCK, ATTN_KEY_WINDOW), F32),
                        pltpu.VMEM((ATTN_BLOCKS_PER_ITER * ATTN_HEADS_PER_GROUP, ATTN_QUERY_BLOCK, ATTN_KEY_WINDOW), F32),
                        pltpu.VMEM((ATTN_BLOCKS_PER_ITER * ATTN_HEADS_PER_GROUP, ATTN_QUERY_BLOCK, ATTN_KEY_WINDOW), F32)],
        compiler_params=_params("arbitrary", "arbitrary"),
        name="attn_dil%d" % dilation,
    )(q_arr, k_arr, v_arr)
    T = batch * seq
    return o.reshape(T, gw), m.reshape(T, ATTN_HEAD_DIM), l.reshape(T, ATTN_HEAD_DIM)


def _running_sum_rows(x, row_index, reverse):
    n = x.shape[0]
    shift = 1
    while shift < n:
        if reverse:
            x = x + jnp.where(row_index < n - shift, pltpu.roll(x, n - shift, 0), 0.0)
        else:
            x = x + jnp.where(row_index >= shift, pltpu.roll(x, shift, 0), 0.0)
        shift *= 2
    return x


def _hgrn_kernel(qf_ref, vf_ref, lf_ref, kf_ref, qb_ref, vb_ref, lb_ref, kb_ref, g_ref, nw_ref, o_ref,
                 accf_ref, accb_ref, sf_ref, sb_ref):
    c = pl.program_id(2)
    nc = pl.num_programs(2)
    chunk = qf_ref.shape[1]
    seq = o_ref.shape[1]
    hp = HGRN_HEADS_PER_STEP
    blk = HGRN_BLOCK
    half = blk // 2
    nblk = chunk // blk
    dh = HGRN_HEAD_DIM
    pw = 2 * dh

    @pl.when(c == 0)
    def _():
        sf_ref[...] = jnp.zeros_like(sf_ref)
        sb_ref[...] = jnp.zeros_like(sb_ref)

    row_index = lax.broadcasted_iota(jnp.int32, (blk, pw), 0)
    ti = lax.broadcasted_iota(jnp.int32, (blk, 2 * blk), 0)
    si = lax.broadcasted_iota(jnp.int32, (blk, 2 * blk), 1) % blk
    causal = si <= ti
    anticausal = si >= ti
    zeros_kv = jnp.zeros((blk, dh), BF16)
    zeros_st = jnp.zeros((dh, dh), BF16)

    def block_diag(a, b, z):
        return jnp.concatenate([jnp.concatenate([a, z], axis=1), jnp.concatenate([z, b], axis=1)], axis=0)

    def load_unit(pair, r0, q_ref, v_ref, logf_ref, key_ref, s_ref):
        lanes = slice(pair * pw, (pair + 1) * pw)
        rows = pl.ds(r0, blk)
        return (q_ref[0, rows, lanes], v_ref[0, rows, lanes], logf_ref[0, rows, lanes], key_ref[0, rows, lanes],
                s_ref[2 * pair], s_ref[2 * pair + 1])

    def prepare_unit(loaded, backward):
        q, v, logf, k, st0, st1 = loaded
        a = _running_sum_rows(logf, row_index, backward)
        if backward:
            total = a[0:1, :]
            mid = a[half:half + 1, :]
        else:
            total = a[blk - 1:blk, :]
            mid = a[half - 1:half, :]
        q_mid = q * jnp.exp(a - mid).astype(BF16)
        k_mid_b = k * jnp.exp(mid - a).astype(BF16)
        q_in = q_mid * jnp.exp(mid).astype(BF16)
        k_out = k_mid_b * jnp.exp(total - mid).astype(BF16)
        decay = jnp.broadcast_to(jnp.exp(total), (SUBLANES, pw)).T[:, 0:1]
        return dict(q_mid=q_mid, k_diag=block_diag(k_mid_b[:, :dh], k_mid_b[:, dh:], zeros_kv),
                    q_in=q_in, s_diag=block_diag(st0.astype(BF16), st1.astype(BF16), zeros_st),
                    v=v, v_diag=block_diag(v[:, :dh], v[:, dh:], zeros_kv), k_out=k_out,
                    decay=decay, states=(st0, st1), mask=anticausal if backward else causal)

    nt = (((1,), (1,)), ((), ()))
    tn = (((0,), (0,)), ((), ()))

    def body(n, carry):
        rf = pl.multiple_of(n * blk, blk)
        rb = pl.multiple_of((nblk - 1 - n) * blk, blk)
        units = []
        for pair in range(hp // 2):
            units.append((pair, False, load_unit(pair, rf, qf_ref, vf_ref, lf_ref, kf_ref, sf_ref)))
            units.append((pair, True, load_unit(pair, rb, qb_ref, vb_ref, lb_ref, kb_ref, sb_ref)))
        prepared = [prepare_unit(loaded, backward) for _, backward, loaded in units]
        att = [lax.dot_general(u["q_mid"], u["k_diag"], nt, preferred_element_type=F32) for u in prepared]
        inter = [jnp.dot(u["q_in"], u["s_diag"], preferred_element_type=F32) for u in prepared]
        upd = [[lax.dot_general(u["k_out"][:, j * dh:(j + 1) * dh], u["v"][:, j * dh:(j + 1) * dh], tn,
                                preferred_element_type=F32) for j in range(2)] for u in prepared]
        att = [jnp.where(u["mask"], s, 0.0).astype(BF16) for u, s in zip(prepared, att)]
        intra = [jnp.dot(s, u["v_diag"], preferred_element_type=F32) for u, s in zip(prepared, att)]
        for i, (pair, backward, _) in enumerate(units):
            u = prepared[i]
            lanes = slice(pair * pw, (pair + 1) * pw)
            o = intra[i] + inter[i]
            states = [st * u["decay"][j * dh:(j + 1) * dh, :] + upd[i][j] for j, st in enumerate(u["states"])]
            if backward:
                accb_ref[pl.ds(pl.multiple_of((nc - 1 - c) * chunk + rb, blk), blk), lanes] = o
                sb_ref[2 * pair], sb_ref[2 * pair + 1] = states
            else:
                accf_ref[pl.ds(pl.multiple_of(c * chunk + rf, blk), blk), lanes] = o
                sf_ref[2 * pair], sf_ref[2 * pair + 1] = states
        return carry

    lax.fori_loop(0, nblk, body, 0, unroll=True)

    @pl.when(c == nc - 1)
    def _():
        fin = min(seq, 256)

        def finish(t, carry):
            rows = pl.ds(pl.multiple_of(t * fin, fin), fin)
            for h in range(hp):
                lanes = slice(h * dh, (h + 1) * dh)
                o = accf_ref[rows, lanes] + accb_ref[rows, lanes]
                o = o * lax.rsqrt(jnp.mean(o * o, axis=-1, keepdims=True) + RMS_EPS)
                o = o * nw_ref[:, lanes] * g_ref[0, rows, lanes].astype(F32)
                o_ref[0, rows, lanes] = o.astype(o_ref.dtype)
            return carry

        lax.fori_loop(0, seq // fin, finish, 0, unroll=4)


def _hgrn(qg3, plain3, logf3, key3, norm_w):
    batch, seq, _ = qg3.shape
    hp = HGRN_HEADS_PER_STEP
    w = hp * HGRN_HEAD_DIM
    chunk = min(seq, HGRN_CHUNK)
    nc = seq // chunk

    def fwd(offset):
        base = offset // w
        return pl.BlockSpec((1, chunk, w), lambda b, h, c: (b, c, base + h))

    def bwd(offset):
        base = offset // w
        return pl.BlockSpec((1, chunk, w), lambda b, h, c: (b, nc - 1 - c, base + h))

    gbase = HGRN_WIDTH // w
    return pl.pallas_call(
        _hgrn_kernel,
        grid=(batch, HGRN_HEADS // hp, nc),
        in_specs=[fwd(0), fwd(PLAIN_HI), fwd(0), fwd(0),
                  bwd(0), bwd(PLAIN_HI), bwd(HGRN_WIDTH), bwd(HGRN_WIDTH),
                  pl.BlockSpec((1, seq, w), lambda b, h, c: (b, 0, gbase + h)),
                  pl.BlockSpec((1, w), lambda b, h, c: (0, h))],
        out_specs=pl.BlockSpec((1, seq, w), lambda b, h, c: (b, 0, h)),
        out_shape=jax.ShapeDtypeStruct((batch, seq, HGRN_WIDTH), BF16),
        scratch_shapes=[pltpu.VMEM((seq, w), F32), pltpu.VMEM((seq, w), F32),
                        pltpu.VMEM((hp, HGRN_HEAD_DIM, HGRN_HEAD_DIM), F32),
                        pltpu.VMEM((hp, HGRN_HEAD_DIM, HGRN_HEAD_DIM), F32)],
        compiler_params=_params("parallel", "parallel", "arbitrary"),
        name="hgrn",
    )(qg3, plain3, logf3, key3, qg3, plain3, logf3, key3, qg3, norm_w)


def _pool_kernel(prev_ref, cur_ref, next_ref, pw_ref, ps_ref, o_ref, *, seq_len):
    i = pl.program_id(1)
    last = pl.num_programs(1) - 1
    ts = cur_ref.shape[1]
    n = ts + 2 * POOL_HALO
    pos = i * ts + lax.broadcasted_iota(jnp.int32, (ts, 1), 0)
    for g in range(POOL_GROUPS):
        lanes = slice(g * POOL_GROUP_DIM, (g + 1) * POOL_GROUP_DIM)
        half = POOL_WINDOWS[g] // 2
        cur = cur_ref[0, :, lanes].astype(F32)
        prev = jnp.where(i > 0, prev_ref[0, :, lanes].astype(F32), 0.0)
        nxt = jnp.where(i < last, next_ref[0, :, lanes].astype(F32), 0.0)
        ext = jnp.concatenate([prev, cur, nxt], axis=0)
        w = ext + pltpu.roll(ext, 1, 0)
        step = 1
        while step < half:
            w = pltpu.roll(w, step, 0) + pltpu.roll(w, n - step, 0)
            step *= 2
        wsum = w[POOL_HALO:POOL_HALO + ts]
        count = (jnp.minimum(pos + half, seq_len) - jnp.maximum(pos - half, 0)).astype(F32)
        mixed = wsum / count - cur
        y = jnp.dot(mixed.astype(BF16), pw_ref[g], preferred_element_type=F32)
        o_ref[0, :, lanes] = (y * ps_ref[:, lanes]).astype(o_ref.dtype)


def _pool(plain3, pool_w, pool_scale):
    batch, seq, _ = plain3.shape
    ts = min(seq, 512)
    hb = ts // POOL_HALO
    nhalo = seq // POOL_HALO
    kern = functools.partial(_pool_kernel, seq_len=seq)
    return pl.pallas_call(
        kern,
        grid=(batch, seq // ts),
        in_specs=[pl.BlockSpec((1, POOL_HALO, D_MODEL), lambda b, i: (b, jnp.maximum(i * hb - 1, 0), 0)),
                  pl.BlockSpec((1, ts, D_MODEL), lambda b, i: (b, i, 0)),
                  pl.BlockSpec((1, POOL_HALO, D_MODEL), lambda b, i: (b, jnp.minimum((i + 1) * hb, nhalo - 1), 0)),
                  pl.BlockSpec((POOL_GROUPS, POOL_GROUP_DIM, POOL_GROUP_DIM), lambda b, i: (0, 0, 0)),
                  pl.BlockSpec((1, D_MODEL), lambda b, i: (0, 0))],
        out_specs=pl.BlockSpec((1, ts, D_MODEL), lambda b, i: (b, i, 0)),
        out_shape=jax.ShapeDtypeStruct((batch, seq, D_MODEL), BF16),
        compiler_params=_params("parallel", "parallel"),
        name="pool",
    )(plain3, plain3, plain3, pool_w, pool_scale)


def _merge_kernel(ya_ref, o1_ref, o2_ref, o3_ref, m1_ref, m2_ref, m3_ref, l1_ref, l2_ref, l3_ref, hc_ref,
                  xin_ref, x_ref, wa_ref, wb_ref, wc_ref, wo_ref, wgate_ref, g_ref, b_ref, xo_ref, xb_ref,
                  *, alpha, nsub):
    d = D_MODEL
    dh = ATTN_HEAD_DIM
    sub = x_ref.shape[0] // nsub
    rows = [slice(s * sub, (s + 1) * sub) for s in range(nsub)]
    xin = [xin_ref[r, :].astype(BF16) for r in rows]
    y_a = [jnp.dot(ya_ref[r, :], wa_ref[...], preferred_element_type=F32) for r in rows]
    y_c = [jnp.dot(hc_ref[r, :], wc_ref[...], preferred_element_type=F32) for r in rows]
    gates = [[jnp.dot(x, wgate_ref[:, n * d:(n + 1) * d], preferred_element_type=F32) for n in range(3)]
             for x in xin]
    att = []
    for r in rows:
        ms = [m1_ref[r, :], m2_ref[r, :], m3_ref[r, :]]
        ls = [l1_ref[r, :], l2_ref[r, :], l3_ref[r, :]]
        top = jnp.maximum(jnp.maximum(ms[0], ms[1]), ms[2])
        es = [jnp.exp2(m - top) for m in ms]
        inv = 1.0 / (es[0] * ls[0] + es[1] * ls[1] + es[2] * ls[2])
        ws = [e * inv for e in es]
        heads = []
        for h in range(ATTN_HEADS_PER_GROUP):
            lanes = slice(h * dh, (h + 1) * dh)
            stat = slice(h * ATTN_STAT_LANES, h * ATTN_STAT_LANES + 1)
            acc = None
            for w, o_ref in zip(ws, (o1_ref, o2_ref, o3_ref)):
                term = jnp.broadcast_to(w[:, stat], (sub, dh)) * o_ref[r, lanes].astype(F32)
                acc = term if acc is None else acc + term
            heads.append(acc)
        att.append(jnp.concatenate(heads, axis=1).astype(BF16))
    y_b = [jnp.dot(a, wb_ref[...], preferred_element_type=F32) for a in att]
    merged = [(_sigmoid(gates[s][0]) * y_a[s] + _sigmoid(gates[s][1]) * y_b[s]
               + _sigmoid(gates[s][2]) * y_c[s]).astype(BF16) for s in range(nsub)]
    mix = [jnp.dot(mg, wo_ref[...], preferred_element_type=F32) for mg in merged]
    for s, r in enumerate(rows):
        x1 = _layer_norm(alpha * x_ref[r, :] + mix[s], g_ref[...], b_ref[...])
        xo_ref[r, :] = x1
        xb_ref[r, :] = x1.astype(BF16)


def _merge(y_pool, o_list, m_list, l_list, y_hgrn, x_bf, x, wa, wb, wc, wo, wgate, ln_g, ln_b, alpha):
    T = x.shape[0]
    tm = min(T, MERGE_ROWS)
    gw = ATTN_GROUP_WIDTH
    d = D_MODEL
    row = lambda w: pl.BlockSpec((tm, w), lambda i: (i, 0))
    const = lambda a: pl.BlockSpec(a.shape, lambda i: (0,) * a.ndim, pipeline_mode=pl.Buffered(1))
    kern = functools.partial(_merge_kernel, alpha=alpha, nsub=MERGE_SUBTILES)
    stat = row(ATTN_HEAD_DIM)
    return pl.pallas_call(
        kern,
        grid=(T // tm,),
        in_specs=[row(d), row(gw), row(gw), row(gw), stat, stat, stat, stat, stat, stat, row(d), row(d),
                  row(d), const(wa), const(wb), const(wc), const(wo), const(wgate), const(ln_g), const(ln_b)],
        out_specs=[row(d), row(d)],
        out_shape=[jax.ShapeDtypeStruct((T, d), F32), jax.ShapeDtypeStruct((T, d), BF16)],
        compiler_params=_params("parallel"),
        name="merge",
    )(y_pool, *o_list, *m_list, *l_list, y_hgrn, x_bf, x, wa, wb, wc, wo, wgate, ln_g, ln_b)


def _ffn_kernel(xb_ref, x_ref, p_ref, wg_ref, wu_ref, wd_ref, wpp_ref, wpg_ref, g_ref, b_ref,
                xo_ref, xbo_ref, *, alpha, nsub):
    sub = xb_ref.shape[0] // nsub
    rows = [slice(s * sub, (s + 1) * sub) for s in range(nsub)]
    xb = [xb_ref[r, :] for r in rows]
    gate = [jnp.dot(x, wg_ref[...], preferred_element_type=F32) for x in xb]
    up = [jnp.dot(x, wu_ref[...], preferred_element_type=F32) for x in xb]
    pgate = [jnp.dot(x, wpg_ref[...], preferred_element_type=F32) for x in xb]
    ple = [jnp.dot(p_ref[r, :].astype(BF16), wpp_ref[...], preferred_element_type=F32) for r in rows]
    hidden = [(g * _sigmoid(g) * u).astype(BF16) for g, u in zip(gate, up)]
    ffn = [jnp.dot(h, wd_ref[...], preferred_element_type=F32) for h in hidden]
    for s, r in enumerate(rows):
        x2 = _layer_norm(alpha * x_ref[r, :] + ffn[s] + ple[s] * _sigmoid(pgate[s]), g_ref[...], b_ref[...])
        xo_ref[r, :] = x2
        xbo_ref[r, :] = x2.astype(BF16)


def _ffn(x1_bf, x1, p, wg, wu, wd, wpp, wpg, ln_g, ln_b, alpha):
    T = x1.shape[0]
    tm = min(T, FFN_ROWS)
    d = D_MODEL
    row = lambda w: pl.BlockSpec((tm, w), lambda i: (i, 0))
    const = lambda a: pl.BlockSpec(a.shape, lambda i: (0,) * a.ndim, pipeline_mode=pl.Buffered(1))
    kern = functools.partial(_ffn_kernel, alpha=alpha, nsub=FFN_SUBTILES)
    return pl.pallas_call(
        kern,
        grid=(T // tm,),
        in_specs=[row(d), row(d), row(PLE_DIM), const(wg), const(wu), const(wd), const(wpp),
                  const(wpg), const(ln_g), const(ln_b)],
        out_specs=[row(d), row(d)],
        out_shape=[jax.ShapeDtypeStruct((T, d), F32), jax.ShapeDtypeStruct((T, d), BF16)],
        compiler_params=_params("parallel"),
        name="ffn",
    )(x1_bf, x1, p, wg, wu, wd, wpp, wpg, ln_g, ln_b)


def _split_in_weights(w_in):
    d, aw, gw, hw = D_MODEL, ATTN_HEADS * ATTN_HEAD_DIM, ATTN_GROUP_WIDTH, HGRN_WIDTH
    pool = w_in[:, :, 0:d]
    aq = w_in[:, :, d:d + aw]
    ak = w_in[:, :, d + aw:d + 2 * aw]
    av = w_in[:, :, d + 2 * aw:d + 3 * aw]
    h0 = d + 3 * aw
    hq, hi, hff, hfb, hg = [w_in[:, :, h0 + n * hw:h0 + (n + 1) * hw] for n in range(5)]
    gates = w_in[:, :, h0 + 5 * hw:]
    qkv = lambda g: [t[:, :, g * gw:(g + 1) * gw] for t in (aq, ak, av)]
    cat = lambda parts: jnp.concatenate(parts, axis=-1).astype(BF16)
    return (cat([pool, hi] + qkv(0)), cat([hq, hg]), cat([hff, hfb]), gates.astype(BF16),
            cat(qkv(1)), cat(qkv(2)))


def kernel(x, p, w_in, pool_w, pool_scale, w_branch_a, w_branch_b, w_branch_c, hgrn_lb_logits, hgrn_norm_w, w_out, ln1_g, ln1_b, w_ffn_gate, w_ffn_up, w_ffn_down, w_ple_proj, w_ple_gate, ln2_g, ln2_b):
    batch, seq, d = x.shape
    depth = w_in.shape[0]
    T = batch * seq
    alpha = float((2 * depth) ** 0.25)

    lb = jnp.cumsum(jax.nn.softmax(hgrn_lb_logits.astype(F32), axis=0), axis=0)
    lb = lb - lb[:1]

    w_plain, w_silu, w_forget, w_gates, w_g2, w_g3 = _split_in_weights(w_in)
    bf = lambda a: a.astype(BF16)
    pool_w_b, wa, wb, wc, wo = bf(pool_w), bf(w_branch_a), bf(w_branch_b), bf(w_branch_c), bf(w_out)
    wg, wu, wd, wpp, wpg = bf(w_ffn_gate), bf(w_ffn_up), bf(w_ffn_down), bf(w_ple_proj), bf(w_ple_gate)

    slopes = [2.0 ** (-ALIBI_MAX_BIAS * (i + 1) / ATTN_HEADS) for i in range(ATTN_HEADS)]
    hpg = ATTN_HEADS_PER_GROUP
    gw = ATTN_GROUP_WIDTH

    xf = x.reshape(T, d).astype(F32)
    xb = xf
    for i in range(depth):
        qg, logf, key, plain = _inproj(xb, w_silu[i], w_forget[i], w_plain[i], lb[i:i + 1])
        plain3 = plain.reshape(batch, seq, PLAIN_WIDTH)
        o_list, m_list, l_list = [], [], []
        for g, (_, dil) in enumerate(ATTN_GROUPS):
            gs = tuple(slopes[g * hpg:(g + 1) * hpg])
            if dil == 1:
                qkv = plain.reshape(batch, 1, seq, PLAIN_WIDTH)
                cols = (PLAIN_AQ // gw, PLAIN_AK // gw, PLAIN_AV // gw)
            else:
                qkv = _inproj_dilated(xf, (w_g2, w_g3)[g - 1][i], batch, seq, dil)
                cols = (0, 1, 2)
            o, m, l = _attention_group(qkv, qkv, qkv, cols, batch, seq, dil, gs)
            o_list.append(o)
            m_list.append(m)
            l_list.append(l)
        y_hgrn = _hgrn(qg.reshape(batch, seq, 2 * HGRN_WIDTH), plain3,
                       logf.reshape(batch, seq, 2 * HGRN_WIDTH), key.reshape(batch, seq, 2 * HGRN_WIDTH),
                       hgrn_norm_w[i:i + 1])
        y_pool = _pool(plain3, pool_w_b[i], pool_scale[i:i + 1])
        x1, x1b = _merge(y_pool.reshape(T, d), o_list, m_list, l_list, y_hgrn.reshape(T, d), xb, xf,
                         wa[i], wb[i], wc[i], wo[i], w_gates[i], ln1_g[i:i + 1], ln1_b[i:i + 1], alpha)
        xf, xb = _ffn(x1b, x1, p[i].reshape(T, PLE_DIM), wg[i], wu[i], wd[i], wpp[i], wpg[i],
                      ln2_g[i:i + 1], ln2_b[i:i + 1], alpha)
    return xf.reshape(batch, seq, d).astype(x.dtype)
```

```python
import functools

import jax
import jax.numpy as jnp
from jax import lax
from jax.experimental import pallas as pl
from jax.experimental.pallas import tpu as pltpu

F32 = jnp.float32
BF16 = jnp.bfloat16
LANES = 128
SUBLANES = 8
LOG2_E = 1.4426950408889634
LN_2 = 0.6931471805599453

D_MODEL = 1024
PLE_DIM = 256
POOL_WINDOWS = (2, 4, 8, 16)
POOL_GROUPS = 4
POOL_GROUP_DIM = D_MODEL // POOL_GROUPS
ATTN_GROUPS = ((128, 1), (512, 4), (2048, 16))
ATTN_HEADS_PER_GROUP = 4
ATTN_HEADS = ATTN_HEADS_PER_GROUP * len(ATTN_GROUPS)
ATTN_HEAD_DIM = 128
ATTN_GROUP_WIDTH = ATTN_HEADS_PER_GROUP * ATTN_HEAD_DIM
ATTN_SIDE = 64
ALIBI_MAX_BIAS = 8.0
NEG_INF = -1e30
HGRN_HEADS = 8
HGRN_HEAD_DIM = 128
HGRN_WIDTH = HGRN_HEADS * HGRN_HEAD_DIM
FFN_HIDDEN = 2816
LN_EPS = 1e-5
RMS_EPS = 1e-6

PLAIN_POOL = 0
PLAIN_HI = 1024
PLAIN_AQ = 2048
PLAIN_AK = 2560
PLAIN_AV = 3072
PLAIN_WIDTH = 3584
QKV_WIDTH = 3 * ATTN_GROUP_WIDTH

INPROJ_COLUMN_STEPS = 2
INPROJ_ROWS = 512
ATTN_QUERY_BLOCK = 128
ATTN_KEY_WINDOW = ATTN_QUERY_BLOCK + 2 * ATTN_SIDE
ATTN_TOKENS_PER_STEP = 2048
ATTN_BLOCKS_PER_ITER = 2
ATTN_STAT_LANES = 32
ATTN_WINDOW_OFFSETS = 3
HGRN_BLOCK = 64
HGRN_HEADS_PER_STEP = 4
HGRN_CHUNK = 512
MERGE_ROWS = 512
MERGE_SUBTILES = 2
FFN_ROWS = 512
FFN_SUBTILES = 2
POOL_HALO = 16
VMEM_LIMIT = 52 * 1024 * 1024


def _params(*semantics):
    return pltpu.CompilerParams(dimension_semantics=semantics, vmem_limit_bytes=VMEM_LIMIT)


def _sigmoid(x):
    return 0.5 * jnp.tanh(0.5 * x) + 0.5


def _layer_norm(h, g, b):
    mu = jnp.mean(h, axis=-1, keepdims=True)
    d = h - mu
    var = jnp.mean(d * d, axis=-1, keepdims=True)
    return d * lax.rsqrt(var + LN_EPS) * g + b


def _inproj_kernel(x_ref, ws_ref, wf_ref, wp_ref, lb_ref, act_ref, logf_ref, key_ref, plain_ref):
    x = x_ref[...].astype(BF16)
    zs = jnp.dot(x, ws_ref[...], preferred_element_type=F32)
    zf = jnp.dot(x, wf_ref[...], preferred_element_type=F32)
    plain_ref[...] = jnp.dot(x, wp_ref[...], preferred_element_type=F32).astype(plain_ref.dtype)
    act_ref[...] = (zs * _sigmoid(zs)).astype(act_ref.dtype)
    lb = lb_ref[...]
    sig = _sigmoid(zf)
    logf_ref[...] = jnp.log(lb + (1.0 - lb) * sig).astype(logf_ref.dtype)
    key_ref[...] = ((1.0 - lb) * (1.0 - sig)).astype(key_ref.dtype)


def _inproj(x_bf, w_silu, w_forget, w_plain, lb):
    T, D = x_bf.shape
    tm = min(T, INPROJ_ROWS)
    steps = INPROJ_COLUMN_STEPS
    ns, nf, np_ = w_silu.shape[1] // steps, w_forget.shape[1] // steps, w_plain.shape[1] // steps
    wspec = lambda n: pl.BlockSpec((D, n), lambda j, i: (0, j))
    ospec = lambda n: pl.BlockSpec((tm, n), lambda j, i: (i, j))
    oshape = lambda w, dt: jax.ShapeDtypeStruct((T, w.shape[1]), dt)
    return pl.pallas_call(
        _inproj_kernel,
        grid=(steps, T // tm),
        in_specs=[pl.BlockSpec((tm, D), lambda j, i: (i, 0)), wspec(ns), wspec(nf), wspec(np_),
                  pl.BlockSpec((1, nf), lambda j, i: (0, j))],
        out_specs=[ospec(ns), ospec(nf), ospec(nf), ospec(np_)],
        out_shape=[oshape(w_silu, BF16), oshape(w_forget, BF16), oshape(w_forget, BF16), oshape(w_plain, BF16)],
        compiler_params=_params("parallel", "parallel"),
        name="inproj",
    )(x_bf, w_silu, w_forget, w_plain, lb)


def _inproj_dilated_kernel(x_ref, w_ref, o_ref, slab_ref, xp_ref, *, dilation):
    n = x_ref.shape[0] // dilation
    for c in range(D_MODEL // LANES):
        slab_ref[c] = x_ref[:, c * LANES:(c + 1) * LANES]
    for r in range(dilation):
        for c in range(D_MODEL // LANES):
            xp_ref[r * n:(r + 1) * n, c * LANES:(c + 1) * LANES] = (
                slab_ref[c, pl.ds(r, n, stride=dilation), :].astype(BF16))
    res = jnp.dot(xp_ref[...], w_ref[...], preferred_element_type=F32).astype(o_ref.dtype)
    for r in range(dilation):
        o_ref[0, r] = res[r * n:(r + 1) * n]


def _inproj_dilated(x_f32, w, batch, seq, dilation):
    N = w.shape[1]
    tb = min(seq, 1024)
    steps = seq // tb
    n = tb // dilation
    return pl.pallas_call(
        functools.partial(_inproj_dilated_kernel, dilation=dilation),
        grid=(batch * steps,),
        in_specs=[pl.BlockSpec((tb, D_MODEL), lambda i: (i, 0)),
                  pl.BlockSpec((D_MODEL, N), lambda i: (0, 0))],
        out_specs=pl.BlockSpec((1, dilation, n, N), lambda i: (i // steps, 0, i % steps, 0)),
        out_shape=jax.ShapeDtypeStruct((batch, dilation, seq // dilation, N), BF16),
        scratch_shapes=[pltpu.VMEM((D_MODEL // LANES, tb, LANES), F32), pltpu.VMEM((tb, D_MODEL), BF16)],
        compiler_params=_params("parallel"),
        name="inproj_dilated%d" % dilation,
    )(x_f32, w)


def _attn_kernel(q_ref, k_ref, v_ref, o_ref, m_ref, l_ref, oacc_ref, macc_ref, lacc_ref, bias_ref, sa_ref, sb_ref, *,
                 seq_len, dilation, slopes):
    c = pl.program_id(1)
    lq = q_ref.shape[2]
    qb, kw = ATTN_QUERY_BLOCK, ATTN_KEY_WINDOW
    nq = lq // qb
    per_iter = ATTN_BLOCKS_PER_ITER
    scale2 = ATTN_HEAD_DIM ** -0.5 * LOG2_E
    heads = range(ATTN_HEADS_PER_GROUP)
    lanes = [slice(h * ATTN_HEAD_DIM, (h + 1) * ATTN_HEAD_DIM) for h in heads]

    @pl.when((pl.program_id(0) == 0) & (c == 0))
    def _():
        row = lax.broadcasted_iota(jnp.int32, (qb, kw), 0)
        col = lax.broadcasted_iota(jnp.int32, (qb, kw), 1)
        for w in range(ATTN_WINDOW_OFFSETS):
            dist = jnp.abs(col - row - w * ATTN_SIDE)
            distf = dist.astype(F32) * float(dilation)
            for h in heads:
                bias_ref[h, w] = jnp.where(dist <= ATTN_SIDE, distf * (-slopes[h] * LOG2_E), NEG_INF * LOG2_E)

    stat_group = lax.broadcasted_iota(jnp.int32, (kw, ATTN_HEAD_DIM), 1) // ATTN_STAT_LANES
    ones_cols = [jnp.where(stat_group == h, 1.0, 0.0).astype(BF16) for h in heads]
    out_group = lax.broadcasted_iota(jnp.int32, (qb, ATTN_HEAD_DIM), 1) // ATTN_STAT_LANES

    def block_coords(idx):
        r = idx // nq
        r0 = pl.multiple_of((idx % nq) * qb, qb)
        t0 = c * lq + r0
        ks = pl.multiple_of(jnp.clip(t0 - ATTN_SIDE, 0, seq_len - kw), ATTN_SIDE)
        return r, r0, t0, ks

    def score_products(it, s_ref):
        for u in range(per_iter):
            r, r0, _, ks = block_coords(it * per_iter + u)
            q = q_ref[0, r, pl.ds(r0, qb), :]
            k = k_ref[0, r, pl.ds(ks, kw), :]
            for h in heads:
                s_ref[u * len(heads) + h] = lax.dot_general(
                    q[:, lanes[h]], k[:, lanes[h]], (((1,), (1,)), ((), ())), preferred_element_type=F32)

    n_iter = dilation * nq // per_iter
    score_products(0, sa_ref)

    def half_body(it, s_ref, next_ref):
        score_products(jnp.minimum(it + 1, n_iter - 1), next_ref)
        blocks = []
        for u in range(per_iter):
            r, r0, t0, ks = block_coords(it * per_iter + u)
            if dilation == 1:
                out_rows = pl.ds(r0, qb)
            else:
                out_rows = pl.ds(r0 * dilation + r, qb, stride=dilation)
            blocks.append(((t0 - ks) // ATTN_SIDE, out_rows, v_ref[0, r, pl.ds(ks, kw), :]))
        probs = []
        for u, (w, _, _) in enumerate(blocks):
            block_probs = []
            for h in heads:
                s = s_ref[u * len(heads) + h] * scale2 + bias_ref[h, w]
                m = jnp.max(s, axis=1, keepdims=True)
                block_probs.append((jnp.exp2(s - m).astype(BF16), m))
            probs.append(block_probs)
        pv = [[jnp.dot(block_probs[h][0], jnp.concatenate([v[:, lanes[h]], ones_cols[h]], axis=1),
                       preferred_element_type=F32) for h in heads]
              for (_, _, v), block_probs in zip(blocks, probs)]
        for (_, out_rows, _), block_probs, block_pv in zip(blocks, probs, pv):
            m_all = jnp.broadcast_to(block_probs[-1][1], (qb, ATTN_HEAD_DIM))
            l_all = block_pv[-1][:, ATTN_HEAD_DIM:]
            for h in reversed(heads[:-1]):
                m_all = jnp.where(out_group == h, block_probs[h][1], m_all)
                l_all = l_all + block_pv[h][:, ATTN_HEAD_DIM:]
            for h in heads:
                oacc_ref[h, out_rows, :] = block_pv[h][:, :ATTN_HEAD_DIM]
            macc_ref[out_rows, :] = m_all
            lacc_ref[out_rows, :] = l_all

    def body(it2, carry):
        half_body(2 * it2, sa_ref, sb_ref)
        half_body(2 * it2 + 1, sb_ref, sa_ref)
        return carry

    lax.fori_loop(0, n_iter // 2, body, 0, unroll=True)
    for h in heads:
        o_ref[0, :, lanes[h]] = oacc_ref[h].astype(o_ref.dtype)
    m_ref[0] = macc_ref[...]
    l_ref[0] = lacc_ref[...]


def _attention_group(q_arr, k_arr, v_arr, cols, batch, seq, dilation, slopes):
    L = seq // dilation
    gw = ATTN_GROUP_WIDTH
    tc = min(seq, ATTN_TOKENS_PER_STEP)
    lq = tc // dilation
    kern = functools.partial(_attn_kernel, seq_len=L, dilation=dilation, slopes=slopes)
    qc, kc, vc = cols
    stat = pl.BlockSpec((1, tc, ATTN_HEAD_DIM), lambda b, c: (b, c, 0))
    stat_shape = jax.ShapeDtypeStruct((batch, seq, ATTN_HEAD_DIM), F32)
    o, m, l = pl.pallas_call(
        kern,
        grid=(batch, seq // tc),
        in_specs=[pl.BlockSpec((1, dilation, lq, gw), lambda b, c: (b, 0, c, qc)),
                  pl.BlockSpec((1, dilation, L, gw), lambda b, c: (b, 0, 0, kc)),
                  pl.BlockSpec((1, dilation, L, gw), lambda b, c: (b, 0, 0, vc))],
        out_specs=[pl.BlockSpec((1, tc, gw), lambda b, c: (b, c, 0)), stat, stat],
        out_shape=[jax.ShapeDtypeStruct((batch, seq, gw), BF16), stat_shape, stat_shape],
        scratch_shapes=[pltpu.VMEM((ATTN_HEADS_PER_GROUP, tc, ATTN_HEAD_DIM), F32),
                        pltpu.VMEM((tc, ATTN_HEAD_DIM), F32), pltpu.VMEM((tc, ATTN_HEAD_DIM), F32),
                        pltpu.VMEM((ATTN_HEADS_PER_GROUP, ATTN_WINDOW_OFFSETS, ATTN_QUERY_BLOCK, ATTN_KEY_WINDOW), F32),
                        pltpu.VMEM((ATTN_BLOCKS_PER_ITER * ATTN_HEADS_PER_GROUP, ATTN_QUERY_BLOCK, ATTN_KEY_WINDOW), F32),
                        pltpu.VMEM((ATTN_BLOCKS_PER_ITER * ATTN_HEADS_PER_GROUP, ATTN_QUERY_BLOCK, ATTN_KEY_WINDOW), F32)],
        compiler_params=_params("arbitrary", "arbitrary"),
        name="attn_dil%d" % dilation,
    )(q_arr, k_arr, v_arr)
    T = batch * seq
    return o.reshape(T, gw), m.reshape(T, ATTN_HEAD_DIM), l.reshape(T, ATTN_HEAD_DIM)


def _running_sum_rows(x, row_index, reverse):
    n = x.shape[0]
    shift = 1
    while shift < n:
        if reverse:
            x = x + jnp.where(row_index < n - shift, pltpu.roll(x, n - shift, 0), 0.0)
        else:
            x = x + jnp.where(row_index >= shift, pltpu.roll(x, shift, 0), 0.0)
        shift *= 2
    return x


def _hgrn_kernel(qf_ref, vf_ref, lf_ref, kf_ref, qb_ref, vb_ref, lb_ref, kb_ref, g_ref, nw_ref, o_ref,
                 accf_ref, accb_ref, sf_ref, sb_ref):
    c = pl.program_id(2)
    nc = pl.num_programs(2)
    chunk = qf_ref.shape[1]
    seq = o_ref.shape[1]
    hp = HGRN_HEADS_PER_STEP
    blk = HGRN_BLOCK
    half = blk // 2
    nblk = chunk // blk
    dh = HGRN_HEAD_DIM
    pw = 2 * dh

    @pl.when(c == 0)
    def _():
        sf_ref[...] = jnp.zeros_like(sf_ref)
        sb_ref[...] = jnp.zeros_like(sb_ref)

    row_index = lax.broadcasted_iota(jnp.int32, (blk, pw), 0)
    ti = lax.broadcasted_iota(jnp.int32, (blk, 2 * blk), 0)
    si = lax.broadcasted_iota(jnp.int32, (blk, 2 * blk), 1) % blk
    causal = si <= ti
    anticausal = si >= ti
    zeros_kv = jnp.zeros((blk, dh), BF16)
    zeros_st = jnp.zeros((dh, dh), BF16)

    def block_diag(a, b, z):
        return jnp.concatenate([jnp.concatenate([a, z], axis=1), jnp.concatenate([z, b], axis=1)], axis=0)

    def load_unit(pair, r0, q_ref, v_ref, logf_ref, key_ref, s_ref):
        lanes = slice(pair * pw, (pair + 1) * pw)
        rows = pl.ds(r0, blk)
        return (q_ref[0, rows, lanes], v_ref[0, rows, lanes], logf_ref[0, rows, lanes], key_ref[0, rows, lanes],
                s_ref[2 * pair], s_ref[2 * pair + 1])

    def prepare_unit(loaded, backward):
        q, v, logf, k, st0, st1 = loaded
        a = _running_sum_rows(logf.astype(F32), row_index, backward)
        if backward:
            total = a[0:1, :]
            mid = a[half:half + 1, :]
        else:
            total = a[blk - 1:blk, :]
            mid = a[half - 1:half, :]
        q_mid = q * jnp.exp(a - mid).astype(BF16)
        k_mid_b = k * jnp.exp(mid - a).astype(BF16)
        q_in = q_mid * jnp.exp(mid).astype(BF16)
        k_out = k_mid_b * jnp.exp(total - mid).astype(BF16)
        decay = jnp.broadcast_to(jnp.exp(total), (SUBLANES, pw)).T[:, 0:1]
        return dict(q_mid=q_mid, k_diag=block_diag(k_mid_b[:, :dh], k_mid_b[:, dh:], zeros_kv),
                    q_in=q_in, s_diag=block_diag(st0.astype(BF16), st1.astype(BF16), zeros_st),
                    v=v, v_diag=block_diag(v[:, :dh], v[:, dh:], zeros_kv), k_out=k_out,
                    decay=decay, states=(st0, st1), mask=anticausal if backward else causal)

    nt = (((1,), (1,)), ((), ()))
    tn = (((0,), (0,)), ((), ()))

    def body(n, carry):
        rf = pl.multiple_of(n * blk, blk)
        rb = pl.multiple_of((nblk - 1 - n) * blk, blk)
        units = []
        for pair in range(hp // 2):
            units.append((pair, False, load_unit(pair, rf, qf_ref, vf_ref, lf_ref, kf_ref, sf_ref)))
            units.append((pair, True, load_unit(pair, rb, qb_ref, vb_ref, lb_ref, kb_ref, sb_ref)))
        prepared = [prepare_unit(loaded, backward) for _, backward, loaded in units]
        att = [lax.dot_general(u["q_mid"], u["k_diag"], nt, preferred_element_type=F32) for u in prepared]
        inter = [jnp.dot(u["q_in"], u["s_diag"], preferred_element_type=F32) for u in prepared]
        upd = [[lax.dot_general(u["k_out"][:, j * dh:(j + 1) * dh], u["v"][:, j * dh:(j + 1) * dh], tn,
                                preferred_element_type=F32) for j in range(2)] for u in prepared]
        att = [jnp.where(u["mask"], s, 0.0).astype(BF16) for u, s in zip(prepared, att)]
        intra = [jnp.dot(s, u["v_diag"], preferred_element_type=F32) for u, s in zip(prepared, att)]
        for i, (pair, backward, _) in enumerate(units):
            u = prepared[i]
            lanes = slice(pair * pw, (pair + 1) * pw)
            o = intra[i] + inter[i]
            states = [st * u["decay"][j * dh:(j + 1) * dh, :] + upd[i][j] for j, st in enumerate(u["states"])]
            if backward:
                accb_ref[pl.ds(pl.multiple_of((nc - 1 - c) * chunk + rb, blk), blk), lanes] = o
                sb_ref[2 * pair], sb_ref[2 * pair + 1] = states
            else:
                accf_ref[pl.ds(pl.multiple_of(c * chunk + rf, blk), blk), lanes] = o
                sf_ref[2 * pair], sf_ref[2 * pair + 1] = states
        return carry

    lax.fori_loop(0, nblk, body, 0, unroll=True)

    @pl.when(c == nc - 1)
    def _():
        fin = min(seq, 256)

        def finish(t, carry):
            rows = pl.ds(pl.multiple_of(t * fin, fin), fin)
            for h in range(hp):
                lanes = slice(h * dh, (h + 1) * dh)
                o = accf_ref[rows, lanes] + accb_ref[rows, lanes]
                o = o * lax.rsqrt(jnp.mean(o * o, axis=-1, keepdims=True) + RMS_EPS)
                o = o * nw_ref[:, lanes] * g_ref[0, rows, lanes].astype(F32)
                o_ref[0, rows, lanes] = o.astype(o_ref.dtype)
            return carry

        lax.fori_loop(0, seq // fin, finish, 0, unroll=4)


def _hgrn(qg3, plain3, logf3, key3, norm_w):
    batch, seq, _ = qg3.shape
    hp = HGRN_HEADS_PER_STEP
    w = hp * HGRN_HEAD_DIM
    chunk = min(seq, HGRN_CHUNK)
    nc = seq // chunk

    def fwd(offset):
        base = offset // w
        return pl.BlockSpec((1, chunk, w), lambda b, h, c: (b, c, base + h))

    def bwd(offset):
        base = offset // w
        return pl.BlockSpec((1, chunk, w), lambda b, h, c: (b, nc - 1 - c, base + h))

    gbase = HGRN_WIDTH // w
    return pl.pallas_call(
        _hgrn_kernel,
        grid=(batch, HGRN_HEADS // hp, nc),
        in_specs=[fwd(0), fwd(PLAIN_HI), fwd(0), fwd(0),
                  bwd(0), bwd(PLAIN_HI), bwd(HGRN_WIDTH), bwd(HGRN_WIDTH),
                  pl.BlockSpec((1, seq, w), lambda b, h, c: (b, 0, gbase + h)),
                  pl.BlockSpec((1, w), lambda b, h, c: (0, h))],
        out_specs=pl.BlockSpec((1, seq, w), lambda b, h, c: (b, 0, h)),
        out_shape=jax.ShapeDtypeStruct((batch, seq, HGRN_WIDTH), BF16),
        scratch_shapes=[pltpu.VMEM((seq, w), F32), pltpu.VMEM((seq, w), F32),
                        pltpu.VMEM((hp, HGRN_HEAD_DIM, HGRN_HEAD_DIM), F32),
                        pltpu.VMEM((hp, HGRN_HEAD_DIM, HGRN_HEAD_DIM), F32)],
        compiler_params=_params("parallel", "parallel", "arbitrary"),
        name="hgrn",
    )(qg3, plain3, logf3, key3, qg3, plain3, logf3, key3, qg3, norm_w)


def _pool_kernel(prev_ref, cur_ref, next_ref, pw_ref, ps_ref, o_ref, *, seq_len):
    i = pl.program_id(1)
    last = pl.num_programs(1) - 1
    ts = cur_ref.shape[1]
    n = ts + 2 * POOL_HALO
    pos = i * ts + lax.broadcasted_iota(jnp.int32, (ts, 1), 0)
    for g in range(POOL_GROUPS):
        lanes = slice(g * POOL_GROUP_DIM, (g + 1) * POOL_GROUP_DIM)
        half = POOL_WINDOWS[g] // 2
        cur = cur_ref[0, :, lanes].astype(F32)
        prev = jnp.where(i > 0, prev_ref[0, :, lanes].astype(F32), 0.0)
        nxt = jnp.where(i < last, next_ref[0, :, lanes].astype(F32), 0.0)
        ext = jnp.concatenate([prev, cur, nxt], axis=0)
        w = ext + pltpu.roll(ext, 1, 0)
        step = 1
        while step < half:
            w = pltpu.roll(w, step, 0) + pltpu.roll(w, n - step, 0)
            step *= 2
        wsum = w[POOL_HALO:POOL_HALO + ts]
        count = (jnp.minimum(pos + half, seq_len) - jnp.maximum(pos - half, 0)).astype(F32)
        mixed = wsum / count - cur
        y = jnp.dot(mixed.astype(BF16), pw_ref[g], preferred_element_type=F32)
        o_ref[0, :, lanes] = (y * ps_ref[:, lanes]).astype(o_ref.dtype)


def _pool(plain3, pool_w, pool_scale):
    batch, seq, _ = plain3.shape
    ts = min(seq, 512)
    hb = ts // POOL_HALO
    nhalo = seq // POOL_HALO
    kern = functools.partial(_pool_kernel, seq_len=seq)
    return pl.pallas_call(
        kern,
        grid=(batch, seq // ts),
        in_specs=[pl.BlockSpec((1, POOL_HALO, D_MODEL), lambda b, i: (b, jnp.maximum(i * hb - 1, 0), 0)),
                  pl.BlockSpec((1, ts, D_MODEL), lambda b, i: (b, i, 0)),
                  pl.BlockSpec((1, POOL_HALO, D_MODEL), lambda b, i: (b, jnp.minimum((i + 1) * hb, nhalo - 1), 0)),
                  pl.BlockSpec((POOL_GROUPS, POOL_GROUP_DIM, POOL_GROUP_DIM), lambda b, i: (0, 0, 0)),
                  pl.BlockSpec((1, D_MODEL), lambda b, i: (0, 0))],
        out_specs=pl.BlockSpec((1, ts, D_MODEL), lambda b, i: (b, i, 0)),
        out_shape=jax.ShapeDtypeStruct((batch, seq, D_MODEL), BF16),
        compiler_params=_params("parallel", "parallel"),
        name="pool",
    )(plain3, plain3, plain3, pool_w, pool_scale)


def _merge_kernel(ya_ref, o1_ref, o2_ref, o3_ref, m1_ref, m2_ref, m3_ref, l1_ref, l2_ref, l3_ref, hc_ref,
                  xin_ref, x_ref, wa_ref, wb_ref, wc_ref, wo_ref, wgate_ref, g_ref, b_ref, xo_ref, xb_ref,
                  *, alpha, nsub):
    d = D_MODEL
    dh = ATTN_HEAD_DIM
    sub = x_ref.shape[0] // nsub
    rows = [slice(s * sub, (s + 1) * sub) for s in range(nsub)]
    xin = [xin_ref[r, :].astype(BF16) for r in rows]
    y_a = [jnp.dot(ya_ref[r, :], wa_ref[...], preferred_element_type=F32) for r in rows]
    y_c = [jnp.dot(hc_ref[r, :], wc_ref[...], preferred_element_type=F32) for r in rows]
    gates = [[jnp.dot(x, wgate_ref[:, n * d:(n + 1) * d], preferred_element_type=F32) for n in range(3)]
             for x in xin]
    att = []
    for r in rows:
        ms = [m1_ref[r, :], m2_ref[r, :], m3_ref[r, :]]
        ls = [l1_ref[r, :], l2_ref[r, :], l3_ref[r, :]]
        top = jnp.maximum(jnp.maximum(ms[0], ms[1]), ms[2])
        es = [jnp.exp2(m - top) for m in ms]
        inv = 1.0 / (es[0] * ls[0] + es[1] * ls[1] + es[2] * ls[2])
        ws = [e * inv for e in es]
        heads = []
        for h in range(ATTN_HEADS_PER_GROUP):
            lanes = slice(h * dh, (h + 1) * dh)
            stat = slice(h * ATTN_STAT_LANES, h * ATTN_STAT_LANES + 1)
            acc = None
            for w, o_ref in zip(ws, (o1_ref, o2_ref, o3_ref)):
                term = jnp.broadcast_to(w[:, stat], (sub, dh)) * o_ref[r, lanes].astype(F32)
                acc = term if acc is None else acc + term
            heads.append(acc)
        att.append(jnp.concatenate(heads, axis=1).astype(BF16))
    y_b = [jnp.dot(a, wb_ref[...], preferred_element_type=F32) for a in att]
    merged = [(_sigmoid(gates[s][0]) * y_a[s] + _sigmoid(gates[s][1]) * y_b[s]
               + _sigmoid(gates[s][2]) * y_c[s]).astype(BF16) for s in range(nsub)]
    mix = [jnp.dot(mg, wo_ref[...], preferred_element_type=F32) for mg in merged]
    for s, r in enumerate(rows):
        x1 = _layer_norm(alpha * x_ref[r, :] + mix[s], g_ref[...], b_ref[...])
        xo_ref[r, :] = x1
        xb_ref[r, :] = x1.astype(BF16)


def _merge(y_pool, o_list, m_list, l_list, y_hgrn, x_bf, x, wa, wb, wc, wo, wgate, ln_g, ln_b, alpha):
    T = x.shape[0]
    tm = min(T, MERGE_ROWS)
    gw = ATTN_GROUP_WIDTH
    d = D_MODEL
    row = lambda w: pl.BlockSpec((tm, w), lambda i: (i, 0))
    const = lambda a: pl.BlockSpec(a.shape, lambda i: (0,) * a.ndim, pipeline_mode=pl.Buffered(1))
    kern = functools.partial(_merge_kernel, alpha=alpha, nsub=MERGE_SUBTILES)
    stat = row(ATTN_HEAD_DIM)
    return pl.pallas_call(
        kern,
        grid=(T // tm,),
        in_specs=[row(d), row(gw), row(gw), row(gw), stat, stat, stat, stat, stat, stat, row(d), row(d),
                  row(d), const(wa), const(wb), const(wc), const(wo), const(wgate), const(ln_g), const(ln_b)],
        out_specs=[row(d), row(d)],
        out_shape=[jax.ShapeDtypeStruct((T, d), F32), jax.ShapeDtypeStruct((T, d), BF16)],
        compiler_params=_params("parallel"),
        name="merge",
    )(y_pool, *o_list, *m_list, *l_list, y_hgrn, x_bf, x, wa, wb, wc, wo, wgate, ln_g, ln_b)


def _ffn_kernel(xb_ref, x_ref, p_ref, wg_ref, wu_ref, wd_ref, wpp_ref, wpg_ref, g_ref, b_ref,
                xo_ref, xbo_ref, *, alpha, nsub):
    sub = xb_ref.shape[0] // nsub
    rows = [slice(s * sub, (s + 1) * sub) for s in range(nsub)]
    xb = [xb_ref[r, :] for r in rows]
    gate = [jnp.dot(x, wg_ref[...], preferred_element_type=F32) for x in xb]
    up = [jnp.dot(x, wu_ref[...], preferred_element_type=F32) for x in xb]
    pgate = [jnp.dot(x, wpg_ref[...], preferred_element_type=F32) for x in xb]
    ple = [jnp.dot(p_ref[r, :].astype(BF16), wpp_ref[...], preferred_element_type=F32) for r in rows]
    hidden = [(g * _sigmoid(g) * u).astype(BF16) for g, u in zip(gate, up)]
    ffn = [jnp.dot(h, wd_ref[...], preferred_element_type=F32) for h in hidden]
    for s, r in enumerate(rows):
        x2 = _layer_norm(alpha * x_ref[r, :] + ffn[s] + ple[s] * _sigmoid(pgate[s]), g_ref[...], b_ref[...])
        xo_ref[r, :] = x2
        xbo_ref[r, :] = x2.astype(BF16)


def _ffn(x1_bf, x1, p, wg, wu, wd, wpp, wpg, ln_g, ln_b, alpha):
    T = x1.shape[0]
    tm = min(T, FFN_ROWS)
    d = D_MODEL
    row = lambda w: pl.BlockSpec((tm, w), lambda i: (i, 0))
    const = lambda a: pl.BlockSpec(a.shape, lambda i: (0,) * a.ndim, pipeline_mode=pl.Buffered(1))
    kern = functools.partial(_ffn_kernel, alpha=alpha, nsub=FFN_SUBTILES)
    return pl.pallas_call(
        kern,
        grid=(T // tm,),
        in_specs=[row(d), row(d), row(PLE_DIM), const(wg), const(wu), const(wd), const(wpp),
                  const(wpg), const(ln_g), const(ln_b)],
        out_specs=[row(d), row(d)],
        out_shape=[jax.ShapeDtypeStruct((T, d), F32), jax.ShapeDtypeStruct((T, d), BF16)],
        compiler_params=_params("parallel"),
        name="ffn",
    )(x1_bf, x1, p, wg, wu, wd, wpp, wpg, ln_g, ln_b)


def _split_in_weights(w_in):
    d, aw, gw, hw = D_MODEL, ATTN_HEADS * ATTN_HEAD_DIM, ATTN_GROUP_WIDTH, HGRN_WIDTH
    pool = w_in[:, :, 0:d]
    aq = w_in[:, :, d:d + aw]
    ak = w_in[:, :, d + aw:d + 2 * aw]
    av = w_in[:, :, d + 2 * aw:d + 3 * aw]
    h0 = d + 3 * aw
    hq, hi, hff, hfb, hg = [w_in[:, :, h0 + n * hw:h0 + (n + 1) * hw] for n in range(5)]
    gates = w_in[:, :, h0 + 5 * hw:]
    qkv = lambda g: [t[:, :, g * gw:(g + 1) * gw] for t in (aq, ak, av)]
    cat = lambda parts: jnp.concatenate(parts, axis=-1).astype(BF16)
    return (cat([pool, hi] + qkv(0)), cat([hq, hg]), cat([hff, hfb]), gates.astype(BF16),
            cat(qkv(1)), cat(qkv(2)))


def kernel(x, p, w_in, pool_w, pool_scale, w_branch_a, w_branch_b, w_branch_c, hgrn_lb_logits, hgrn_norm_w, w_out, ln1_g, ln1_b, w_ffn_gate, w_ffn_up, w_ffn_down, w_ple_proj, w_ple_gate, ln2_g, ln2_b):
    batch, seq, d = x.shape
    depth = w_in.shape[0]
    T = batch * seq
    alpha = float((2 * depth) ** 0.25)

    lb = jnp.cumsum(jax.nn.softmax(hgrn_lb_logits.astype(F32), axis=0), axis=0)
    lb = lb - lb[:1]

    w_plain, w_silu, w_forget, w_gates, w_g2, w_g3 = _split_in_weights(w_in)
    bf = lambda a: a.astype(BF16)
    pool_w_b, wa, wb, wc, wo = bf(pool_w), bf(w_branch_a), bf(w_branch_b), bf(w_branch_c), bf(w_out)
    wg, wu, wd, wpp, wpg = bf(w_ffn_gate), bf(w_ffn_up), bf(w_ffn_down), bf(w_ple_proj), bf(w_ple_gate)

    slopes = [2.0 ** (-ALIBI_MAX_BIAS * (i + 1) / ATTN_HEADS) for i in range(ATTN_HEADS)]
    hpg = ATTN_HEADS_PER_GROUP
    gw = ATTN_GROUP_WIDTH

    xf = x.reshape(T, d).astype(F32)
    xb = xf
    for i in range(depth):
        qg, logf, key, plain = _inproj(xb, w_silu[i], w_forget[i], w_plain[i], lb[i:i + 1])
        plain3 = plain.reshape(batch, seq, PLAIN_WIDTH)
        o_list, m_list, l_list = [], [], []
        for g, (_, dil) in enumerate(ATTN_GROUPS):
            gs = tuple(slopes[g * hpg:(g + 1) * hpg])
            if dil == 1:
                qkv = plain.reshape(batch, 1, seq, PLAIN_WIDTH)
                cols = (PLAIN_AQ // gw, PLAIN_AK // gw, PLAIN_AV // gw)
            else:
                qkv = _inproj_dilated(xf, (w_g2, w_g3)[g - 1][i], batch, seq, dil)
                cols = (0, 1, 2)
            o, m, l = _attention_group(qkv, qkv, qkv, cols, batch, seq, dil, gs)
            o_list.append(o)
            m_list.append(m)
            l_list.append(l)
        y_hgrn = _hgrn(qg.reshape(batch, seq, 2 * HGRN_WIDTH), plain3,
                       logf.reshape(batch, seq, 2 * HGRN_WIDTH), key.reshape(batch, seq, 2 * HGRN_WIDTH),
                       hgrn_norm_w[i:i + 1])
        y_pool = _pool(plain3, pool_w_b[i], pool_scale[i:i + 1])
        x1, x1b = _merge(y_pool.reshape(T, d), o_list, m_list, l_list, y_hgrn.reshape(T, d), xb, xf,
                         wa[i], wb[i], wc[i], wo[i], w_gates[i], ln1_g[i:i + 1], ln1_b[i:i + 1], alpha)
        xf, xb = _ffn(x1b, x1, p[i].reshape(T, PLE_DIM), wg[i], wu[i], wd[i], wpp[i], wpg[i],
                      ln2_g[i:i + 1], ln2_b[i:i + 1], alpha)
    return xf.reshape(batch, seq, d).astype(x.dtype)
```

```python
import functools

import jax
import jax.numpy as jnp
from jax import lax
from jax.experimental import pallas as pl
from jax.experimental.pallas import tpu as pltpu

F32 = jnp.float32
BF16 = jnp.bfloat16
LANES = 128
SUBLANES = 8
LOG2_E = 1.4426950408889634

D_MODEL = 1024
PLE_DIM = 256
POOL_WINDOWS = (2, 4, 8, 16)
POOL_GROUPS = 4
POOL_GROUP_DIM = D_MODEL // POOL_GROUPS
ATTN_GROUPS = ((128, 1), (512, 4), (2048, 16))
ATTN_HEADS_PER_GROUP = 4
ATTN_HEADS = ATTN_HEADS_PER_GROUP * len(ATTN_GROUPS)
ATTN_HEAD_DIM = 128
ATTN_GROUP_WIDTH = ATTN_HEADS_PER_GROUP * ATTN_HEAD_DIM
ATTN_SIDE = 64
ALIBI_MAX_BIAS = 8.0
NEG_INF = -1e30
HGRN_HEADS = 8
HGRN_HEAD_DIM = 128
HGRN_WIDTH = HGRN_HEADS * HGRN_HEAD_DIM
LN_EPS = 1e-5
RMS_EPS = 1e-6

W_IN_BLOCK = 512
ATTN_COLUMN = D_MODEL
HGRN_COLUMN = ATTN_COLUMN + 3 * ATTN_HEADS * ATTN_HEAD_DIM
GATES_COLUMN = HGRN_COLUMN + 5 * HGRN_WIDTH
PLAIN_HI = 1024
PLAIN_AQ = 2048
PLAIN_AK = 2560
PLAIN_AV = 3072
PLAIN_WIDTH = 3584

INPROJ_ROWS = 256
INPROJ_BLOCKS = (4, 4, 7)
DILATED_ROWS = 1024
ATTN_QUERY_BLOCK = 128
ATTN_KEY_WINDOW = ATTN_QUERY_BLOCK + 2 * ATTN_SIDE
ATTN_TOKENS_PER_STEP = 2048
ATTN_BLOCKS_PER_ITER = 2
ATTN_STAT_LANES = 32
ATTN_WINDOW_OFFSETS = 3
HGRN_BLOCK = 64
HGRN_HEADS_PER_STEP = 4
HGRN_FINISH_ROWS = 256
HGRN_CHUNK = 512
MERGE_ROWS = 512
MERGE_SUBTILES = 2
FFN_ROWS = 512
FFN_SUBTILES = 2
POOL_ROWS = 512
POOL_HALO = 16
VMEM_LIMIT = 52 * 1024 * 1024


def _params(*semantics):
    return pltpu.CompilerParams(dimension_semantics=semantics, vmem_limit_bytes=VMEM_LIMIT)


def _sigmoid(x):
    return 0.5 * jnp.tanh(0.5 * x) + 0.5


def _layer_norm(h, g, b):
    mu = jnp.mean(h, axis=-1, keepdims=True)
    d = h - mu
    var = jnp.mean(d * d, axis=-1, keepdims=True)
    return d * lax.rsqrt(var + LN_EPS) * g + b


def _inproj_kernel(x_ref, *refs):
    n_act, n_forget, n_plain = INPROJ_BLOCKS
    w_refs = refs[:n_act + n_forget + n_plain]
    lb_ref, act_ref, logf_ref, key_ref, plain_ref = refs[n_act + n_forget + n_plain:]
    x = x_ref[...].astype(BF16)
    wb = W_IN_BLOCK
    z = [jnp.dot(x, w_ref[...], preferred_element_type=F32) for w_ref in w_refs[:n_act + n_forget]]
    for n, w_ref in enumerate(w_refs[n_act + n_forget:]):
        plain_ref[:, n * wb:(n + 1) * wb] = jnp.dot(x, w_ref[...], preferred_element_type=F32).astype(plain_ref.dtype)
    for n in range(n_act):
        act_ref[:, n * wb:(n + 1) * wb] = (z[n] * _sigmoid(z[n])).astype(act_ref.dtype)
    for n in range(n_forget):
        cols = slice(n * wb, (n + 1) * wb)
        lb = lb_ref[:, cols]
        sig = _sigmoid(z[n_act + n])
        logf_ref[:, cols] = jnp.log(lb + (1.0 - lb) * sig).astype(logf_ref.dtype)
        key_ref[:, cols] = ((1.0 - lb) * (1.0 - sig)).astype(key_ref.dtype)


def _inproj(x, w_in_bf, layer, lb):
    T, D = x.shape
    tm = min(T, INPROJ_ROWS)
    wb = W_IN_BLOCK
    hw = HGRN_WIDTH
    span = lambda start, width: list(range(start // wb, (start + width) // wb))
    attn_part = ATTN_HEADS * ATTN_HEAD_DIM
    act_blocks = span(HGRN_COLUMN, hw) + span(HGRN_COLUMN + 4 * hw, hw)
    forget_blocks = span(HGRN_COLUMN + 2 * hw, 2 * hw)
    plain_blocks = (span(0, D_MODEL) + span(HGRN_COLUMN + hw, hw)
                    + [(ATTN_COLUMN + part * attn_part) // wb for part in range(3)])
    blocks = act_blocks + forget_blocks + plain_blocks
    assert (len(act_blocks), len(forget_blocks), len(plain_blocks)) == INPROJ_BLOCKS
    wspec = lambda blk: pl.BlockSpec((None, D, wb), lambda i: (layer, 0, blk), pipeline_mode=pl.Buffered(1))
    ospec = lambda n: pl.BlockSpec((tm, n * wb), lambda i: (i, 0))
    oshape = lambda n: jax.ShapeDtypeStruct((T, n * wb), BF16)
    n_act, n_forget, n_plain = INPROJ_BLOCKS
    return pl.pallas_call(
        _inproj_kernel,
        grid=(T // tm,),
        in_specs=[pl.BlockSpec((tm, D), lambda i: (i, 0))] + [wspec(blk) for blk in blocks]
                 + [pl.BlockSpec((1, n_forget * wb), lambda i: (0, 0))],
        out_specs=[ospec(n_act), ospec(n_forget), ospec(n_forget), ospec(n_plain)],
        out_shape=[oshape(n_act), oshape(n_forget), oshape(n_forget), oshape(n_plain)],
        compiler_params=_params("parallel"),
        name="inproj",
    )(x, *([w_in_bf] * len(blocks)), lb)


def _inproj_dilated_kernel(x_ref, wq_ref, wk_ref, wv_ref, o_ref, slab_ref, xp_ref, *, dilation):
    n = x_ref.shape[0] // dilation
    for c in range(D_MODEL // LANES):
        slab_ref[c] = x_ref[:, c * LANES:(c + 1) * LANES]
    for r in range(dilation):
        for c in range(D_MODEL // LANES):
            xp_ref[r * n:(r + 1) * n, c * LANES:(c + 1) * LANES] = (
                slab_ref[c, pl.ds(r, n, stride=dilation), :].astype(BF16))
    xp = xp_ref[...]
    for part, w_ref in enumerate((wq_ref, wk_ref, wv_ref)):
        res = jnp.dot(xp, w_ref[...], preferred_element_type=F32).astype(o_ref.dtype)
        cols = slice(part * ATTN_GROUP_WIDTH, (part + 1) * ATTN_GROUP_WIDTH)
        for r in range(dilation):
            o_ref[0, r, :, cols] = res[r * n:(r + 1) * n]


def _inproj_dilated(x_f32, w_in_bf, layer, blocks, batch, seq, dilation):
    gw = ATTN_GROUP_WIDTH
    tb = min(seq, DILATED_ROWS)
    steps = seq // tb
    n = tb // dilation
    wspec = lambda blk: pl.BlockSpec((None, D_MODEL, gw), lambda i: (layer, 0, blk))
    return pl.pallas_call(
        functools.partial(_inproj_dilated_kernel, dilation=dilation),
        grid=(batch * steps,),
        in_specs=[pl.BlockSpec((tb, D_MODEL), lambda i: (i, 0))] + [wspec(blk) for blk in blocks],
        out_specs=pl.BlockSpec((1, dilation, n, 3 * gw), lambda i: (i // steps, 0, i % steps, 0)),
        out_shape=jax.ShapeDtypeStruct((batch, dilation, seq // dilation, 3 * gw), BF16),
        scratch_shapes=[pltpu.VMEM((D_MODEL // LANES, tb, LANES), F32), pltpu.VMEM((tb, D_MODEL), BF16)],
        compiler_params=_params("parallel"),
        name="inproj_dilated%d" % dilation,
    )(x_f32, w_in_bf, w_in_bf, w_in_bf)


def _attn_kernel(q_ref, k_ref, v_ref, o_ref, m_ref, l_ref, oacc_ref, macc_ref, lacc_ref, bias_ref, sa_ref, sb_ref, *,
                 seq_len, dilation, slopes):
    c = pl.program_id(1)
    lq = q_ref.shape[2]
    qb, kw = ATTN_QUERY_BLOCK, ATTN_KEY_WINDOW
    nq = lq // qb
    per_iter = ATTN_BLOCKS_PER_ITER
    scale2 = ATTN_HEAD_DIM ** -0.5 * LOG2_E
    heads = range(ATTN_HEADS_PER_GROUP)
    lanes = [slice(h * ATTN_HEAD_DIM, (h + 1) * ATTN_HEAD_DIM) for h in heads]

    @pl.when((pl.program_id(0) == 0) & (c == 0))
    def _():
        row = lax.broadcasted_iota(jnp.int32, (qb, kw), 0)
        col = lax.broadcasted_iota(jnp.int32, (qb, kw), 1)
        for w in range(ATTN_WINDOW_OFFSETS):
            dist = jnp.abs(col - row - w * ATTN_SIDE)
            distf = dist.astype(F32) * float(dilation)
            for h in heads:
                bias_ref[h, w] = jnp.where(dist <= ATTN_SIDE, distf * (-slopes[h] * LOG2_E), NEG_INF * LOG2_E)

    stat_group = lax.broadcasted_iota(jnp.int32, (kw, ATTN_HEAD_DIM), 1) // ATTN_STAT_LANES
    ones_cols = [jnp.where(stat_group == h, 1.0, 0.0).astype(BF16) for h in heads]
    out_group = lax.broadcasted_iota(jnp.int32, (qb, ATTN_HEAD_DIM), 1) // ATTN_STAT_LANES

    def block_coords(idx):
        r = idx // nq
        r0 = pl.multiple_of((idx % nq) * qb, qb)
        t0 = c * lq + r0
        ks = pl.multiple_of(jnp.clip(t0 - ATTN_SIDE, 0, seq_len - kw), ATTN_SIDE)
        return r, r0, t0, ks

    def score_products(it, s_ref):
        for u in range(per_iter):
            r, r0, _, ks = block_coords(it * per_iter + u)
            q = q_ref[0, r, pl.ds(r0, qb), :]
            k = k_ref[0, r, pl.ds(ks, kw), :]
            for h in heads:
                s_ref[u * len(heads) + h] = lax.dot_general(
                    q[:, lanes[h]], k[:, lanes[h]], (((1,), (1,)), ((), ())), preferred_element_type=F32)

    n_iter = dilation * nq // per_iter
    score_products(0, sa_ref)

    def half_body(it, s_ref, next_ref):
        score_products(jnp.minimum(it + 1, n_iter - 1), next_ref)
        blocks = []
        for u in range(per_iter):
            r, r0, t0, ks = block_coords(it * per_iter + u)
            if dilation == 1:
                out_rows = pl.ds(r0, qb)
            else:
                out_rows = pl.ds(r0 * dilation + r, qb, stride=dilation)
            blocks.append(((t0 - ks) // ATTN_SIDE, out_rows, v_ref[0, r, pl.ds(ks, kw), :]))
        probs = []
        for u, (w, _, _) in enumerate(blocks):
            block_probs = []
            for h in heads:
                s = s_ref[u * len(heads) + h] * scale2 + bias_ref[h, w]
                m = jnp.max(s, axis=1, keepdims=True)
                block_probs.append((jnp.exp2(s - m).astype(BF16), m))
            probs.append(block_probs)
        pv = [[jnp.dot(block_probs[h][0], jnp.concatenate([v[:, lanes[h]], ones_cols[h]], axis=1),
                       preferred_element_type=F32) for h in heads]
              for (_, _, v), block_probs in zip(blocks, probs)]
        for (_, out_rows, _), block_probs, block_pv in zip(blocks, probs, pv):
            m_all = jnp.broadcast_to(block_probs[-1][1], (qb, ATTN_HEAD_DIM))
            l_all = block_pv[-1][:, ATTN_HEAD_DIM:]
            for h in reversed(heads[:-1]):
                m_all = jnp.where(out_group == h, block_probs[h][1], m_all)
                l_all = l_all + block_pv[h][:, ATTN_HEAD_DIM:]
            for h in heads:
                oacc_ref[h, out_rows, :] = block_pv[h][:, :ATTN_HEAD_DIM]
            macc_ref[out_rows, :] = m_all
            lacc_ref[out_rows, :] = l_all

    def body(it2, carry):
        half_body(2 * it2, sa_ref, sb_ref)
        half_body(2 * it2 + 1, sb_ref, sa_ref)
        return carry

    lax.fori_loop(0, n_iter // 2, body, 0, unroll=True)
    for h in heads:
        o_ref[0, :, lanes[h]] = oacc_ref[h].astype(o_ref.dtype)
    m_ref[0] = macc_ref[...]
    l_ref[0] = lacc_ref[...]


def _attention_group(q_arr, k_arr, v_arr, cols, batch, seq, dilation, slopes):
    L = seq // dilation
    gw = ATTN_GROUP_WIDTH
    tc = min(seq, ATTN_TOKENS_PER_STEP)
    lq = tc // dilation
    kern = functools.partial(_attn_kernel, seq_len=L, dilation=dilation, slopes=slopes)
    qc, kc, vc = cols
    stat = pl.BlockSpec((1, tc, ATTN_HEAD_DIM), lambda b, c: (b, c, 0))
    stat_shape = jax.ShapeDtypeStruct((batch, seq, ATTN_HEAD_DIM), F32)
    o, m, l = pl.pallas_call(
        kern,
        grid=(batch, seq // tc),
        in_specs=[pl.BlockSpec((1, dilation, lq, gw), lambda b, c: (b, 0, c, qc)),
                  pl.BlockSpec((1, dilation, L, gw), lambda b, c: (b, 0, 0, kc)),
                  pl.BlockSpec((1, dilation, L, gw), lambda b, c: (b, 0, 0, vc))],
        out_specs=[pl.BlockSpec((1, tc, gw), lambda b, c: (b, c, 0)), stat, stat],
        out_shape=[jax.ShapeDtypeStruct((batch, seq, gw), BF16), stat_shape, stat_shape],
        scratch_shapes=[pltpu.VMEM((ATTN_HEADS_PER_GROUP, tc, ATTN_HEAD_DIM), F32),
                        pltpu.VMEM((tc, ATTN_HEAD_DIM), F32), pltpu.VMEM((tc, ATTN_HEAD_DIM), F32),
                        pltpu.VMEM((ATTN_HEADS_PER_GROUP, ATTN_WINDOW_OFFSETS, ATTN_QUERY_BLOCK, ATTN_KEY_WINDOW), F32),
                        pltpu.VMEM((ATTN_BLOCKS_PER_ITER * ATTN_HEADS_PER_GROUP, ATTN_QUERY_BLOCK, ATTN_KEY_WINDOW), F32),
                        pltpu.VMEM((ATTN_BLOCKS_PER_ITER * ATTN_HEADS_PER_GROUP, ATTN_QUERY_BLOCK, ATTN_KEY_WINDOW), F32)],
        compiler_params=_params("arbitrary", "arbitrary"),
        name="attn_dil%d" % dilation,
    )(q_arr, k_arr, v_arr)
    T = batch * seq
    return o.reshape(T, gw), m.reshape(T, ATTN_HEAD_DIM), l.reshape(T, ATTN_HEAD_DIM)


def _running_sum_rows(x, row_index, reverse):
    n = x.shape[0]
    shift = 1
    while shift < n:
        if reverse:
            x = x + jnp.where(row_index < n - shift, pltpu.roll(x, n - shift, 0), 0.0)
        else:
            x = x + jnp.where(row_index >= shift, pltpu.roll(x, shift, 0), 0.0)
        shift *= 2
    return x


def _hgrn_kernel(qf_ref, vf_ref, lf_ref, kf_ref, qb_ref, vb_ref, lb_ref, kb_ref, g_ref, nw_ref, o_ref,
                 accf_ref, accb_ref, sf_ref, sb_ref):
    c = pl.program_id(2)
    nc = pl.num_programs(2)
    chunk = qf_ref.shape[1]
    seq = o_ref.shape[1]
    hp = HGRN_HEADS_PER_STEP
    blk = HGRN_BLOCK
    half = blk // 2
    nblk = chunk // blk
    dh = HGRN_HEAD_DIM
    pw = 2 * dh

    @pl.when(c == 0)
    def _():
        sf_ref[...] = jnp.zeros_like(sf_ref)
        sb_ref[...] = jnp.zeros_like(sb_ref)

    row_index = lax.broadcasted_iota(jnp.int32, (blk, pw), 0)
    ti = lax.broadcasted_iota(jnp.int32, (blk, 2 * blk), 0)
    si = lax.broadcasted_iota(jnp.int32, (blk, 2 * blk), 1) % blk
    causal = si <= ti
    anticausal = si >= ti
    zeros_kv = jnp.zeros((blk, dh), BF16)
    zeros_st = jnp.zeros((dh, dh), BF16)

    def block_diag(a, b, z):
        return jnp.concatenate([jnp.concatenate([a, z], axis=1), jnp.concatenate([z, b], axis=1)], axis=0)

    def load_unit(pair, r0, q_ref, v_ref, logf_ref, key_ref, s_ref):
        lanes = slice(pair * pw, (pair + 1) * pw)
        rows = pl.ds(r0, blk)
        return (q_ref[0, rows, lanes], v_ref[0, rows, lanes], logf_ref[0, rows, lanes], key_ref[0, rows, lanes],
                s_ref[2 * pair], s_ref[2 * pair + 1])

    def prepare_unit(loaded, backward):
        q, v, logf, k, st0, st1 = loaded
        a = _running_sum_rows(logf.astype(F32), row_index, backward)
        if backward:
            total = a[0:1, :]
            mid = a[half:half + 1, :]
        else:
            total = a[blk - 1:blk, :]
            mid = a[half - 1:half, :]
        q_mid = q * jnp.exp(a - mid).astype(BF16)
        k_mid_b = k * jnp.exp(mid - a).astype(BF16)
        q_in = q_mid * jnp.exp(mid).astype(BF16)
        k_out = k_mid_b * jnp.exp(total - mid).astype(BF16)
        decay = jnp.broadcast_to(jnp.exp(total), (SUBLANES, pw)).T[:, 0:1]
        return dict(q_mid=q_mid, k_diag=block_diag(k_mid_b[:, :dh], k_mid_b[:, dh:], zeros_kv),
                    q_in=q_in, s_diag=block_diag(st0.astype(BF16), st1.astype(BF16), zeros_st),
                    v=v, v_diag=block_diag(v[:, :dh], v[:, dh:], zeros_kv), k_out=k_out,
                    decay=decay, states=(st0, st1), mask=anticausal if backward else causal)

    nt = (((1,), (1,)), ((), ()))
    tn = (((0,), (0,)), ((), ()))

    def body(n, carry):
        rf = pl.multiple_of(n * blk, blk)
        rb = pl.multiple_of((nblk - 1 - n) * blk, blk)
        units = []
        for pair in range(hp // 2):
            units.append((pair, False, load_unit(pair, rf, qf_ref, vf_ref, lf_ref, kf_ref, sf_ref)))
            units.append((pair, True, load_unit(pair, rb, qb_ref, vb_ref, lb_ref, kb_ref, sb_ref)))
        prepared = [prepare_unit(loaded, backward) for _, backward, loaded in units]
        att = [lax.dot_general(u["q_mid"], u["k_diag"], nt, preferred_element_type=F32) for u in prepared]
        inter = [jnp.dot(u["q_in"], u["s_diag"], preferred_element_type=F32) for u in prepared]
        upd = [[lax.dot_general(u["k_out"][:, j * dh:(j + 1) * dh], u["v"][:, j * dh:(j + 1) * dh], tn,
                                preferred_element_type=F32) for j in range(2)] for u in prepared]
        att = [jnp.where(u["mask"], s, 0.0).astype(BF16) for u, s in zip(prepared, att)]
        intra = [jnp.dot(s, u["v_diag"], preferred_element_type=F32) for u, s in zip(prepared, att)]
        for i, (pair, backward, _) in enumerate(units):
            u = prepared[i]
            lanes = slice(pair * pw, (pair + 1) * pw)
            o = intra[i] + inter[i]
            states = [st * u["decay"][j * dh:(j + 1) * dh, :] + upd[i][j] for j, st in enumerate(u["states"])]
            if backward:
                accb_ref[pl.ds(pl.multiple_of((nc - 1 - c) * chunk + rb, blk), blk), lanes] = o
                sb_ref[2 * pair], sb_ref[2 * pair + 1] = states
            else:
                accf_ref[pl.ds(pl.multiple_of(c * chunk + rf, blk), blk), lanes] = o
                sf_ref[2 * pair], sf_ref[2 * pair + 1] = states
        return carry

    lax.fori_loop(0, nblk, body, 0, unroll=True)

    @pl.when(c == nc - 1)
    def _():
        fin = min(seq, HGRN_FINISH_ROWS)

        def finish(t, carry):
            rows = pl.ds(pl.multiple_of(t * fin, fin), fin)
            for h in range(hp):
                lanes = slice(h * dh, (h + 1) * dh)
                o = accf_ref[rows, lanes] + accb_ref[rows, lanes]
                o = o * lax.rsqrt(jnp.mean(o * o, axis=-1, keepdims=True) + RMS_EPS)
                o = o * nw_ref[:, lanes] * g_ref[0, rows, lanes].astype(F32)
                o_ref[0, rows, lanes] = o.astype(o_ref.dtype)
            return carry

        lax.fori_loop(0, seq // fin, finish, 0, unroll=4)


def _hgrn(qg3, plain3, logf3, key3, norm_w):
    batch, seq, _ = qg3.shape
    hp = HGRN_HEADS_PER_STEP
    w = hp * HGRN_HEAD_DIM
    chunk = min(seq, HGRN_CHUNK)
    nc = seq // chunk

    def fwd(offset):
        base = offset // w
        return pl.BlockSpec((1, chunk, w), lambda b, h, c: (b, c, base + h))

    def bwd(offset):
        base = offset // w
        return pl.BlockSpec((1, chunk, w), lambda b, h, c: (b, nc - 1 - c, base + h))

    gbase = HGRN_WIDTH // w
    return pl.pallas_call(
        _hgrn_kernel,
        grid=(batch, HGRN_HEADS // hp, nc),
        in_specs=[fwd(0), fwd(PLAIN_HI), fwd(0), fwd(0),
                  bwd(0), bwd(PLAIN_HI), bwd(HGRN_WIDTH), bwd(HGRN_WIDTH),
                  pl.BlockSpec((1, seq, w), lambda b, h, c: (b, 0, gbase + h)),
                  pl.BlockSpec((1, w), lambda b, h, c: (0, h))],
        out_specs=pl.BlockSpec((1, seq, w), lambda b, h, c: (b, 0, h)),
        out_shape=jax.ShapeDtypeStruct((batch, seq, HGRN_WIDTH), BF16),
        scratch_shapes=[pltpu.VMEM((seq, w), F32), pltpu.VMEM((seq, w), F32),
                        pltpu.VMEM((hp, HGRN_HEAD_DIM, HGRN_HEAD_DIM), F32),
                        pltpu.VMEM((hp, HGRN_HEAD_DIM, HGRN_HEAD_DIM), F32)],
        compiler_params=_params("parallel", "parallel", "arbitrary"),
        name="hgrn",
    )(qg3, plain3, logf3, key3, qg3, plain3, logf3, key3, qg3, norm_w)


def _pool_kernel(prev_ref, cur_ref, next_ref, pw_ref, ps_ref, o_ref, *, seq_len):
    i = pl.program_id(1)
    last = pl.num_programs(1) - 1
    ts = cur_ref.shape[1]
    n = ts + 2 * POOL_HALO
    pos = i * ts + lax.broadcasted_iota(jnp.int32, (ts, 1), 0)
    for g in range(POOL_GROUPS):
        lanes = slice(g * POOL_GROUP_DIM, (g + 1) * POOL_GROUP_DIM)
        half = POOL_WINDOWS[g] // 2
        cur = cur_ref[0, :, lanes].astype(F32)
        prev = jnp.where(i > 0, prev_ref[0, :, lanes].astype(F32), 0.0)
        nxt = jnp.where(i < last, next_ref[0, :, lanes].astype(F32), 0.0)
        ext = jnp.concatenate([prev, cur, nxt], axis=0)
        w = ext + pltpu.roll(ext, 1, 0)
        step = 1
        while step < half:
            w = pltpu.roll(w, step, 0) + pltpu.roll(w, n - step, 0)
            step *= 2
        wsum = w[POOL_HALO:POOL_HALO + ts]
        count = (jnp.minimum(pos + half, seq_len) - jnp.maximum(pos - half, 0)).astype(F32)
        mixed = wsum / count - cur
        y = jnp.dot(mixed.astype(BF16), pw_ref[g], preferred_element_type=F32)
        o_ref[0, :, lanes] = (y * ps_ref[:, lanes]).astype(o_ref.dtype)


def _pool(plain3, pool_w, pool_scale):
    batch, seq, _ = plain3.shape
    ts = min(seq, POOL_ROWS)
    hb = ts // POOL_HALO
    nhalo = seq // POOL_HALO
    kern = functools.partial(_pool_kernel, seq_len=seq)
    return pl.pallas_call(
        kern,
        grid=(batch, seq // ts),
        in_specs=[pl.BlockSpec((1, POOL_HALO, D_MODEL), lambda b, i: (b, jnp.maximum(i * hb - 1, 0), 0)),
                  pl.BlockSpec((1, ts, D_MODEL), lambda b, i: (b, i, 0)),
                  pl.BlockSpec((1, POOL_HALO, D_MODEL), lambda b, i: (b, jnp.minimum((i + 1) * hb, nhalo - 1), 0)),
                  pl.BlockSpec((POOL_GROUPS, POOL_GROUP_DIM, POOL_GROUP_DIM), lambda b, i: (0, 0, 0)),
                  pl.BlockSpec((1, D_MODEL), lambda b, i: (0, 0))],
        out_specs=pl.BlockSpec((1, ts, D_MODEL), lambda b, i: (b, i, 0)),
        out_shape=jax.ShapeDtypeStruct((batch, seq, D_MODEL), BF16),
        compiler_params=_params("parallel", "parallel"),
        name="pool",
    )(plain3, plain3, plain3, pool_w, pool_scale)


def _merge_kernel(ya_ref, o1_ref, o2_ref, o3_ref, m1_ref, m2_ref, m3_ref, l1_ref, l2_ref, l3_ref, hc_ref,
                  xin_ref, x_ref, wa_ref, wb_ref, wc_ref, wo_ref, *rest, alpha, nsub):
    wgate_refs = rest[:-4]
    g_ref, b_ref, xo_ref, xb_ref = rest[-4:]
    d = D_MODEL
    dh = ATTN_HEAD_DIM
    sub = x_ref.shape[0] // nsub
    rows = [slice(s * sub, (s + 1) * sub) for s in range(nsub)]
    xin = [xin_ref[r, :].astype(BF16) for r in rows]
    y_a = [jnp.dot(ya_ref[r, :], wa_ref[...], preferred_element_type=F32) for r in rows]
    y_c = [jnp.dot(hc_ref[r, :], wc_ref[...], preferred_element_type=F32) for r in rows]
    per_gate = d // W_IN_BLOCK
    gates = [[jnp.concatenate([jnp.dot(x, w_ref[...], preferred_element_type=F32)
                               for w_ref in wgate_refs[n * per_gate:(n + 1) * per_gate]], axis=1)
              for n in range(3)] for x in xin]
    att = []
    for r in rows:
        ms = [m1_ref[r, :], m2_ref[r, :], m3_ref[r, :]]
        ls = [l1_ref[r, :], l2_ref[r, :], l3_ref[r, :]]
        top = jnp.maximum(jnp.maximum(ms[0], ms[1]), ms[2])
        es = [jnp.exp2(m - top) for m in ms]
        inv = 1.0 / (es[0] * ls[0] + es[1] * ls[1] + es[2] * ls[2])
        ws = [e * inv for e in es]
        heads = []
        for h in range(ATTN_HEADS_PER_GROUP):
            lanes = slice(h * dh, (h + 1) * dh)
            stat = slice(h * ATTN_STAT_LANES, h * ATTN_STAT_LANES + 1)
            acc = None
            for w, o_ref in zip(ws, (o1_ref, o2_ref, o3_ref)):
                term = jnp.broadcast_to(w[:, stat], (sub, dh)) * o_ref[r, lanes].astype(F32)
                acc = term if acc is None else acc + term
            heads.append(acc)
        att.append(jnp.concatenate(heads, axis=1).astype(BF16))
    y_b = [jnp.dot(a, wb_ref[...], preferred_element_type=F32) for a in att]
    merged = [(_sigmoid(gates[s][0]) * y_a[s] + _sigmoid(gates[s][1]) * y_b[s]
               + _sigmoid(gates[s][2]) * y_c[s]).astype(BF16) for s in range(nsub)]
    mix = [jnp.dot(mg, wo_ref[...], preferred_element_type=F32) for mg in merged]
    for s, r in enumerate(rows):
        x1 = _layer_norm(alpha * x_ref[r, :] + mix[s], g_ref[...], b_ref[...])
        xo_ref[r, :] = x1
        xb_ref[r, :] = x1.astype(BF16)


def _merge(y_pool, o_list, m_list, l_list, y_hgrn, x_bf, x, wa, wb, wc, wo, w_in_bf, layer, ln_g, ln_b, alpha):
    T = x.shape[0]
    tm = min(T, MERGE_ROWS)
    gw = ATTN_GROUP_WIDTH
    d = D_MODEL
    row = lambda w: pl.BlockSpec((tm, w), lambda i: (i, 0))
    const = lambda a: pl.BlockSpec(a.shape, lambda i: (0,) * a.ndim, pipeline_mode=pl.Buffered(1))
    kern = functools.partial(_merge_kernel, alpha=alpha, nsub=MERGE_SUBTILES)
    stat = row(ATTN_HEAD_DIM)
    gate_blocks = range(GATES_COLUMN // W_IN_BLOCK, (GATES_COLUMN + 3 * d) // W_IN_BLOCK)
    gate_specs = [pl.BlockSpec((None, d, W_IN_BLOCK), lambda i, blk=blk: (layer, 0, blk),
                               pipeline_mode=pl.Buffered(1)) for blk in gate_blocks]
    return pl.pallas_call(
        kern,
        grid=(T // tm,),
        in_specs=[row(d), row(gw), row(gw), row(gw), stat, stat, stat, stat, stat, stat, row(d), row(d),
                  row(d), const(wa), const(wb), const(wc), const(wo)] + gate_specs + [const(ln_g), const(ln_b)],
        out_specs=[row(d), row(d)],
        out_shape=[jax.ShapeDtypeStruct((T, d), F32), jax.ShapeDtypeStruct((T, d), BF16)],
        compiler_params=_params("parallel"),
        name="merge",
    )(y_pool, *o_list, *m_list, *l_list, y_hgrn, x_bf, x, wa, wb, wc, wo, *([w_in_bf] * len(gate_specs)), ln_g, ln_b)


def _ffn_kernel(xb_ref, x_ref, p_ref, wg_ref, wu_ref, wd_ref, wpp_ref, wpg_ref, g_ref, b_ref,
                xo_ref, xbo_ref, *, alpha, nsub):
    sub = xb_ref.shape[0] // nsub
    rows = [slice(s * sub, (s + 1) * sub) for s in range(nsub)]
    xb = [xb_ref[r, :] for r in rows]
    gate = [jnp.dot(x, wg_ref[...], preferred_element_type=F32) for x in xb]
    up = [jnp.dot(x, wu_ref[...], preferred_element_type=F32) for x in xb]
    pgate = [jnp.dot(x, wpg_ref[...], preferred_element_type=F32) for x in xb]
    ple = [jnp.dot(p_ref[r, :].astype(BF16), wpp_ref[...], preferred_element_type=F32) for r in rows]
    hidden = [(g * _sigmoid(g) * u).astype(BF16) for g, u in zip(gate, up)]
    ffn = [jnp.dot(h, wd_ref[...], preferred_element_type=F32) for h in hidden]
    for s, r in enumerate(rows):
        x2 = _layer_norm(alpha * x_ref[r, :] + ffn[s] + ple[s] * _sigmoid(pgate[s]), g_ref[...], b_ref[...])
        xo_ref[r, :] = x2
        xbo_ref[r, :] = x2.astype(BF16)


def _ffn(x1_bf, x1, p, wg, wu, wd, wpp, wpg, ln_g, ln_b, alpha):
    T = x1.shape[0]
    tm = min(T, FFN_ROWS)
    d = D_MODEL
    row = lambda w: pl.BlockSpec((tm, w), lambda i: (i, 0))
    const = lambda a: pl.BlockSpec(a.shape, lambda i: (0,) * a.ndim, pipeline_mode=pl.Buffered(1))
    kern = functools.partial(_ffn_kernel, alpha=alpha, nsub=FFN_SUBTILES)
    return pl.pallas_call(
        kern,
        grid=(T // tm,),
        in_specs=[row(d), row(d), row(PLE_DIM), const(wg), const(wu), const(wd), const(wpp),
                  const(wpg), const(ln_g), const(ln_b)],
        out_specs=[row(d), row(d)],
        out_shape=[jax.ShapeDtypeStruct((T, d), F32), jax.ShapeDtypeStruct((T, d), BF16)],
        compiler_params=_params("parallel"),
        name="ffn",
    )(x1_bf, x1, p, wg, wu, wd, wpp, wpg, ln_g, ln_b)


def kernel(x, p, w_in, pool_w, pool_scale, w_branch_a, w_branch_b, w_branch_c, hgrn_lb_logits, hgrn_norm_w, w_out, ln1_g, ln1_b, w_ffn_gate, w_ffn_up, w_ffn_down, w_ple_proj, w_ple_gate, ln2_g, ln2_b):
    batch, seq, d = x.shape
    depth = w_in.shape[0]
    T = batch * seq
    alpha = float((2 * depth) ** 0.25)

    lb = jnp.cumsum(jax.nn.softmax(hgrn_lb_logits.astype(F32), axis=0), axis=0)
    lb = lb - lb[:1]

    w_in_bf = w_in.astype(BF16)
    bf = lambda a: a.astype(BF16)
    pool_w_b, wa, wb, wc, wo = bf(pool_w), bf(w_branch_a), bf(w_branch_b), bf(w_branch_c), bf(w_out)
    wg, wu, wd, wpp, wpg = bf(w_ffn_gate), bf(w_ffn_up), bf(w_ffn_down), bf(w_ple_proj), bf(w_ple_gate)

    slopes = [2.0 ** (-ALIBI_MAX_BIAS * (i + 1) / ATTN_HEADS) for i in range(ATTN_HEADS)]
    hpg = ATTN_HEADS_PER_GROUP
    gw = ATTN_GROUP_WIDTH

    xf = x.reshape(T, d).astype(F32)
    xb = xf
    for i in range(depth):
        qg, logf, key, plain = _inproj(xb, w_in_bf, i, lb[i:i + 1])
        plain3 = plain.reshape(batch, seq, PLAIN_WIDTH)
        o_list, m_list, l_list = [], [], []
        for g, (_, dil) in enumerate(ATTN_GROUPS):
            gs = tuple(slopes[g * hpg:(g + 1) * hpg])
            if dil == 1:
                qkv = plain.reshape(batch, 1, seq, PLAIN_WIDTH)
                cols = (PLAIN_AQ // gw, PLAIN_AK // gw, PLAIN_AV // gw)
            else:
                blocks = [(ATTN_COLUMN + part * ATTN_HEADS * ATTN_HEAD_DIM) // W_IN_BLOCK + g for part in range(3)]
                qkv = _inproj_dilated(xf, w_in_bf, i, blocks, batch, seq, dil)
                cols = (0, 1, 2)
            o, m, l = _attention_group(qkv, qkv, qkv, cols, batch, seq, dil, gs)
            o_list.append(o)
            m_list.append(m)
            l_list.append(l)
        y_hgrn = _hgrn(qg.reshape(batch, seq, 2 * HGRN_WIDTH), plain3,
                       logf.reshape(batch, seq, 2 * HGRN_WIDTH), key.reshape(batch, seq, 2 * HGRN_WIDTH),
                       hgrn_norm_w[i:i + 1])
        y_pool = _pool(plain3, pool_w_b[i], pool_scale[i:i + 1])
        x1, x1b = _merge(y_pool.reshape(T, d), o_list, m_list, l_list, y_hgrn.reshape(T, d), xb, xf,
                         wa[i], wb[i], wc[i], wo[i], w_in_bf, i, ln1_g[i:i + 1], ln1_b[i:i + 1], alpha)
        xf, xb = _ffn(x1b, x1, p[i].reshape(T, PLE_DIM), wg[i], wu[i], wd[i], wpp[i], wpg[i],
                      ln2_g[i:i + 1], ln2_b[i:i + 1], alpha)
    return xf.reshape(batch, seq, d).astype(x.dtype)
```

```python
import functools

import jax
import jax.numpy as jnp
from jax import lax
from jax.experimental import pallas as pl
from jax.experimental.pallas import tpu as pltpu

F32 = jnp.float32
BF16 = jnp.bfloat16
LANES = 128
SUBLANES = 8
LOG2_E = 1.4426950408889634

D_MODEL = 1024
PLE_DIM = 256
POOL_WINDOWS = (2, 4, 8, 16)
POOL_GROUPS = 4
POOL_GROUP_DIM = D_MODEL // POOL_GROUPS
ATTN_GROUPS = ((128, 1), (512, 4), (2048, 16))
ATTN_HEADS_PER_GROUP = 4
ATTN_HEADS = ATTN_HEADS_PER_GROUP * len(ATTN_GROUPS)
ATTN_HEAD_DIM = 128
ATTN_GROUP_WIDTH = ATTN_HEADS_PER_GROUP * ATTN_HEAD_DIM
ATTN_SIDE = 64
ALIBI_MAX_BIAS = 8.0
NEG_INF = -1e30
HGRN_HEADS = 8
HGRN_HEAD_DIM = 128
HGRN_WIDTH = HGRN_HEADS * HGRN_HEAD_DIM
LN_EPS = 1e-5
RMS_EPS = 1e-6

W_IN_BLOCK = 512
ATTN_COLUMN = D_MODEL
HGRN_COLUMN = ATTN_COLUMN + 3 * ATTN_HEADS * ATTN_HEAD_DIM
GATES_COLUMN = HGRN_COLUMN + 5 * HGRN_WIDTH
PLAIN_HI = 1024
PLAIN_AQ = 2048
PLAIN_AK = 2560
PLAIN_AV = 3072
PLAIN_WIDTH = 3584

INPROJ_ROWS = 256
INPROJ_BLOCKS = (4, 4, 7)
DILATED_ROWS = 1024
ATTN_QUERY_BLOCK = 128
ATTN_KEY_WINDOW = ATTN_QUERY_BLOCK + 2 * ATTN_SIDE
ATTN_TOKENS_PER_STEP = 2048
ATTN_BLOCKS_PER_ITER = 2
ATTN_STAT_LANES = 32
ATTN_WINDOW_OFFSETS = 3
HGRN_BLOCK = 64
HGRN_HEADS_PER_STEP = 4
HGRN_FINISH_ROWS = 256
HGRN_CHUNK = 512
MERGE_ROWS = 512
MERGE_SUBTILES = 2
FFN_ROWS = 512
FFN_SUBTILES = 2
POOL_ROWS = 512
POOL_HALO = 16
VMEM_LIMIT = 52 * 1024 * 1024


def _params(*semantics):
    return pltpu.CompilerParams(dimension_semantics=semantics, vmem_limit_bytes=VMEM_LIMIT)


def _layer_spec(stacked, layer):
    tail = stacked.shape[1:]
    return pl.BlockSpec((None,) + tail, lambda i: (layer,) + (0,) * len(tail), pipeline_mode=pl.Buffered(1))


def _sigmoid(x):
    return 0.5 * jnp.tanh(0.5 * x) + 0.5


def _layer_norm(h, g, b):
    mu = jnp.mean(h, axis=-1, keepdims=True)
    d = h - mu
    var = jnp.mean(d * d, axis=-1, keepdims=True)
    return d * lax.rsqrt(var + LN_EPS) * g + b


def _inproj_kernel(x_ref, *refs):
    n_act, n_forget, n_plain = INPROJ_BLOCKS
    w_refs = refs[:n_act + n_forget + n_plain]
    lb_ref, act_ref, logf_ref, key_ref, plain_ref = refs[n_act + n_forget + n_plain:]
    x = x_ref[...].astype(BF16)
    wb = W_IN_BLOCK
    z = [jnp.dot(x, w_ref[...], preferred_element_type=F32) for w_ref in w_refs[:n_act + n_forget]]
    for n, w_ref in enumerate(w_refs[n_act + n_forget:]):
        plain_ref[:, n * wb:(n + 1) * wb] = jnp.dot(x, w_ref[...], preferred_element_type=F32).astype(plain_ref.dtype)
    for n in range(n_act):
        act_ref[:, n * wb:(n + 1) * wb] = (z[n] * _sigmoid(z[n])).astype(act_ref.dtype)
    for n in range(n_forget):
        cols = slice(n * wb, (n + 1) * wb)
        lb = lb_ref[:, cols]
        sig = _sigmoid(z[n_act + n])
        logf_ref[:, cols] = jnp.log(lb + (1.0 - lb) * sig).astype(logf_ref.dtype)
        key_ref[:, cols] = ((1.0 - lb) * (1.0 - sig)).astype(key_ref.dtype)


def _inproj(x, w_in_bf, layer, lb):
    T, D = x.shape
    tm = min(T, INPROJ_ROWS)
    wb = W_IN_BLOCK
    hw = HGRN_WIDTH
    span = lambda start, width: list(range(start // wb, (start + width) // wb))
    attn_part = ATTN_HEADS * ATTN_HEAD_DIM
    act_blocks = span(HGRN_COLUMN, hw) + span(HGRN_COLUMN + 4 * hw, hw)
    forget_blocks = span(HGRN_COLUMN + 2 * hw, 2 * hw)
    plain_blocks = (span(0, D_MODEL) + span(HGRN_COLUMN + hw, hw)
                    + [(ATTN_COLUMN + part * attn_part) // wb for part in range(3)])
    blocks = act_blocks + forget_blocks + plain_blocks
    assert (len(act_blocks), len(forget_blocks), len(plain_blocks)) == INPROJ_BLOCKS
    wspec = lambda blk: pl.BlockSpec((None, D, wb), lambda i: (layer, 0, blk), pipeline_mode=pl.Buffered(1))
    ospec = lambda n: pl.BlockSpec((tm, n * wb), lambda i: (i, 0))
    oshape = lambda n: jax.ShapeDtypeStruct((T, n * wb), BF16)
    n_act, n_forget, n_plain = INPROJ_BLOCKS
    return pl.pallas_call(
        _inproj_kernel,
        grid=(T // tm,),
        in_specs=[pl.BlockSpec((tm, D), lambda i: (i, 0))] + [wspec(blk) for blk in blocks]
                 + [pl.BlockSpec((1, n_forget * wb), lambda i: (0, 0))],
        out_specs=[ospec(n_act), ospec(n_forget), ospec(n_forget), ospec(n_plain)],
        out_shape=[oshape(n_act), oshape(n_forget), oshape(n_forget), oshape(n_plain)],
        compiler_params=_params("parallel"),
        name="inproj",
    )(x, *([w_in_bf] * len(blocks)), lb)


def _inproj_dilated_kernel(x_ref, wq_ref, wk_ref, wv_ref, o_ref, slab_ref, xp_ref, *, dilation):
    n = x_ref.shape[0] // dilation
    for c in range(D_MODEL // LANES):
        slab_ref[c] = x_ref[:, c * LANES:(c + 1) * LANES]
    for r in range(dilation):
        for c in range(D_MODEL // LANES):
            xp_ref[r * n:(r + 1) * n, c * LANES:(c + 1) * LANES] = (
                slab_ref[c, pl.ds(r, n, stride=dilation), :].astype(BF16))
    xp = xp_ref[...]
    for part, w_ref in enumerate((wq_ref, wk_ref, wv_ref)):
        res = jnp.dot(xp, w_ref[...], preferred_element_type=F32).astype(o_ref.dtype)
        cols = slice(part * ATTN_GROUP_WIDTH, (part + 1) * ATTN_GROUP_WIDTH)
        for r in range(dilation):
            o_ref[0, r, :, cols] = res[r * n:(r + 1) * n]


def _inproj_dilated(x_f32, w_in_bf, layer, blocks, batch, seq, dilation):
    gw = ATTN_GROUP_WIDTH
    tb = min(seq, DILATED_ROWS)
    steps = seq // tb
    n = tb // dilation
    wspec = lambda blk: pl.BlockSpec((None, D_MODEL, gw), lambda i: (layer, 0, blk))
    return pl.pallas_call(
        functools.partial(_inproj_dilated_kernel, dilation=dilation),
        grid=(batch * steps,),
        in_specs=[pl.BlockSpec((tb, D_MODEL), lambda i: (i, 0))] + [wspec(blk) for blk in blocks],
        out_specs=pl.BlockSpec((1, dilation, n, 3 * gw), lambda i: (i // steps, 0, i % steps, 0)),
        out_shape=jax.ShapeDtypeStruct((batch, dilation, seq // dilation, 3 * gw), BF16),
        scratch_shapes=[pltpu.VMEM((D_MODEL // LANES, tb, LANES), F32), pltpu.VMEM((tb, D_MODEL), BF16)],
        compiler_params=_params("parallel"),
        name="inproj_dilated%d" % dilation,
    )(x_f32, w_in_bf, w_in_bf, w_in_bf)


def _attn_kernel(q_ref, k_ref, v_ref, o_ref, m_ref, l_ref, oacc_ref, macc_ref, lacc_ref, bias_ref, sa_ref, sb_ref, *,
                 seq_len, dilation, slopes):
    c = pl.program_id(1)
    lq = q_ref.shape[2]
    qb, kw = ATTN_QUERY_BLOCK, ATTN_KEY_WINDOW
    nq = lq // qb
    per_iter = ATTN_BLOCKS_PER_ITER
    scale2 = ATTN_HEAD_DIM ** -0.5 * LOG2_E
    heads = range(ATTN_HEADS_PER_GROUP)
    lanes = [slice(h * ATTN_HEAD_DIM, (h + 1) * ATTN_HEAD_DIM) for h in heads]

    @pl.when((pl.program_id(0) == 0) & (c == 0))
    def _():
        row = lax.broadcasted_iota(jnp.int32, (qb, kw), 0)
        col = lax.broadcasted_iota(jnp.int32, (qb, kw), 1)
        for w in range(ATTN_WINDOW_OFFSETS):
            dist = jnp.abs(col - row - w * ATTN_SIDE)
            distf = dist.astype(F32) * float(dilation)
            for h in heads:
                bias_ref[h, w] = jnp.where(dist <= ATTN_SIDE, distf * (-slopes[h] * LOG2_E), NEG_INF * LOG2_E)

    stat_group = lax.broadcasted_iota(jnp.int32, (kw, ATTN_HEAD_DIM), 1) // ATTN_STAT_LANES
    ones_cols = [jnp.where(stat_group == h, 1.0, 0.0).astype(BF16) for h in heads]
    out_group = lax.broadcasted_iota(jnp.int32, (qb, ATTN_HEAD_DIM), 1) // ATTN_STAT_LANES

    def block_coords(idx):
        r = idx // nq
        r0 = pl.multiple_of((idx % nq) * qb, qb)
        t0 = c * lq + r0
        ks = pl.multiple_of(jnp.clip(t0 - ATTN_SIDE, 0, seq_len - kw), ATTN_SIDE)
        return r, r0, t0, ks

    def score_products(it, s_ref):
        for u in range(per_iter):
            r, r0, _, ks = block_coords(it * per_iter + u)
            q = q_ref[0, r, pl.ds(r0, qb), :]
            k = k_ref[0, r, pl.ds(ks, kw), :]
            for h in heads:
                s_ref[u * len(heads) + h] = lax.dot_general(
                    q[:, lanes[h]], k[:, lanes[h]], (((1,), (1,)), ((), ())), preferred_element_type=F32)

    n_iter = dilation * nq // per_iter
    score_products(0, sa_ref)

    def half_body(it, s_ref, next_ref):
        score_products(jnp.minimum(it + 1, n_iter - 1), next_ref)
        blocks = []
        for u in range(per_iter):
            r, r0, t0, ks = block_coords(it * per_iter + u)
            if dilation == 1:
                out_rows = pl.ds(r0, qb)
            else:
                out_rows = pl.ds(r0 * dilation + r, qb, stride=dilation)
            blocks.append(((t0 - ks) // ATTN_SIDE, out_rows, v_ref[0, r, pl.ds(ks, kw), :]))
        probs = []
        for u, (w, _, _) in enumerate(blocks):
            block_probs = []
            for h in heads:
                s = s_ref[u * len(heads) + h] * scale2 + bias_ref[h, w]
                m = jnp.max(s, axis=1, keepdims=True)
                block_probs.append((jnp.exp2(s - m).astype(BF16), m))
            probs.append(block_probs)
        pv = [[jnp.dot(block_probs[h][0], jnp.concatenate([v[:, lanes[h]], ones_cols[h]], axis=1),
                       preferred_element_type=F32) for h in heads]
              for (_, _, v), block_probs in zip(blocks, probs)]
        for (_, out_rows, _), block_probs, block_pv in zip(blocks, probs, pv):
            m_all = jnp.broadcast_to(block_probs[-1][1], (qb, ATTN_HEAD_DIM))
            l_all = block_pv[-1][:, ATTN_HEAD_DIM:]
            for h in reversed(heads[:-1]):
                m_all = jnp.where(out_group == h, block_probs[h][1], m_all)
                l_all = l_all + block_pv[h][:, ATTN_HEAD_DIM:]
            for h in heads:
                oacc_ref[h, out_rows, :] = block_pv[h][:, :ATTN_HEAD_DIM]
            macc_ref[out_rows, :] = m_all
            lacc_ref[out_rows, :] = l_all

    def body(it2, carry):
        half_body(2 * it2, sa_ref, sb_ref)
        half_body(2 * it2 + 1, sb_ref, sa_ref)
        return carry

    lax.fori_loop(0, n_iter // 2, body, 0, unroll=True)
    for h in heads:
        o_ref[0, :, lanes[h]] = oacc_ref[h].astype(o_ref.dtype)
    m_ref[0] = macc_ref[...]
    l_ref[0] = lacc_ref[...]


def _attention_group(q_arr, k_arr, v_arr, cols, batch, seq, dilation, slopes):
    L = seq // dilation
    gw = ATTN_GROUP_WIDTH
    tc = min(seq, ATTN_TOKENS_PER_STEP)
    lq = tc // dilation
    kern = functools.partial(_attn_kernel, seq_len=L, dilation=dilation, slopes=slopes)
    qc, kc, vc = cols
    stat = pl.BlockSpec((1, tc, ATTN_HEAD_DIM), lambda b, c: (b, c, 0))
    stat_shape = jax.ShapeDtypeStruct((batch, seq, ATTN_HEAD_DIM), F32)
    o, m, l = pl.pallas_call(
        kern,
        grid=(batch, seq // tc),
        in_specs=[pl.BlockSpec((1, dilation, lq, gw), lambda b, c: (b, 0, c, qc)),
                  pl.BlockSpec((1, dilation, L, gw), lambda b, c: (b, 0, 0, kc)),
                  pl.BlockSpec((1, dilation, L, gw), lambda b, c: (b, 0, 0, vc))],
        out_specs=[pl.BlockSpec((1, tc, gw), lambda b, c: (b, c, 0)), stat, stat],
        out_shape=[jax.ShapeDtypeStruct((batch, seq, gw), BF16), stat_shape, stat_shape],
        scratch_shapes=[pltpu.VMEM((ATTN_HEADS_PER_GROUP, tc, ATTN_HEAD_DIM), F32),
                        pltpu.VMEM((tc, ATTN_HEAD_DIM), F32), pltpu.VMEM((tc, ATTN_HEAD_DIM), F32),
                        pltpu.VMEM((ATTN_HEADS_PER_GROUP, ATTN_WINDOW_OFFSETS, ATTN_QUERY_BLOCK, ATTN_KEY_WINDOW), F32),
                        pltpu.VMEM((ATTN_BLOCKS_PER_ITER * ATTN_HEADS_PER_GROUP, ATTN_QUERY_BLOCK, ATTN_KEY_WINDOW), F32),
                        pltpu.VMEM((ATTN_BLOCKS_PER_ITER * ATTN_HEADS_PER_GROUP, ATTN_QUERY_BLOCK, ATTN_KEY_WINDOW), F32)],
        compiler_params=_params("arbitrary", "arbitrary"),
        name="attn_dil%d" % dilation,
    )(q_arr, k_arr, v_arr)
    T = batch * seq
    return o.reshape(T, gw), m.reshape(T, ATTN_HEAD_DIM), l.reshape(T, ATTN_HEAD_DIM)


def _running_sum_rows(x, row_index, reverse):
    n = x.shape[0]
    shift = 1
    while shift < n:
        if reverse:
            x = x + jnp.where(row_index < n - shift, pltpu.roll(x, n - shift, 0), 0.0)
        else:
            x = x + jnp.where(row_index >= shift, pltpu.roll(x, shift, 0), 0.0)
        shift *= 2
    return x


def _hgrn_kernel(qf_ref, vf_ref, lf_ref, kf_ref, qb_ref, vb_ref, lb_ref, kb_ref, g_ref, nw_ref, o_ref,
                 accf_ref, accb_ref, sf_ref, sb_ref):
    c = pl.program_id(2)
    nc = pl.num_programs(2)
    chunk = qf_ref.shape[1]
    seq = o_ref.shape[1]
    hp = HGRN_HEADS_PER_STEP
    blk = HGRN_BLOCK
    half = blk // 2
    nblk = chunk // blk
    dh = HGRN_HEAD_DIM
    pw = 2 * dh

    @pl.when(c == 0)
    def _():
        sf_ref[...] = jnp.zeros_like(sf_ref)
        sb_ref[...] = jnp.zeros_like(sb_ref)

    row_index = lax.broadcasted_iota(jnp.int32, (blk, pw), 0)
    ti = lax.broadcasted_iota(jnp.int32, (blk, 2 * blk), 0)
    si = lax.broadcasted_iota(jnp.int32, (blk, 2 * blk), 1) % blk
    causal = si <= ti
    anticausal = si >= ti
    zeros_kv = jnp.zeros((blk, dh), BF16)
    zeros_st = jnp.zeros((dh, dh), BF16)

    def block_diag(a, b, z):
        return jnp.concatenate([jnp.concatenate([a, z], axis=1), jnp.concatenate([z, b], axis=1)], axis=0)

    def load_unit(pair, r0, q_ref, v_ref, logf_ref, key_ref, s_ref):
        lanes = slice(pair * pw, (pair + 1) * pw)
        rows = pl.ds(r0, blk)
        return (q_ref[0, rows, lanes], v_ref[0, rows, lanes], logf_ref[0, rows, lanes], key_ref[0, rows, lanes],
                s_ref[2 * pair], s_ref[2 * pair + 1])

    def prepare_unit(loaded, backward):
        q, v, logf, k, st0, st1 = loaded
        a = _running_sum_rows(logf.astype(F32), row_index, backward)
        if backward:
            total = a[0:1, :]
            mid = a[half:half + 1, :]
        else:
            total = a[blk - 1:blk, :]
            mid = a[half - 1:half, :]
        q_mid = q * jnp.exp(a - mid).astype(BF16)
        k_mid_b = k * jnp.exp(mid - a).astype(BF16)
        q_in = q_mid * jnp.exp(mid).astype(BF16)
        k_out = k_mid_b * jnp.exp(total - mid).astype(BF16)
        decay = jnp.broadcast_to(jnp.exp(total), (SUBLANES, pw)).T[:, 0:1]
        return dict(q_mid=q_mid, k_diag=block_diag(k_mid_b[:, :dh], k_mid_b[:, dh:], zeros_kv),
                    q_in=q_in, s_diag=block_diag(st0.astype(BF16), st1.astype(BF16), zeros_st),
                    v=v, v_diag=block_diag(v[:, :dh], v[:, dh:], zeros_kv), k_out=k_out,
                    decay=decay, states=(st0, st1), mask=anticausal if backward else causal)

    nt = (((1,), (1,)), ((), ()))
    tn = (((0,), (0,)), ((), ()))

    def body(n, carry):
        rf = pl.multiple_of(n * blk, blk)
        rb = pl.multiple_of((nblk - 1 - n) * blk, blk)
        units = []
        for pair in range(hp // 2):
            units.append((pair, False, load_unit(pair, rf, qf_ref, vf_ref, lf_ref, kf_ref, sf_ref)))
            units.append((pair, True, load_unit(pair, rb, qb_ref, vb_ref, lb_ref, kb_ref, sb_ref)))
        prepared = [prepare_unit(loaded, backward) for _, backward, loaded in units]
        att = [lax.dot_general(u["q_mid"], u["k_diag"], nt, preferred_element_type=F32) for u in prepared]
        inter = [jnp.dot(u["q_in"], u["s_diag"], preferred_element_type=F32) for u in prepared]
        upd = [[lax.dot_general(u["k_out"][:, j * dh:(j + 1) * dh], u["v"][:, j * dh:(j + 1) * dh], tn,
                                preferred_element_type=F32) for j in range(2)] for u in prepared]
        att = [jnp.where(u["mask"], s, 0.0).astype(BF16) for u, s in zip(prepared, att)]
        intra = [jnp.dot(s, u["v_diag"], preferred_element_type=F32) for u, s in zip(prepared, att)]
        for i, (pair, backward, _) in enumerate(units):
            u = prepared[i]
            lanes = slice(pair * pw, (pair + 1) * pw)
            o = intra[i] + inter[i]
            states = [st * u["decay"][j * dh:(j + 1) * dh, :] + upd[i][j] for j, st in enumerate(u["states"])]
            if backward:
                accb_ref[pl.ds(pl.multiple_of((nc - 1 - c) * chunk + rb, blk), blk), lanes] = o
                sb_ref[2 * pair], sb_ref[2 * pair + 1] = states
            else:
                accf_ref[pl.ds(pl.multiple_of(c * chunk + rf, blk), blk), lanes] = o
                sf_ref[2 * pair], sf_ref[2 * pair + 1] = states
        return carry

    lax.fori_loop(0, nblk, body, 0, unroll=True)

    @pl.when(c == nc - 1)
    def _():
        fin = min(seq, HGRN_FINISH_ROWS)

        def finish(t, carry):
            rows = pl.ds(pl.multiple_of(t * fin, fin), fin)
            for h in range(hp):
                lanes = slice(h * dh, (h + 1) * dh)
                o = accf_ref[rows, lanes] + accb_ref[rows, lanes]
                o = o * lax.rsqrt(jnp.mean(o * o, axis=-1, keepdims=True) + RMS_EPS)
                o = o * nw_ref[:, lanes] * g_ref[0, rows, lanes].astype(F32)
                o_ref[0, rows, lanes] = o.astype(o_ref.dtype)
            return carry

        lax.fori_loop(0, seq // fin, finish, 0, unroll=4)


def _hgrn(qg3, plain3, logf3, key3, norm_w):
    batch, seq, _ = qg3.shape
    hp = HGRN_HEADS_PER_STEP
    w = hp * HGRN_HEAD_DIM
    chunk = min(seq, HGRN_CHUNK)
    nc = seq // chunk

    def fwd(offset):
        base = offset // w
        return pl.BlockSpec((1, chunk, w), lambda b, h, c: (b, c, base + h))

    def bwd(offset):
        base = offset // w
        return pl.BlockSpec((1, chunk, w), lambda b, h, c: (b, nc - 1 - c, base + h))

    gbase = HGRN_WIDTH // w
    return pl.pallas_call(
        _hgrn_kernel,
        grid=(batch, HGRN_HEADS // hp, nc),
        in_specs=[fwd(0), fwd(PLAIN_HI), fwd(0), fwd(0),
                  bwd(0), bwd(PLAIN_HI), bwd(HGRN_WIDTH), bwd(HGRN_WIDTH),
                  pl.BlockSpec((1, seq, w), lambda b, h, c: (b, 0, gbase + h)),
                  pl.BlockSpec((1, w), lambda b, h, c: (0, h))],
        out_specs=pl.BlockSpec((1, seq, w), lambda b, h, c: (b, 0, h)),
        out_shape=jax.ShapeDtypeStruct((batch, seq, HGRN_WIDTH), BF16),
        scratch_shapes=[pltpu.VMEM((seq, w), F32), pltpu.VMEM((seq, w), F32),
                        pltpu.VMEM((hp, HGRN_HEAD_DIM, HGRN_HEAD_DIM), F32),
                        pltpu.VMEM((hp, HGRN_HEAD_DIM, HGRN_HEAD_DIM), F32)],
        compiler_params=_params("parallel", "parallel", "arbitrary"),
        name="hgrn",
    )(qg3, plain3, logf3, key3, qg3, plain3, logf3, key3, qg3, norm_w)


def _pool_kernel(prev_ref, cur_ref, next_ref, pw_ref, ps_ref, o_ref, *, seq_len):
    i = pl.program_id(1)
    last = pl.num_programs(1) - 1
    ts = cur_ref.shape[1]
    n = ts + 2 * POOL_HALO
    pos = i * ts + lax.broadcasted_iota(jnp.int32, (ts, 1), 0)
    for g in range(POOL_GROUPS):
        lanes = slice(g * POOL_GROUP_DIM, (g + 1) * POOL_GROUP_DIM)
        half = POOL_WINDOWS[g] // 2
        cur = cur_ref[0, :, lanes].astype(F32)
        prev = jnp.where(i > 0, prev_ref[0, :, lanes].astype(F32), 0.0)
        nxt = jnp.where(i < last, next_ref[0, :, lanes].astype(F32), 0.0)
        ext = jnp.concatenate([prev, cur, nxt], axis=0)
        w = ext + pltpu.roll(ext, 1, 0)
        step = 1
        while step < half:
            w = pltpu.roll(w, step, 0) + pltpu.roll(w, n - step, 0)
            step *= 2
        wsum = w[POOL_HALO:POOL_HALO + ts]
        count = (jnp.minimum(pos + half, seq_len) - jnp.maximum(pos - half, 0)).astype(F32)
        mixed = wsum / count - cur
        y = jnp.dot(mixed.astype(BF16), pw_ref[g], preferred_element_type=F32)
        o_ref[0, :, lanes] = (y * ps_ref[:, lanes]).astype(o_ref.dtype)


def _pool(plain3, pool_w, layer, pool_scale):
    batch, seq, _ = plain3.shape
    ts = min(seq, POOL_ROWS)
    hb = ts // POOL_HALO
    nhalo = seq // POOL_HALO
    kern = functools.partial(_pool_kernel, seq_len=seq)
    return pl.pallas_call(
        kern,
        grid=(batch, seq // ts),
        in_specs=[pl.BlockSpec((1, POOL_HALO, D_MODEL), lambda b, i: (b, jnp.maximum(i * hb - 1, 0), 0)),
                  pl.BlockSpec((1, ts, D_MODEL), lambda b, i: (b, i, 0)),
                  pl.BlockSpec((1, POOL_HALO, D_MODEL), lambda b, i: (b, jnp.minimum((i + 1) * hb, nhalo - 1), 0)),
                  pl.BlockSpec((None, POOL_GROUPS, POOL_GROUP_DIM, POOL_GROUP_DIM), lambda b, i: (layer, 0, 0, 0)),
                  pl.BlockSpec((1, D_MODEL), lambda b, i: (0, 0))],
        out_specs=pl.BlockSpec((1, ts, D_MODEL), lambda b, i: (b, i, 0)),
        out_shape=jax.ShapeDtypeStruct((batch, seq, D_MODEL), BF16),
        compiler_params=_params("parallel", "parallel"),
        name="pool",
    )(plain3, plain3, plain3, pool_w, pool_scale)


def _merge_kernel(ya_ref, o1_ref, o2_ref, o3_ref, m1_ref, m2_ref, m3_ref, l1_ref, l2_ref, l3_ref, hc_ref,
                  xin_ref, x_ref, wa_ref, wb_ref, wc_ref, wo_ref, *rest, alpha, nsub):
    wgate_refs = rest[:-4]
    g_ref, b_ref, xo_ref, xb_ref = rest[-4:]
    d = D_MODEL
    dh = ATTN_HEAD_DIM
    sub = x_ref.shape[0] // nsub
    rows = [slice(s * sub, (s + 1) * sub) for s in range(nsub)]
    xin = [xin_ref[r, :].astype(BF16) for r in rows]
    y_a = [jnp.dot(ya_ref[r, :], wa_ref[...], preferred_element_type=F32) for r in rows]
    y_c = [jnp.dot(hc_ref[r, :], wc_ref[...], preferred_element_type=F32) for r in rows]
    per_gate = d // W_IN_BLOCK
    gates = [[jnp.concatenate([jnp.dot(x, w_ref[...], preferred_element_type=F32)
                               for w_ref in wgate_refs[n * per_gate:(n + 1) * per_gate]], axis=1)
              for n in range(3)] for x in xin]
    att = []
    for r in rows:
        ms = [m1_ref[r, :], m2_ref[r, :], m3_ref[r, :]]
        ls = [l1_ref[r, :], l2_ref[r, :], l3_ref[r, :]]
        top = jnp.maximum(jnp.maximum(ms[0], ms[1]), ms[2])
        es = [jnp.exp2(m - top) for m in ms]
        inv = 1.0 / (es[0] * ls[0] + es[1] * ls[1] + es[2] * ls[2])
        ws = [e * inv for e in es]
        heads = []
        for h in range(ATTN_HEADS_PER_GROUP):
            lanes = slice(h * dh, (h + 1) * dh)
            stat = slice(h * ATTN_STAT_LANES, h * ATTN_STAT_LANES + 1)
            acc = None
            for w, o_ref in zip(ws, (o1_ref, o2_ref, o3_ref)):
                term = jnp.broadcast_to(w[:, stat], (sub, dh)) * o_ref[r, lanes].astype(F32)
                acc = term if acc is None else acc + term
            heads.append(acc)
        att.append(jnp.concatenate(heads, axis=1).astype(BF16))
    y_b = [jnp.dot(a, wb_ref[...], preferred_element_type=F32) for a in att]
    merged = [(_sigmoid(gates[s][0]) * y_a[s] + _sigmoid(gates[s][1]) * y_b[s]
               + _sigmoid(gates[s][2]) * y_c[s]).astype(BF16) for s in range(nsub)]
    mix = [jnp.dot(mg, wo_ref[...], preferred_element_type=F32) for mg in merged]
    for s, r in enumerate(rows):
        x1 = _layer_norm(alpha * x_ref[r, :] + mix[s], g_ref[...], b_ref[...])
        xo_ref[r, :] = x1
        xb_ref[r, :] = x1.astype(BF16)


def _merge(y_pool, o_list, m_list, l_list, y_hgrn, x_bf, x, wa, wb, wc, wo, w_in_bf, layer, ln_g, ln_b, alpha):
    T = x.shape[0]
    tm = min(T, MERGE_ROWS)
    gw = ATTN_GROUP_WIDTH
    d = D_MODEL
    row = lambda w: pl.BlockSpec((tm, w), lambda i: (i, 0))
    const = lambda a: pl.BlockSpec(a.shape, lambda i: (0,) * a.ndim, pipeline_mode=pl.Buffered(1))
    kern = functools.partial(_merge_kernel, alpha=alpha, nsub=MERGE_SUBTILES)
    stat = row(ATTN_HEAD_DIM)
    gate_blocks = range(GATES_COLUMN // W_IN_BLOCK, (GATES_COLUMN + 3 * d) // W_IN_BLOCK)
    gate_specs = [pl.BlockSpec((None, d, W_IN_BLOCK), lambda i, blk=blk: (layer, 0, blk),
                               pipeline_mode=pl.Buffered(1)) for blk in gate_blocks]
    return pl.pallas_call(
        kern,
        grid=(T // tm,),
        in_specs=[row(d), row(gw), row(gw), row(gw), stat, stat, stat, stat, stat, stat, row(d), row(d),
                  row(d)] + [_layer_spec(w, layer) for w in (wa, wb, wc, wo)] + gate_specs
                 + [const(ln_g), const(ln_b)],
        out_specs=[row(d), row(d)],
        out_shape=[jax.ShapeDtypeStruct((T, d), F32), jax.ShapeDtypeStruct((T, d), BF16)],
        compiler_params=_params("parallel"),
        name="merge",
    )(y_pool, *o_list, *m_list, *l_list, y_hgrn, x_bf, x, wa, wb, wc, wo, *([w_in_bf] * len(gate_specs)), ln_g, ln_b)


def _ffn_kernel(xb_ref, x_ref, p_ref, wg_ref, wu_ref, wd_ref, wpp_ref, wpg_ref, g_ref, b_ref,
                xo_ref, xbo_ref, *, alpha, nsub):
    sub = xb_ref.shape[0] // nsub
    rows = [slice(s * sub, (s + 1) * sub) for s in range(nsub)]
    xb = [xb_ref[r, :] for r in rows]
    gate = [jnp.dot(x, wg_ref[...], preferred_element_type=F32) for x in xb]
    up = [jnp.dot(x, wu_ref[...], preferred_element_type=F32) for x in xb]
    pgate = [jnp.dot(x, wpg_ref[...], preferred_element_type=F32) for x in xb]
    ple = [jnp.dot(p_ref[r, :].astype(BF16), wpp_ref[...], preferred_element_type=F32) for r in rows]
    hidden = [(g * _sigmoid(g) * u).astype(BF16) for g, u in zip(gate, up)]
    ffn = [jnp.dot(h, wd_ref[...], preferred_element_type=F32) for h in hidden]
    for s, r in enumerate(rows):
        x2 = _layer_norm(alpha * x_ref[r, :] + ffn[s] + ple[s] * _sigmoid(pgate[s]), g_ref[...], b_ref[...])
        xo_ref[r, :] = x2
        xbo_ref[r, :] = x2.astype(BF16)


def _ffn(x1_bf, x1, p, wg, wu, wd, wpp, wpg, layer, ln_g, ln_b, alpha):
    T = x1.shape[0]
    tm = min(T, FFN_ROWS)
    d = D_MODEL
    row = lambda w: pl.BlockSpec((tm, w), lambda i: (i, 0))
    const = lambda a: pl.BlockSpec(a.shape, lambda i: (0,) * a.ndim, pipeline_mode=pl.Buffered(1))
    weight = lambda a: _layer_spec(a, layer)
    kern = functools.partial(_ffn_kernel, alpha=alpha, nsub=FFN_SUBTILES)
    return pl.pallas_call(
        kern,
        grid=(T // tm,),
        in_specs=[row(d), row(d), pl.BlockSpec((None, tm, PLE_DIM), lambda i: (layer, i, 0)),
                  weight(wg), weight(wu), weight(wd), weight(wpp), weight(wpg), const(ln_g), const(ln_b)],
        out_specs=[row(d), row(d)],
        out_shape=[jax.ShapeDtypeStruct((T, d), F32), jax.ShapeDtypeStruct((T, d), BF16)],
        compiler_params=_params("parallel"),
        name="ffn",
    )(x1_bf, x1, p, wg, wu, wd, wpp, wpg, ln_g, ln_b)


def kernel(x, p, w_in, pool_w, pool_scale, w_branch_a, w_branch_b, w_branch_c, hgrn_lb_logits, hgrn_norm_w, w_out, ln1_g, ln1_b, w_ffn_gate, w_ffn_up, w_ffn_down, w_ple_proj, w_ple_gate, ln2_g, ln2_b):
    batch, seq, d = x.shape
    depth = w_in.shape[0]
    T = batch * seq
    alpha = float((2 * depth) ** 0.25)

    lb = jnp.cumsum(jax.nn.softmax(hgrn_lb_logits.astype(F32), axis=0), axis=0)
    lb = lb - lb[:1]

    w_in_bf = w_in.astype(BF16)
    bf = lambda a: a.astype(BF16)
    pool_w_b, wa, wb, wc, wo = bf(pool_w), bf(w_branch_a), bf(w_branch_b), bf(w_branch_c), bf(w_out)
    wg, wu, wd, wpp, wpg = bf(w_ffn_gate), bf(w_ffn_up), bf(w_ffn_down), bf(w_ple_proj), bf(w_ple_gate)

    slopes = [2.0 ** (-ALIBI_MAX_BIAS * (i + 1) / ATTN_HEADS) for i in range(ATTN_HEADS)]
    hpg = ATTN_HEADS_PER_GROUP
    gw = ATTN_GROUP_WIDTH

    xf = x.reshape(T, d).astype(F32)
    xb = xf
    p_rows = p.reshape(depth, T, PLE_DIM)
    for i in range(depth):
        qg, logf, key, plain = _inproj(xb, w_in_bf, i, lb[i:i + 1])
        plain3 = plain.reshape(batch, seq, PLAIN_WIDTH)
        o_list, m_list, l_list = [], [], []
        for g, (_, dil) in enumerate(ATTN_GROUPS):
            gs = tuple(slopes[g * hpg:(g + 1) * hpg])
            if dil == 1:
                qkv = plain.reshape(batch, 1, seq, PLAIN_WIDTH)
                cols = (PLAIN_AQ // gw, PLAIN_AK // gw, PLAIN_AV // gw)
            else:
                blocks = [(ATTN_COLUMN + part * ATTN_HEADS * ATTN_HEAD_DIM) // W_IN_BLOCK + g for part in range(3)]
                qkv = _inproj_dilated(xf, w_in_bf, i, blocks, batch, seq, dil)
                cols = (0, 1, 2)
            o, m, l = _attention_group(qkv, qkv, qkv, cols, batch, seq, dil, gs)
            o_list.append(o)
            m_list.append(m)
            l_list.append(l)
        y_hgrn = _hgrn(qg.reshape(batch, seq, 2 * HGRN_WIDTH), plain3,
                       logf.reshape(batch, seq, 2 * HGRN_WIDTH), key.reshape(batch, seq, 2 * HGRN_WIDTH),
                       hgrn_norm_w[i:i + 1])
        y_pool = _pool(plain3, pool_w_b, i, pool_scale[i:i + 1])
        x1, x1b = _merge(y_pool.reshape(T, d), o_list, m_list, l_list, y_hgrn.reshape(T, d), xb, xf,
                         wa, wb, wc, wo, w_in_bf, i, ln1_g[i:i + 1], ln1_b[i:i + 1], alpha)
        xf, xb = _ffn(x1b, x1, p_rows, wg, wu, wd, wpp, wpg, i, ln2_g[i:i + 1], ln2_b[i:i + 1], alpha)
    return xf.reshape(batch, seq, d).astype(x.dtype)
```

```python
import functools

import jax
import jax.numpy as jnp
from jax import lax
from jax.experimental import pallas as pl
from jax.experimental.pallas import tpu as pltpu

F32 = jnp.float32
BF16 = jnp.bfloat16
LANES = 128
SUBLANES = 8
LOG2_E = 1.4426950408889634

D_MODEL = 1024
PLE_DIM = 256
POOL_WINDOWS = (2, 4, 8, 16)
POOL_GROUPS = 4
POOL_GROUP_DIM = D_MODEL // POOL_GROUPS
ATTN_GROUPS = ((128, 1), (512, 4), (2048, 16))
ATTN_HEADS_PER_GROUP = 4
ATTN_HEADS = ATTN_HEADS_PER_GROUP * len(ATTN_GROUPS)
ATTN_HEAD_DIM = 128
ATTN_GROUP_WIDTH = ATTN_HEADS_PER_GROUP * ATTN_HEAD_DIM
ATTN_SIDE = 64
ALIBI_MAX_BIAS = 8.0
NEG_INF = -1e30
HGRN_HEADS = 8
HGRN_HEAD_DIM = 128
HGRN_WIDTH = HGRN_HEADS * HGRN_HEAD_DIM
LN_EPS = 1e-5
RMS_EPS = 1e-6

W_IN_BLOCK = 512
ATTN_COLUMN = D_MODEL
HGRN_COLUMN = ATTN_COLUMN + 3 * ATTN_HEADS * ATTN_HEAD_DIM
GATES_COLUMN = HGRN_COLUMN + 5 * HGRN_WIDTH
PLAIN_HI = 1024
PLAIN_AQ = 2048
PLAIN_AK = 2560
PLAIN_AV = 3072
PLAIN_WIDTH = 3584

INPROJ_ROWS = 256
INPROJ_BLOCKS = (4, 4, 7)
DILATED_ROWS = 1024
ATTN_QUERY_BLOCK = 128
ATTN_KEY_WINDOW = ATTN_QUERY_BLOCK + 2 * ATTN_SIDE
ATTN_TOKENS_PER_STEP = 2048
ATTN_BLOCKS_PER_ITER = 2
ATTN_STAT_LANES = 32
ATTN_WINDOW_OFFSETS = 3
HGRN_BLOCK = 64
HGRN_HEADS_PER_STEP = 4
HGRN_FINISH_ROWS = 256
HGRN_CHUNK = 512
MERGE_ROWS = 512
MERGE_SUBTILES = 2
FFN_ROWS = 512
FFN_SUBTILES = 2
POOL_ROWS = 512
POOL_HALO = 16
VMEM_LIMIT = 52 * 1024 * 1024


def _params(*semantics):
    return pltpu.CompilerParams(dimension_semantics=semantics, vmem_limit_bytes=VMEM_LIMIT)


def _layer_spec(stacked, layer):
    tail = stacked.shape[1:]
    return pl.BlockSpec((None,) + tail, lambda i: (layer,) + (0,) * len(tail), pipeline_mode=pl.Buffered(1))


def _sigmoid(x):
    return 0.5 * jnp.tanh(0.5 * x) + 0.5


def _layer_norm(h, g, b):
    mu = jnp.mean(h, axis=-1, keepdims=True)
    d = h - mu
    var = jnp.mean(d * d, axis=-1, keepdims=True)
    return d * lax.rsqrt(var + LN_EPS) * g + b


def _inproj_kernel(x_ref, *refs):
    n_act, n_forget, n_plain = INPROJ_BLOCKS
    w_refs = refs[:n_act + n_forget + n_plain]
    lb_ref, act_ref, logf_ref, key_ref, plain_ref = refs[n_act + n_forget + n_plain:]
    x = x_ref[...].astype(BF16)
    wb = W_IN_BLOCK
    z = [jnp.dot(x, w_ref[...], preferred_element_type=F32) for w_ref in w_refs[:n_act + n_forget]]
    for n, w_ref in enumerate(w_refs[n_act + n_forget:]):
        plain_ref[:, n * wb:(n + 1) * wb] = jnp.dot(x, w_ref[...], preferred_element_type=F32).astype(plain_ref.dtype)
    for n in range(n_act):
        act_ref[:, n * wb:(n + 1) * wb] = (z[n] * _sigmoid(z[n])).astype(act_ref.dtype)
    for n in range(n_forget):
        cols = slice(n * wb, (n + 1) * wb)
        lb = lb_ref[:, cols]
        sig = _sigmoid(z[n_act + n])
        logf_ref[:, cols] = jnp.log(lb + (1.0 - lb) * sig).astype(logf_ref.dtype)
        key_ref[:, cols] = ((1.0 - lb) * (1.0 - sig)).astype(key_ref.dtype)


def _inproj(x, w_in_bf, layer, lb):
    T, D = x.shape
    tm = min(T, INPROJ_ROWS)
    wb = W_IN_BLOCK
    hw = HGRN_WIDTH
    span = lambda start, width: list(range(start // wb, (start + width) // wb))
    attn_part = ATTN_HEADS * ATTN_HEAD_DIM
    act_blocks = span(HGRN_COLUMN, hw) + span(HGRN_COLUMN + 4 * hw, hw)
    forget_blocks = span(HGRN_COLUMN + 2 * hw, 2 * hw)
    plain_blocks = (span(0, D_MODEL) + span(HGRN_COLUMN + hw, hw)
                    + [(ATTN_COLUMN + part * attn_part) // wb for part in range(3)])
    blocks = act_blocks + forget_blocks + plain_blocks
    assert (len(act_blocks), len(forget_blocks), len(plain_blocks)) == INPROJ_BLOCKS
    wspec = lambda blk: pl.BlockSpec((None, D, wb), lambda i: (layer, 0, blk), pipeline_mode=pl.Buffered(1))
    ospec = lambda n: pl.BlockSpec((tm, n * wb), lambda i: (i, 0))
    oshape = lambda n: jax.ShapeDtypeStruct((T, n * wb), BF16)
    n_act, n_forget, n_plain = INPROJ_BLOCKS
    return pl.pallas_call(
        _inproj_kernel,
        grid=(T // tm,),
        in_specs=[pl.BlockSpec((tm, D), lambda i: (i, 0))] + [wspec(blk) for blk in blocks]
                 + [pl.BlockSpec((1, n_forget * wb), lambda i: (0, 0))],
        out_specs=[ospec(n_act), ospec(n_forget), ospec(n_forget), ospec(n_plain)],
        out_shape=[oshape(n_act), oshape(n_forget), oshape(n_forget), oshape(n_plain)],
        compiler_params=_params("parallel"),
        name="inproj",
    )(x, *([w_in_bf] * len(blocks)), lb)


def _inproj_dilated_kernel(*refs, dilation):
    nslab = D_MODEL // LANES
    x_refs = refs[:nslab]
    wq_ref, wk_ref, wv_ref, o_ref, xp_ref = refs[nslab:]
    n = x_refs[0].shape[0] // dilation
    for r in range(dilation):
        for c, x_ref in enumerate(x_refs):
            xp_ref[r * n:(r + 1) * n, c * LANES:(c + 1) * LANES] = (
                x_ref[pl.ds(r, n, stride=dilation), :].astype(BF16))
    xp = xp_ref[...]
    for part, w_ref in enumerate((wq_ref, wk_ref, wv_ref)):
        res = jnp.dot(xp, w_ref[...], preferred_element_type=F32).astype(o_ref.dtype)
        cols = slice(part * ATTN_GROUP_WIDTH, (part + 1) * ATTN_GROUP_WIDTH)
        for r in range(dilation):
            o_ref[0, r, :, cols] = res[r * n:(r + 1) * n]


def _inproj_dilated(x_f32, w_in_bf, layer, blocks, batch, seq, dilation):
    gw = ATTN_GROUP_WIDTH
    tb = min(seq, DILATED_ROWS)
    steps = seq // tb
    n = tb // dilation
    nslab = D_MODEL // LANES
    wspec = lambda blk: pl.BlockSpec((None, D_MODEL, gw), lambda i: (layer, 0, blk))
    xspec = lambda c: pl.BlockSpec((tb, LANES), lambda i: (i, c))
    return pl.pallas_call(
        functools.partial(_inproj_dilated_kernel, dilation=dilation),
        grid=(batch * steps,),
        in_specs=[xspec(c) for c in range(nslab)] + [wspec(blk) for blk in blocks],
        out_specs=pl.BlockSpec((1, dilation, n, 3 * gw), lambda i: (i // steps, 0, i % steps, 0)),
        out_shape=jax.ShapeDtypeStruct((batch, dilation, seq // dilation, 3 * gw), BF16),
        scratch_shapes=[pltpu.VMEM((tb, D_MODEL), BF16)],
        compiler_params=_params("parallel"),
        name="inproj_dilated%d" % dilation,
    )(*([x_f32] * nslab), w_in_bf, w_in_bf, w_in_bf)


def _attn_kernel(q_ref, k_ref, v_ref, o_ref, m_ref, l_ref, oacc_ref, macc_ref, lacc_ref, bias_ref, sa_ref, sb_ref, *,
                 seq_len, dilation, slopes):
    c = pl.program_id(1)
    lq = q_ref.shape[2]
    qb, kw = ATTN_QUERY_BLOCK, ATTN_KEY_WINDOW
    nq = lq // qb
    per_iter = ATTN_BLOCKS_PER_ITER
    scale2 = ATTN_HEAD_DIM ** -0.5 * LOG2_E
    heads = range(ATTN_HEADS_PER_GROUP)
    lanes = [slice(h * ATTN_HEAD_DIM, (h + 1) * ATTN_HEAD_DIM) for h in heads]

    @pl.when((pl.program_id(0) == 0) & (c == 0))
    def _():
        row = lax.broadcasted_iota(jnp.int32, (qb, kw), 0)
        col = lax.broadcasted_iota(jnp.int32, (qb, kw), 1)
        for w in range(ATTN_WINDOW_OFFSETS):
            dist = jnp.abs(col - row - w * ATTN_SIDE)
            distf = dist.astype(F32) * float(dilation)
            for h in heads:
                bias_ref[h, w] = jnp.where(dist <= ATTN_SIDE, distf * (-slopes[h] * LOG2_E), NEG_INF * LOG2_E)

    stat_group = lax.broadcasted_iota(jnp.int32, (kw, ATTN_HEAD_DIM), 1) // ATTN_STAT_LANES
    ones_cols = [jnp.where(stat_group == h, 1.0, 0.0).astype(BF16) for h in heads]
    out_group = lax.broadcasted_iota(jnp.int32, (qb, ATTN_HEAD_DIM), 1) // ATTN_STAT_LANES

    def block_coords(idx):
        r = idx // nq
        r0 = pl.multiple_of((idx % nq) * qb, qb)
        t0 = c * lq + r0
        ks = pl.multiple_of(jnp.clip(t0 - ATTN_SIDE, 0, seq_len - kw), ATTN_SIDE)
        return r, r0, t0, ks

    def score_products(it, s_ref):
        for u in range(per_iter):
            r, r0, _, ks = block_coords(it * per_iter + u)
            q = q_ref[0, r, pl.ds(r0, qb), :]
            k = k_ref[0, r, pl.ds(ks, kw), :]
            for h in heads:
                s_ref[u * len(heads) + h] = lax.dot_general(
                    q[:, lanes[h]], k[:, lanes[h]], (((1,), (1,)), ((), ())), preferred_element_type=F32)

    n_iter = dilation * nq // per_iter
    score_products(0, sa_ref)

    def half_body(it, s_ref, next_ref):
        score_products(jnp.minimum(it + 1, n_iter - 1), next_ref)
        blocks = []
        for u in range(per_iter):
            r, r0, t0, ks = block_coords(it * per_iter + u)
            if dilation == 1:
                out_rows = pl.ds(r0, qb)
            else:
                out_rows = pl.ds(r0 * dilation + r, qb, stride=dilation)
            blocks.append(((t0 - ks) // ATTN_SIDE, out_rows, v_ref[0, r, pl.ds(ks, kw), :]))
        probs = []
        for u, (w, _, _) in enumerate(blocks):
            block_probs = []
            for h in heads:
                s = s_ref[u * len(heads) + h] * scale2 + bias_ref[h, w]
                m = jnp.max(s, axis=1, keepdims=True)
                block_probs.append((jnp.exp2(s - m).astype(BF16), m))
            probs.append(block_probs)
        pv = [[jnp.dot(block_probs[h][0], jnp.concatenate([v[:, lanes[h]], ones_cols[h]], axis=1),
                       preferred_element_type=F32) for h in heads]
              for (_, _, v), block_probs in zip(blocks, probs)]
        for (_, out_rows, _), block_probs, block_pv in zip(blocks, probs, pv):
            m_all = jnp.broadcast_to(block_probs[-1][1], (qb, ATTN_HEAD_DIM))
            l_all = block_pv[-1][:, ATTN_HEAD_DIM:]
            for h in reversed(heads[:-1]):
                m_all = jnp.where(out_group == h, block_probs[h][1], m_all)
                l_all = l_all + block_pv[h][:, ATTN_HEAD_DIM:]
            for h in heads:
                oacc_ref[h, out_rows, :] = block_pv[h][:, :ATTN_HEAD_DIM]
            macc_ref[out_rows, :] = m_all
            lacc_ref[out_rows, :] = l_all

    def body(it2, carry):
        half_body(2 * it2, sa_ref, sb_ref)
        half_body(2 * it2 + 1, sb_ref, sa_ref)
        return carry

    lax.fori_loop(0, n_iter // 2, body, 0, unroll=True)
    for h in heads:
        o_ref[0, :, lanes[h]] = oacc_ref[h].astype(o_ref.dtype)
    m_ref[0] = macc_ref[...]
    l_ref[0] = lacc_ref[...]


def _attention_group(q_arr, k_arr, v_arr, cols, batch, seq, dilation, slopes):
    L = seq // dilation
    gw = ATTN_GROUP_WIDTH
    tc = min(seq, ATTN_TOKENS_PER_STEP)
    lq = tc // dilation
    kern = functools.partial(_attn_kernel, seq_len=L, dilation=dilation, slopes=slopes)
    qc, kc, vc = cols
    stat = pl.BlockSpec((1, tc, ATTN_HEAD_DIM), lambda b, c: (b, c, 0))
    stat_shape = jax.ShapeDtypeStruct((batch, seq, ATTN_HEAD_DIM), F32)
    o, m, l = pl.pallas_call(
        kern,
        grid=(batch, seq // tc),
        in_specs=[pl.BlockSpec((1, dilation, lq, gw), lambda b, c: (b, 0, c, qc)),
                  pl.BlockSpec((1, dilation, L, gw), lambda b, c: (b, 0, 0, kc)),
                  pl.BlockSpec((1, dilation, L, gw), lambda b, c: (b, 0, 0, vc))],
        out_specs=[pl.BlockSpec((1, tc, gw), lambda b, c: (b, c, 0)), stat, stat],
        out_shape=[jax.ShapeDtypeStruct((batch, seq, gw), BF16), stat_shape, stat_shape],
        scratch_shapes=[pltpu.VMEM((ATTN_HEADS_PER_GROUP, tc, ATTN_HEAD_DIM), F32),
                        pltpu.VMEM((tc, ATTN_HEAD_DIM), F32), pltpu.VMEM((tc, ATTN_HEAD_DIM), F32),
                        pltpu.VMEM((ATTN_HEADS_PER_GROUP, ATTN_WINDOW_OFFSETS, ATTN_QUERY_BLOCK, ATTN_KEY_WINDOW), F32),
                        pltpu.VMEM((ATTN_BLOCKS_PER_ITER * ATTN_HEADS_PER_GROUP, ATTN_QUERY_BLOCK, ATTN_KEY_WINDOW), F32),
                        pltpu.VMEM((ATTN_BLOCKS_PER_ITER * ATTN_HEADS_PER_GROUP, ATTN_QUERY_BLOCK, ATTN_KEY_WINDOW), F32)],
        compiler_params=_params("arbitrary", "arbitrary"),
        name="attn_dil%d" % dilation,
    )(q_arr, k_arr, v_arr)
    T = batch * seq
    return o.reshape(T, gw), m.reshape(T, ATTN_HEAD_DIM), l.reshape(T, ATTN_HEAD_DIM)


def _running_sum_rows(x, row_index, reverse):
    n = x.shape[0]
    shift = 1
    while shift < n:
        if reverse:
            x = x + jnp.where(row_index < n - shift, pltpu.roll(x, n - shift, 0), 0.0)
        else:
            x = x + jnp.where(row_index >= shift, pltpu.roll(x, shift, 0), 0.0)
        shift *= 2
    return x


def _hgrn_kernel(qf_ref, vf_ref, lf_ref, kf_ref, qb_ref, vb_ref, lb_ref, kb_ref, g_ref, nw_ref, o_ref,
                 accf_ref, accb_ref, sf_ref, sb_ref):
    c = pl.program_id(2)
    nc = pl.num_programs(2)
    chunk = qf_ref.shape[1]
    seq = o_ref.shape[1]
    hp = HGRN_HEADS_PER_STEP
    blk = HGRN_BLOCK
    half = blk // 2
    nblk = chunk // blk
    dh = HGRN_HEAD_DIM
    pw = 2 * dh

    @pl.when(c == 0)
    def _():
        sf_ref[...] = jnp.zeros_like(sf_ref)
        sb_ref[...] = jnp.zeros_like(sb_ref)

    row_index = lax.broadcasted_iota(jnp.int32, (blk, pw), 0)
    ti = lax.broadcasted_iota(jnp.int32, (blk, 2 * blk), 0)
    si = lax.broadcasted_iota(jnp.int32, (blk, 2 * blk), 1) % blk
    causal = si <= ti
    anticausal = si >= ti
    zeros_kv = jnp.zeros((blk, dh), BF16)
    zeros_st = jnp.zeros((dh, dh), BF16)

    def block_diag(a, b, z):
        return jnp.concatenate([jnp.concatenate([a, z], axis=1), jnp.concatenate([z, b], axis=1)], axis=0)

    def load_unit(pair, r0, q_ref, v_ref, logf_ref, key_ref, s_ref):
        lanes = slice(pair * pw, (pair + 1) * pw)
        rows = pl.ds(r0, blk)
        return (q_ref[0, rows, lanes], v_ref[0, rows, lanes], logf_ref[0, rows, lanes], key_ref[0, rows, lanes],
                s_ref[2 * pair], s_ref[2 * pair + 1])

    def prepare_unit(loaded, backward):
        q, v, logf, k, st0, st1 = loaded
        a = _running_sum_rows(logf.astype(F32), row_index, backward)
        if backward:
            total = a[0:1, :]
            mid = a[half:half + 1, :]
        else:
            total = a[blk - 1:blk, :]
            mid = a[half - 1:half, :]
        q_mid = q * jnp.exp(a - mid).astype(BF16)
        k_mid_b = k * jnp.exp(mid - a).astype(BF16)
        q_in = q_mid * jnp.exp(mid).astype(BF16)
        k_out = k_mid_b * jnp.exp(total - mid).astype(BF16)
        decay = jnp.broadcast_to(jnp.exp(total), (SUBLANES, pw)).T[:, 0:1]
        return dict(q_mid=q_mid, k_diag=block_diag(k_mid_b[:, :dh], k_mid_b[:, dh:], zeros_kv),
                    q_in=q_in, s_diag=block_diag(st0.astype(BF16), st1.astype(BF16), zeros_st),
                    v=v, v_diag=block_diag(v[:, :dh], v[:, dh:], zeros_kv), k_out=k_out,
                    decay=decay, states=(st0, st1), mask=anticausal if backward else causal)

    nt = (((1,), (1,)), ((), ()))
    tn = (((0,), (0,)), ((), ()))

    def body(n, carry):
        rf = pl.multiple_of(n * blk, blk)
        rb = pl.multiple_of((nblk - 1 - n) * blk, blk)
        units = []
        for pair in range(hp // 2):
            units.append((pair, False, load_unit(pair, rf, qf_ref, vf_ref, lf_ref, kf_ref, sf_ref)))
            units.append((pair, True, load_unit(pair, rb, qb_ref, vb_ref, lb_ref, kb_ref, sb_ref)))
        prepared = [prepare_unit(loaded, backward) for _, backward, loaded in units]
        att = [lax.dot_general(u["q_mid"], u["k_diag"], nt, preferred_element_type=F32) for u in prepared]
        inter = [jnp.dot(u["q_in"], u["s_diag"], preferred_element_type=F32) for u in prepared]
        upd = [[lax.dot_general(u["k_out"][:, j * dh:(j + 1) * dh], u["v"][:, j * dh:(j + 1) * dh], tn,
                                preferred_element_type=F32) for j in range(2)] for u in prepared]
        att = [jnp.where(u["mask"], s, 0.0).astype(BF16) for u, s in zip(prepared, att)]
        intra = [jnp.dot(s, u["v_diag"], preferred_element_type=F32) for u, s in zip(prepared, att)]
        for i, (pair, backward, _) in enumerate(units):
            u = prepared[i]
            lanes = slice(pair * pw, (pair + 1) * pw)
            o = intra[i] + inter[i]
            states = [st * u["decay"][j * dh:(j + 1) * dh, :] + upd[i][j] for j, st in enumerate(u["states"])]
            if backward:
                accb_ref[pl.ds(pl.multiple_of((nc - 1 - c) * chunk + rb, blk), blk), lanes] = o
                sb_ref[2 * pair], sb_ref[2 * pair + 1] = states
            else:
                accf_ref[pl.ds(pl.multiple_of(c * chunk + rf, blk), blk), lanes] = o
                sf_ref[2 * pair], sf_ref[2 * pair + 1] = states
        return carry

    lax.fori_loop(0, nblk, body, 0, unroll=True)

    @pl.when(c == nc - 1)
    def _():
        fin = min(seq, HGRN_FINISH_ROWS)

        def finish(t, carry):
            rows = pl.ds(pl.multiple_of(t * fin, fin), fin)
            for h in range(hp):
                lanes = slice(h * dh, (h + 1) * dh)
                o = accf_ref[rows, lanes] + accb_ref[rows, lanes]
                o = o * lax.rsqrt(jnp.mean(o * o, axis=-1, keepdims=True) + RMS_EPS)
                o = o * nw_ref[:, lanes] * g_ref[0, rows, lanes].astype(F32)
                o_ref[0, rows, lanes] = o.astype(o_ref.dtype)
            return carry

        lax.fori_loop(0, seq // fin, finish, 0, unroll=4)


def _hgrn(qg3, plain3, logf3, key3, norm_w):
    batch, seq, _ = qg3.shape
    hp = HGRN_HEADS_PER_STEP
    w = hp * HGRN_HEAD_DIM
    chunk = min(seq, HGRN_CHUNK)
    nc = seq // chunk

    def fwd(offset):
        base = offset // w
        return pl.BlockSpec((1, chunk, w), lambda b, h, c: (b, c, base + h))

    def bwd(offset):
        base = offset // w
        return pl.BlockSpec((1, chunk, w), lambda b, h, c: (b, nc - 1 - c, base + h))

    gbase = HGRN_WIDTH // w
    return pl.pallas_call(
        _hgrn_kernel,
        grid=(batch, HGRN_HEADS // hp, nc),
        in_specs=[fwd(0), fwd(PLAIN_HI), fwd(0), fwd(0),
                  bwd(0), bwd(PLAIN_HI), bwd(HGRN_WIDTH), bwd(HGRN_WIDTH),
                  pl.BlockSpec((1, seq, w), lambda b, h, c: (b, 0, gbase + h)),
                  pl.BlockSpec((1, w), lambda b, h, c: (0, h))],
        out_specs=pl.BlockSpec((1, seq, w), lambda b, h, c: (b, 0, h)),
        out_shape=jax.ShapeDtypeStruct((batch, seq, HGRN_WIDTH), BF16),
        scratch_shapes=[pltpu.VMEM((seq, w), F32), pltpu.VMEM((seq, w), F32),
                        pltpu.VMEM((hp, HGRN_HEAD_DIM, HGRN_HEAD_DIM), F32),
                        pltpu.VMEM((hp, HGRN_HEAD_DIM, HGRN_HEAD_DIM), F32)],
        compiler_params=_params("parallel", "parallel", "arbitrary"),
        name="hgrn",
    )(qg3, plain3, logf3, key3, qg3, plain3, logf3, key3, qg3, norm_w)


def _pool_kernel(prev_ref, cur_ref, next_ref, pw_ref, ps_ref, o_ref, *, seq_len):
    i = pl.program_id(1)
    last = pl.num_programs(1) - 1
    ts = cur_ref.shape[1]
    n = ts + 2 * POOL_HALO
    pos = i * ts + lax.broadcasted_iota(jnp.int32, (ts, 1), 0)
    for g in range(POOL_GROUPS):
        lanes = slice(g * POOL_GROUP_DIM, (g + 1) * POOL_GROUP_DIM)
        half = POOL_WINDOWS[g] // 2
        cur = cur_ref[0, :, lanes].astype(F32)
        prev = jnp.where(i > 0, prev_ref[0, :, lanes].astype(F32), 0.0)
        nxt = jnp.where(i < last, next_ref[0, :, lanes].astype(F32), 0.0)
        ext = jnp.concatenate([prev, cur, nxt], axis=0)
        w = ext + pltpu.roll(ext, 1, 0)
        step = 1
        while step < half:
            w = pltpu.roll(w, step, 0) + pltpu.roll(w, n - step, 0)
            step *= 2
        wsum = w[POOL_HALO:POOL_HALO + ts]
        count = (jnp.minimum(pos + half, seq_len) - jnp.maximum(pos - half, 0)).astype(F32)
        mixed = wsum / count - cur
        y = jnp.dot(mixed.astype(BF16), pw_ref[g], preferred_element_type=F32)
        o_ref[0, :, lanes] = (y * ps_ref[:, lanes]).astype(o_ref.dtype)


def _pool(plain3, pool_w, layer, pool_scale):
    batch, seq, _ = plain3.shape
    ts = min(seq, POOL_ROWS)
    hb = ts // POOL_HALO
    nhalo = seq // POOL_HALO
    kern = functools.partial(_pool_kernel, seq_len=seq)
    return pl.pallas_call(
        kern,
        grid=(batch, seq // ts),
        in_specs=[pl.BlockSpec((1, POOL_HALO, D_MODEL), lambda b, i: (b, jnp.maximum(i * hb - 1, 0), 0)),
                  pl.BlockSpec((1, ts, D_MODEL), lambda b, i: (b, i, 0)),
                  pl.BlockSpec((1, POOL_HALO, D_MODEL), lambda b, i: (b, jnp.minimum((i + 1) * hb, nhalo - 1), 0)),
                  pl.BlockSpec((None, POOL_GROUPS, POOL_GROUP_DIM, POOL_GROUP_DIM), lambda b, i: (layer, 0, 0, 0)),
                  pl.BlockSpec((1, D_MODEL), lambda b, i: (0, 0))],
        out_specs=pl.BlockSpec((1, ts, D_MODEL), lambda b, i: (b, i, 0)),
        out_shape=jax.ShapeDtypeStruct((batch, seq, D_MODEL), BF16),
        compiler_params=_params("parallel", "parallel"),
        name="pool",
    )(plain3, plain3, plain3, pool_w, pool_scale)


def _merge_kernel(ya_ref, o1_ref, o2_ref, o3_ref, m1_ref, m2_ref, m3_ref, l1_ref, l2_ref, l3_ref, hc_ref,
                  xin_ref, x_ref, wa_ref, wb_ref, wc_ref, wo_ref, *rest, alpha, nsub):
    wgate_refs = rest[:-4]
    g_ref, b_ref, xo_ref, xb_ref = rest[-4:]
    d = D_MODEL
    dh = ATTN_HEAD_DIM
    sub = x_ref.shape[0] // nsub
    rows = [slice(s * sub, (s + 1) * sub) for s in range(nsub)]
    xin = [xin_ref[r, :].astype(BF16) for r in rows]
    y_a = [jnp.dot(ya_ref[r, :], wa_ref[...], preferred_element_type=F32) for r in rows]
    y_c = [jnp.dot(hc_ref[r, :], wc_ref[...], preferred_element_type=F32) for r in rows]
    per_gate = d // W_IN_BLOCK
    gates = [[jnp.concatenate([jnp.dot(x, w_ref[...], preferred_element_type=F32)
                               for w_ref in wgate_refs[n * per_gate:(n + 1) * per_gate]], axis=1)
              for n in range(3)] for x in xin]
    att = []
    for r in rows:
        ms = [m1_ref[r, :], m2_ref[r, :], m3_ref[r, :]]
        ls = [l1_ref[r, :], l2_ref[r, :], l3_ref[r, :]]
        top = jnp.maximum(jnp.maximum(ms[0], ms[1]), ms[2])
        es = [jnp.exp2(m - top) for m in ms]
        inv = 1.0 / (es[0] * ls[0] + es[1] * ls[1] + es[2] * ls[2])
        ws = [e * inv for e in es]
        heads = []
        for h in range(ATTN_HEADS_PER_GROUP):
            lanes = slice(h * dh, (h + 1) * dh)
            stat = slice(h * ATTN_STAT_LANES, h * ATTN_STAT_LANES + 1)
            acc = None
            for w, o_ref in zip(ws, (o1_ref, o2_ref, o3_ref)):
                term = jnp.broadcast_to(w[:, stat], (sub, dh)) * o_ref[r, lanes].astype(F32)
                acc = term if acc is None else acc + term
            heads.append(acc)
        att.append(jnp.concatenate(heads, axis=1).astype(BF16))
    y_b = [jnp.dot(a, wb_ref[...], preferred_element_type=F32) for a in att]
    merged = [(_sigmoid(gates[s][0]) * y_a[s] + _sigmoid(gates[s][1]) * y_b[s]
               + _sigmoid(gates[s][2]) * y_c[s]).astype(BF16) for s in range(nsub)]
    mix = [jnp.dot(mg, wo_ref[...], preferred_element_type=F32) for mg in merged]
    for s, r in enumerate(rows):
        x1 = _layer_norm(alpha * x_ref[r, :] + mix[s], g_ref[...], b_ref[...])
        xo_ref[r, :] = x1
        xb_ref[r, :] = x1.astype(BF16)


def _merge(y_pool, o_list, m_list, l_list, y_hgrn, x_bf, x, wa, wb, wc, wo, w_in_bf, layer, ln_g, ln_b, alpha):
    T = x.shape[0]
    tm = min(T, MERGE_ROWS)
    gw = ATTN_GROUP_WIDTH
    d = D_MODEL
    row = lambda w: pl.BlockSpec((tm, w), lambda i: (i, 0))
    const = lambda a: pl.BlockSpec(a.shape, lambda i: (0,) * a.ndim, pipeline_mode=pl.Buffered(1))
    kern = functools.partial(_merge_kernel, alpha=alpha, nsub=MERGE_SUBTILES)
    stat = row(ATTN_HEAD_DIM)
    gate_blocks = range(GATES_COLUMN // W_IN_BLOCK, (GATES_COLUMN + 3 * d) // W_IN_BLOCK)
    gate_specs = [pl.BlockSpec((None, d, W_IN_BLOCK), lambda i, blk=blk: (layer, 0, blk),
                               pipeline_mode=pl.Buffered(1)) for blk in gate_blocks]
    return pl.pallas_call(
        kern,
        grid=(T // tm,),
        in_specs=[row(d), row(gw), row(gw), row(gw), stat, stat, stat, stat, stat, stat, row(d), row(d),
                  row(d)] + [_layer_spec(w, layer) for w in (wa, wb, wc, wo)] + gate_specs
                 + [const(ln_g), const(ln_b)],
        out_specs=[row(d), row(d)],
        out_shape=[jax.ShapeDtypeStruct((T, d), F32), jax.ShapeDtypeStruct((T, d), BF16)],
        compiler_params=_params("parallel"),
        name="merge",
    )(y_pool, *o_list, *m_list, *l_list, y_hgrn, x_bf, x, wa, wb, wc, wo, *([w_in_bf] * len(gate_specs)), ln_g, ln_b)


def _ffn_kernel(xb_ref, x_ref, p_ref, wg_ref, wu_ref, wd_ref, wpp_ref, wpg_ref, g_ref, b_ref,
                xo_ref, xbo_ref, *, alpha, nsub):
    sub = xb_ref.shape[0] // nsub
    rows = [slice(s * sub, (s + 1) * sub) for s in range(nsub)]
    xb = [xb_ref[r, :] for r in rows]
    gate = [jnp.dot(x, wg_ref[...], preferred_element_type=F32) for x in xb]
    up = [jnp.dot(x, wu_ref[...], preferred_element_type=F32) for x in xb]
    pgate = [jnp.dot(x, wpg_ref[...], preferred_element_type=F32) for x in xb]
    ple = [jnp.dot(p_ref[r, :].astype(BF16), wpp_ref[...], preferred_element_type=F32) for r in rows]
    hidden = [(g * _sigmoid(g) * u).astype(BF16) for g, u in zip(gate, up)]
    ffn = [jnp.dot(h, wd_ref[...], preferred_element_type=F32) for h in hidden]
    for s, r in enumerate(rows):
        x2 = _layer_norm(alpha * x_ref[r, :] + ffn[s] + ple[s] * _sigmoid(pgate[s]), g_ref[...], b_ref[...])
        xo_ref[r, :] = x2
        xbo_ref[r, :] = x2.astype(BF16)


def _ffn(x1_bf, x1, p, wg, wu, wd, wpp, wpg, layer, ln_g, ln_b, alpha):
    T = x1.shape[0]
    tm = min(T, FFN_ROWS)
    d = D_MODEL
    row = lambda w: pl.BlockSpec((tm, w), lambda i: (i, 0))
    const = lambda a: pl.BlockSpec(a.shape, lambda i: (0,) * a.ndim, pipeline_mode=pl.Buffered(1))
    weight = lambda a: _layer_spec(a, layer)
    kern = functools.partial(_ffn_kernel, alpha=alpha, nsub=FFN_SUBTILES)
    return pl.pallas_call(
        kern,
        grid=(T // tm,),
        in_specs=[row(d), row(d), pl.BlockSpec((None, tm, PLE_DIM), lambda i: (layer, i, 0)),
                  weight(wg), weight(wu), weight(wd), weight(wpp), weight(wpg), const(ln_g), const(ln_b)],
        out_specs=[row(d), row(d)],
        out_shape=[jax.ShapeDtypeStruct((T, d), F32), jax.ShapeDtypeStruct((T, d), BF16)],
        compiler_params=_params("parallel"),
        name="ffn",
    )(x1_bf, x1, p, wg, wu, wd, wpp, wpg, ln_g, ln_b)


def kernel(x, p, w_in, pool_w, pool_scale, w_branch_a, w_branch_b, w_branch_c, hgrn_lb_logits, hgrn_norm_w, w_out, ln1_g, ln1_b, w_ffn_gate, w_ffn_up, w_ffn_down, w_ple_proj, w_ple_gate, ln2_g, ln2_b):
    batch, seq, d = x.shape
    depth = w_in.shape[0]
    T = batch * seq
    alpha = float((2 * depth) ** 0.25)

    lb = jnp.cumsum(jax.nn.softmax(hgrn_lb_logits.astype(F32), axis=0), axis=0)
    lb = lb - lb[:1]

    w_in_bf = w_in.astype(BF16)
    bf = lambda a: a.astype(BF16)
    pool_w_b, wa, wb, wc, wo = bf(pool_w), bf(w_branch_a), bf(w_branch_b), bf(w_branch_c), bf(w_out)
    wg, wu, wd, wpp, wpg = bf(w_ffn_gate), bf(w_ffn_up), bf(w_ffn_down), bf(w_ple_proj), bf(w_ple_gate)

    slopes = [2.0 ** (-ALIBI_MAX_BIAS * (i + 1) / ATTN_HEADS) for i in range(ATTN_HEADS)]
    hpg = ATTN_HEADS_PER_GROUP
    gw = ATTN_GROUP_WIDTH

    xf = x.reshape(T, d).astype(F32)
    xb = xf
    p_rows = p.reshape(depth, T, PLE_DIM)
    for i in range(depth):
        qg, logf, key, plain = _inproj(xb, w_in_bf, i, lb[i:i + 1])
        plain3 = plain.reshape(batch, seq, PLAIN_WIDTH)
        o_list, m_list, l_list = [], [], []
        for g, (_, dil) in enumerate(ATTN_GROUPS):
            gs = tuple(slopes[g * hpg:(g + 1) * hpg])
            if dil == 1:
                qkv = plain.reshape(batch, 1, seq, PLAIN_WIDTH)
                cols = (PLAIN_AQ // gw, PLAIN_AK // gw, PLAIN_AV // gw)
            else:
                blocks = [(ATTN_COLUMN + part * ATTN_HEADS * ATTN_HEAD_DIM) // W_IN_BLOCK + g for part in range(3)]
                qkv = _inproj_dilated(xf, w_in_bf, i, blocks, batch, seq, dil)
                cols = (0, 1, 2)
            o, m, l = _attention_group(qkv, qkv, qkv, cols, batch, seq, dil, gs)
            o_list.append(o)
            m_list.append(m)
            l_list.append(l)
        y_hgrn = _hgrn(qg.reshape(batch, seq, 2 * HGRN_WIDTH), plain3,
                       logf.reshape(batch, seq, 2 * HGRN_WIDTH), key.reshape(batch, seq, 2 * HGRN_WIDTH),
                       hgrn_norm_w[i:i + 1])
        y_pool = _pool(plain3, pool_w_b, i, pool_scale[i:i + 1])
        x1, x1b = _merge(y_pool.reshape(T, d), o_list, m_list, l_list, y_hgrn.reshape(T, d), xb, xf,
                         wa, wb, wc, wo, w_in_bf, i, ln1_g[i:i + 1], ln1_b[i:i + 1], alpha)
        xf, xb = _ffn(x1b, x1, p_rows, wg, wu, wd, wpp, wpg, i, ln2_g[i:i + 1], ln2_b[i:i + 1], alpha)
    return xf.reshape(batch, seq, d).astype(x.dtype)
```

```python
import functools

import jax
import jax.numpy as jnp
from jax import lax
from jax.experimental import pallas as pl
from jax.experimental.pallas import tpu as pltpu

F32 = jnp.float32
BF16 = jnp.bfloat16
LANES = 128
SUBLANES = 8
LOG2_E = 1.4426950408889634

D_MODEL = 1024
PLE_DIM = 256
POOL_WINDOWS = (2, 4, 8, 16)
POOL_GROUPS = 4
POOL_GROUP_DIM = D_MODEL // POOL_GROUPS
ATTN_GROUPS = ((128, 1), (512, 4), (2048, 16))
ATTN_HEADS_PER_GROUP = 4
ATTN_HEADS = ATTN_HEADS_PER_GROUP * len(ATTN_GROUPS)
ATTN_HEAD_DIM = 128
ATTN_GROUP_WIDTH = ATTN_HEADS_PER_GROUP * ATTN_HEAD_DIM
ATTN_SIDE = 64
ALIBI_MAX_BIAS = 8.0
NEG_INF = -1e30
HGRN_HEADS = 8
HGRN_HEAD_DIM = 128
HGRN_WIDTH = HGRN_HEADS * HGRN_HEAD_DIM
LN_EPS = 1e-5
RMS_EPS = 1e-6

W_IN_BLOCK = 512
ATTN_COLUMN = D_MODEL
HGRN_COLUMN = ATTN_COLUMN + 3 * ATTN_HEADS * ATTN_HEAD_DIM
GATES_COLUMN = HGRN_COLUMN + 5 * HGRN_WIDTH
PLAIN_HI = 1024
PLAIN_AQ = 2048
PLAIN_AK = 2560
PLAIN_AV = 3072
PLAIN_WIDTH = 3584

INPROJ_ROWS = 256
INPROJ_BLOCKS = (4, 4, 7)
DILATED_ROWS = 1024
ATTN_QUERY_BLOCK = 128
ATTN_KEY_WINDOW = ATTN_QUERY_BLOCK + 2 * ATTN_SIDE
ATTN_TOKENS_PER_STEP = 2048
ATTN_BLOCKS_PER_ITER = 2
ATTN_STAT_LANES = 32
ATTN_WINDOW_OFFSETS = 3
HGRN_BLOCK = 64
HGRN_HEADS_PER_STEP = 4
HGRN_FINISH_ROWS = 256
HGRN_CHUNK = 512
MERGE_ROWS = 512
MERGE_SUBTILES = 2
FFN_ROWS = 512
FFN_SUBTILES = 2
POOL_ROWS = 512
POOL_BAND_GROUPS = 2
POOL_HALO = 16
VMEM_LIMIT = 52 * 1024 * 1024


def _params(*semantics):
    return pltpu.CompilerParams(dimension_semantics=semantics, vmem_limit_bytes=VMEM_LIMIT)


def _layer_spec(stacked, layer):
    tail = stacked.shape[1:]
    return pl.BlockSpec((None,) + tail, lambda i: (layer,) + (0,) * len(tail), pipeline_mode=pl.Buffered(1))


def _sigmoid(x):
    return 0.5 * jnp.tanh(0.5 * x) + 0.5


def _layer_norm(h, g, b):
    mu = jnp.mean(h, axis=-1, keepdims=True)
    d = h - mu
    var = jnp.mean(d * d, axis=-1, keepdims=True)
    return d * lax.rsqrt(var + LN_EPS) * g + b


def _inproj_kernel(x_ref, *refs):
    n_act, n_forget, n_plain = INPROJ_BLOCKS
    w_refs = refs[:n_act + n_forget + n_plain]
    lb_ref, act_ref, logf_ref, key_ref, plain_ref = refs[n_act + n_forget + n_plain:]
    x = x_ref[...].astype(BF16)
    wb = W_IN_BLOCK
    z = [jnp.dot(x, w_ref[...], preferred_element_type=F32) for w_ref in w_refs[:n_act + n_forget]]
    for n, w_ref in enumerate(w_refs[n_act + n_forget:]):
        plain_ref[:, n * wb:(n + 1) * wb] = jnp.dot(x, w_ref[...], preferred_element_type=F32).astype(plain_ref.dtype)
    for n in range(n_act):
        act_ref[:, n * wb:(n + 1) * wb] = (z[n] * _sigmoid(z[n])).astype(act_ref.dtype)
    for n in range(n_forget):
        cols = slice(n * wb, (n + 1) * wb)
        lb = lb_ref[:, cols]
        sig = _sigmoid(z[n_act + n])
        logf_ref[:, cols] = jnp.log(lb + (1.0 - lb) * sig).astype(logf_ref.dtype)
        key_ref[:, cols] = ((1.0 - lb) * (1.0 - sig)).astype(key_ref.dtype)


def _inproj(x, w_in_bf, layer, lb):
    T, D = x.shape
    tm = min(T, INPROJ_ROWS)
    wb = W_IN_BLOCK
    hw = HGRN_WIDTH
    span = lambda start, width: list(range(start // wb, (start + width) // wb))
    attn_part = ATTN_HEADS * ATTN_HEAD_DIM
    act_blocks = span(HGRN_COLUMN, hw) + span(HGRN_COLUMN + 4 * hw, hw)
    forget_blocks = span(HGRN_COLUMN + 2 * hw, 2 * hw)
    plain_blocks = (span(0, D_MODEL) + span(HGRN_COLUMN + hw, hw)
                    + [(ATTN_COLUMN + part * attn_part) // wb for part in range(3)])
    blocks = act_blocks + forget_blocks + plain_blocks
    assert (len(act_blocks), len(forget_blocks), len(plain_blocks)) == INPROJ_BLOCKS
    wspec = lambda blk: pl.BlockSpec((None, D, wb), lambda i: (layer, 0, blk), pipeline_mode=pl.Buffered(1))
    ospec = lambda n: pl.BlockSpec((tm, n * wb), lambda i: (i, 0))
    oshape = lambda n: jax.ShapeDtypeStruct((T, n * wb), BF16)
    n_act, n_forget, n_plain = INPROJ_BLOCKS
    return pl.pallas_call(
        _inproj_kernel,
        grid=(T // tm,),
        in_specs=[pl.BlockSpec((tm, D), lambda i: (i, 0))] + [wspec(blk) for blk in blocks]
                 + [pl.BlockSpec((1, n_forget * wb), lambda i: (0, 0))],
        out_specs=[ospec(n_act), ospec(n_forget), ospec(n_forget), ospec(n_plain)],
        out_shape=[oshape(n_act), oshape(n_forget), oshape(n_forget), oshape(n_plain)],
        compiler_params=_params("parallel"),
        name="inproj",
    )(x, *([w_in_bf] * len(blocks)), lb)


def _inproj_dilated_kernel(*refs, dilation):
    nslab = D_MODEL // LANES
    x_refs = refs[:nslab]
    wq_ref, wk_ref, wv_ref, o_ref, xp_ref = refs[nslab:]
    n = x_refs[0].shape[0] // dilation
    for r in range(dilation):
        for c, x_ref in enumerate(x_refs):
            xp_ref[r * n:(r + 1) * n, c * LANES:(c + 1) * LANES] = (
                x_ref[pl.ds(r, n, stride=dilation), :].astype(BF16))
    xp = xp_ref[...]
    for part, w_ref in enumerate((wq_ref, wk_ref, wv_ref)):
        res = jnp.dot(xp, w_ref[...], preferred_element_type=F32).astype(o_ref.dtype)
        cols = slice(part * ATTN_GROUP_WIDTH, (part + 1) * ATTN_GROUP_WIDTH)
        for r in range(dilation):
            o_ref[0, r, :, cols] = res[r * n:(r + 1) * n]


def _inproj_dilated(x_f32, w_in_bf, layer, blocks, batch, seq, dilation):
    gw = ATTN_GROUP_WIDTH
    tb = min(seq, DILATED_ROWS)
    steps = seq // tb
    n = tb // dilation
    nslab = D_MODEL // LANES
    wspec = lambda blk: pl.BlockSpec((None, D_MODEL, gw), lambda i: (layer, 0, blk))
    xspec = lambda c: pl.BlockSpec((tb, LANES), lambda i: (i, c))
    return pl.pallas_call(
        functools.partial(_inproj_dilated_kernel, dilation=dilation),
        grid=(batch * steps,),
        in_specs=[xspec(c) for c in range(nslab)] + [wspec(blk) for blk in blocks],
        out_specs=pl.BlockSpec((1, dilation, n, 3 * gw), lambda i: (i // steps, 0, i % steps, 0)),
        out_shape=jax.ShapeDtypeStruct((batch, dilation, seq // dilation, 3 * gw), BF16),
        scratch_shapes=[pltpu.VMEM((tb, D_MODEL), BF16)],
        compiler_params=_params("parallel"),
        name="inproj_dilated%d" % dilation,
    )(*([x_f32] * nslab), w_in_bf, w_in_bf, w_in_bf)


def _attn_kernel(q_ref, k_ref, v_ref, o_ref, m_ref, l_ref, oacc_ref, macc_ref, lacc_ref, bias_ref, sa_ref, sb_ref, *,
                 seq_len, dilation, slopes):
    c = pl.program_id(1)
    lq = q_ref.shape[2]
    qb, kw = ATTN_QUERY_BLOCK, ATTN_KEY_WINDOW
    nq = lq // qb
    per_iter = ATTN_BLOCKS_PER_ITER
    scale2 = ATTN_HEAD_DIM ** -0.5 * LOG2_E
    heads = range(ATTN_HEADS_PER_GROUP)
    lanes = [slice(h * ATTN_HEAD_DIM, (h + 1) * ATTN_HEAD_DIM) for h in heads]

    @pl.when((pl.program_id(0) == 0) & (c == 0))
    def _():
        row = lax.broadcasted_iota(jnp.int32, (qb, kw), 0)
        col = lax.broadcasted_iota(jnp.int32, (qb, kw), 1)
        for w in range(ATTN_WINDOW_OFFSETS):
            dist = jnp.abs(col - row - w * ATTN_SIDE)
            distf = dist.astype(F32) * float(dilation)
            for h in heads:
                bias_ref[h, w] = jnp.where(dist <= ATTN_SIDE, distf * (-slopes[h] * LOG2_E), NEG_INF * LOG2_E)

    stat_group = lax.broadcasted_iota(jnp.int32, (kw, ATTN_HEAD_DIM), 1) // ATTN_STAT_LANES
    ones_cols = [jnp.where(stat_group == h, 1.0, 0.0).astype(BF16) for h in heads]
    out_group = lax.broadcasted_iota(jnp.int32, (qb, ATTN_HEAD_DIM), 1) // ATTN_STAT_LANES

    def block_coords(idx):
        r = idx // nq
        r0 = pl.multiple_of((idx % nq) * qb, qb)
        t0 = c * lq + r0
        ks = pl.multiple_of(jnp.clip(t0 - ATTN_SIDE, 0, seq_len - kw), ATTN_SIDE)
        return r, r0, t0, ks

    def score_products(it, s_ref):
        for u in range(per_iter):
            r, r0, _, ks = block_coords(it * per_iter + u)
            q = q_ref[0, r, pl.ds(r0, qb), :]
            k = k_ref[0, r, pl.ds(ks, kw), :]
            for h in heads:
                s_ref[u * len(heads) + h] = lax.dot_general(
                    q[:, lanes[h]], k[:, lanes[h]], (((1,), (1,)), ((), ())), preferred_element_type=F32)

    n_iter = dilation * nq // per_iter
    score_products(0, sa_ref)

    def half_body(it, s_ref, next_ref):
        score_products(jnp.minimum(it + 1, n_iter - 1), next_ref)
        blocks = []
        for u in range(per_iter):
            r, r0, t0, ks = block_coords(it * per_iter + u)
            if dilation == 1:
                out_rows = pl.ds(r0, qb)
            else:
                out_rows = pl.ds(r0 * dilation + r, qb, stride=dilation)
            blocks.append(((t0 - ks) // ATTN_SIDE, out_rows, v_ref[0, r, pl.ds(ks, kw), :]))
        probs = []
        for u, (w, _, _) in enumerate(blocks):
            block_probs = []
            for h in heads:
                s = s_ref[u * len(heads) + h] * scale2 + bias_ref[h, w]
                m = jnp.max(s, axis=1, keepdims=True)
                block_probs.append((jnp.exp2(s - m).astype(BF16), m))
            probs.append(block_probs)
        pv = [[jnp.dot(block_probs[h][0], jnp.concatenate([v[:, lanes[h]], ones_cols[h]], axis=1),
                       preferred_element_type=F32) for h in heads]
              for (_, _, v), block_probs in zip(blocks, probs)]
        for (_, out_rows, _), block_probs, block_pv in zip(blocks, probs, pv):
            m_all = jnp.broadcast_to(block_probs[-1][1], (qb, ATTN_HEAD_DIM))
            l_all = block_pv[-1][:, ATTN_HEAD_DIM:]
            for h in reversed(heads[:-1]):
                m_all = jnp.where(out_group == h, block_probs[h][1], m_all)
                l_all = l_all + block_pv[h][:, ATTN_HEAD_DIM:]
            for h in heads:
                oacc_ref[h, out_rows, :] = block_pv[h][:, :ATTN_HEAD_DIM]
            macc_ref[out_rows, :] = m_all
            lacc_ref[out_rows, :] = l_all

    def body(it2, carry):
        half_body(2 * it2, sa_ref, sb_ref)
        half_body(2 * it2 + 1, sb_ref, sa_ref)
        return carry

    lax.fori_loop(0, n_iter // 2, body, 0, unroll=True)
    for h in heads:
        o_ref[0, :, lanes[h]] = oacc_ref[h].astype(o_ref.dtype)
    m_ref[0] = macc_ref[...]
    l_ref[0] = lacc_ref[...]


def _attention_group(q_arr, k_arr, v_arr, cols, batch, seq, dilation, slopes):
    L = seq // dilation
    gw = ATTN_GROUP_WIDTH
    tc = min(seq, ATTN_TOKENS_PER_STEP)
    lq = tc // dilation
    kern = functools.partial(_attn_kernel, seq_len=L, dilation=dilation, slopes=slopes)
    qc, kc, vc = cols
    stat = pl.BlockSpec((1, tc, ATTN_HEAD_DIM), lambda b, c: (b, c, 0))
    stat_shape = jax.ShapeDtypeStruct((batch, seq, ATTN_HEAD_DIM), F32)
    o, m, l = pl.pallas_call(
        kern,
        grid=(batch, seq // tc),
        in_specs=[pl.BlockSpec((1, dilation, lq, gw), lambda b, c: (b, 0, c, qc)),
                  pl.BlockSpec((1, dilation, L, gw), lambda b, c: (b, 0, 0, kc)),
                  pl.BlockSpec((1, dilation, L, gw), lambda b, c: (b, 0, 0, vc))],
        out_specs=[pl.BlockSpec((1, tc, gw), lambda b, c: (b, c, 0)), stat, stat],
        out_shape=[jax.ShapeDtypeStruct((batch, seq, gw), BF16), stat_shape, stat_shape],
        scratch_shapes=[pltpu.VMEM((ATTN_HEADS_PER_GROUP, tc, ATTN_HEAD_DIM), F32),
                        pltpu.VMEM((tc, ATTN_HEAD_DIM), F32), pltpu.VMEM((tc, ATTN_HEAD_DIM), F32),
                        pltpu.VMEM((ATTN_HEADS_PER_GROUP, ATTN_WINDOW_OFFSETS, ATTN_QUERY_BLOCK, ATTN_KEY_WINDOW), F32),
                        pltpu.VMEM((ATTN_BLOCKS_PER_ITER * ATTN_HEADS_PER_GROUP, ATTN_QUERY_BLOCK, ATTN_KEY_WINDOW), F32),
                        pltpu.VMEM((ATTN_BLOCKS_PER_ITER * ATTN_HEADS_PER_GROUP, ATTN_QUERY_BLOCK, ATTN_KEY_WINDOW), F32)],
        compiler_params=_params("arbitrary", "arbitrary"),
        name="attn_dil%d" % dilation,
    )(q_arr, k_arr, v_arr)
    T = batch * seq
    return o.reshape(T, gw), m.reshape(T, ATTN_HEAD_DIM), l.reshape(T, ATTN_HEAD_DIM)


def _running_sum_rows(x, row_index, reverse):
    n = x.shape[0]
    shift = 1
    while shift < n:
        if reverse:
            x = x + jnp.where(row_index < n - shift, pltpu.roll(x, n - shift, 0), 0.0)
        else:
            x = x + jnp.where(row_index >= shift, pltpu.roll(x, shift, 0), 0.0)
        shift *= 2
    return x


def _hgrn_kernel(qf_ref, vf_ref, lf_ref, kf_ref, qb_ref, vb_ref, lb_ref, kb_ref, g_ref, nw_ref, o_ref,
                 accf_ref, accb_ref, sf_ref, sb_ref):
    c = pl.program_id(2)
    nc = pl.num_programs(2)
    chunk = qf_ref.shape[1]
    seq = o_ref.shape[1]
    hp = HGRN_HEADS_PER_STEP
    blk = HGRN_BLOCK
    half = blk // 2
    nblk = chunk // blk
    dh = HGRN_HEAD_DIM
    pw = 2 * dh

    @pl.when(c == 0)
    def _():
        sf_ref[...] = jnp.zeros_like(sf_ref)
        sb_ref[...] = jnp.zeros_like(sb_ref)

    row_index = lax.broadcasted_iota(jnp.int32, (blk, pw), 0)
    ti = lax.broadcasted_iota(jnp.int32, (blk, 2 * blk), 0)
    si = lax.broadcasted_iota(jnp.int32, (blk, 2 * blk), 1) % blk
    causal = si <= ti
    anticausal = si >= ti
    zeros_kv = jnp.zeros((blk, dh), BF16)
    zeros_st = jnp.zeros((dh, dh), BF16)

    def block_diag(a, b, z):
        return jnp.concatenate([jnp.concatenate([a, z], axis=1), jnp.concatenate([z, b], axis=1)], axis=0)

    def load_unit(pair, r0, q_ref, v_ref, logf_ref, key_ref, s_ref):
        lanes = slice(pair * pw, (pair + 1) * pw)
        rows = pl.ds(r0, blk)
        return (q_ref[0, rows, lanes], v_ref[0, rows, lanes], logf_ref[0, rows, lanes], key_ref[0, rows, lanes],
                s_ref[2 * pair], s_ref[2 * pair + 1])

    def prepare_unit(loaded, backward):
        q, v, logf, k, st0, st1 = loaded
        a = _running_sum_rows(logf.astype(F32), row_index, backward)
        if backward:
            total = a[0:1, :]
            mid = a[half:half + 1, :]
        else:
            total = a[blk - 1:blk, :]
            mid = a[half - 1:half, :]
        q_mid = q * jnp.exp(a - mid).astype(BF16)
        k_mid_b = k * jnp.exp(mid - a).astype(BF16)
        q_in = q_mid * jnp.exp(mid).astype(BF16)
        k_out = k_mid_b * jnp.exp(total - mid).astype(BF16)
        decay = jnp.broadcast_to(jnp.exp(total), (SUBLANES, pw)).T[:, 0:1]
        return dict(q_mid=q_mid, k_diag=block_diag(k_mid_b[:, :dh], k_mid_b[:, dh:], zeros_kv),
                    q_in=q_in, s_diag=block_diag(st0.astype(BF16), st1.astype(BF16), zeros_st),
                    v=v, v_diag=block_diag(v[:, :dh], v[:, dh:], zeros_kv), k_out=k_out,
                    decay=decay, states=(st0, st1), mask=anticausal if backward else causal)

    nt = (((1,), (1,)), ((), ()))
    tn = (((0,), (0,)), ((), ()))

    def body(n, carry):
        rf = pl.multiple_of(n * blk, blk)
        rb = pl.multiple_of((nblk - 1 - n) * blk, blk)
        units = []
        for pair in range(hp // 2):
            units.append((pair, False, load_unit(pair, rf, qf_ref, vf_ref, lf_ref, kf_ref, sf_ref)))
            units.append((pair, True, load_unit(pair, rb, qb_ref, vb_ref, lb_ref, kb_ref, sb_ref)))
        prepared = [prepare_unit(loaded, backward) for _, backward, loaded in units]
        att = [lax.dot_general(u["q_mid"], u["k_diag"], nt, preferred_element_type=F32) for u in prepared]
        inter = [jnp.dot(u["q_in"], u["s_diag"], preferred_element_type=F32) for u in prepared]
        upd = [[lax.dot_general(u["k_out"][:, j * dh:(j + 1) * dh], u["v"][:, j * dh:(j + 1) * dh], tn,
                                preferred_element_type=F32) for j in range(2)] for u in prepared]
        att = [jnp.where(u["mask"], s, 0.0).astype(BF16) for u, s in zip(prepared, att)]
        intra = [jnp.dot(s, u["v_diag"], preferred_element_type=F32) for u, s in zip(prepared, att)]
        for i, (pair, backward, _) in enumerate(units):
            u = prepared[i]
            lanes = slice(pair * pw, (pair + 1) * pw)
            o = intra[i] + inter[i]
            states = [st * u["decay"][j * dh:(j + 1) * dh, :] + upd[i][j] for j, st in enumerate(u["states"])]
            if backward:
                accb_ref[pl.ds(pl.multiple_of((nc - 1 - c) * chunk + rb, blk), blk), lanes] = o
                sb_ref[2 * pair], sb_ref[2 * pair + 1] = states
            else:
                accf_ref[pl.ds(pl.multiple_of(c * chunk + rf, blk), blk), lanes] = o
                sf_ref[2 * pair], sf_ref[2 * pair + 1] = states
        return carry

    lax.fori_loop(0, nblk, body, 0, unroll=True)

    @pl.when(c == nc - 1)
    def _():
        fin = min(seq, HGRN_FINISH_ROWS)

        def finish(t, carry):
            rows = pl.ds(pl.multiple_of(t * fin, fin), fin)
            for h in range(hp):
                lanes = slice(h * dh, (h + 1) * dh)
                o = accf_ref[rows, lanes] + accb_ref[rows, lanes]
                o = o * lax.rsqrt(jnp.mean(o * o, axis=-1, keepdims=True) + RMS_EPS)
                o = o * nw_ref[:, lanes] * g_ref[0, rows, lanes].astype(F32)
                o_ref[0, rows, lanes] = o.astype(o_ref.dtype)
            return carry

        lax.fori_loop(0, seq // fin, finish, 0, unroll=4)


def _hgrn(qg3, plain3, logf3, key3, norm_w):
    batch, seq, _ = qg3.shape
    hp = HGRN_HEADS_PER_STEP
    w = hp * HGRN_HEAD_DIM
    chunk = min(seq, HGRN_CHUNK)
    nc = seq // chunk

    def fwd(offset):
        base = offset // w
        return pl.BlockSpec((1, chunk, w), lambda b, h, c: (b, c, base + h))

    def bwd(offset):
        base = offset // w
        return pl.BlockSpec((1, chunk, w), lambda b, h, c: (b, nc - 1 - c, base + h))

    gbase = HGRN_WIDTH // w
    return pl.pallas_call(
        _hgrn_kernel,
        grid=(batch, HGRN_HEADS // hp, nc),
        in_specs=[fwd(0), fwd(PLAIN_HI), fwd(0), fwd(0),
                  bwd(0), bwd(PLAIN_HI), bwd(HGRN_WIDTH), bwd(HGRN_WIDTH),
                  pl.BlockSpec((1, seq, w), lambda b, h, c: (b, 0, gbase + h)),
                  pl.BlockSpec((1, w), lambda b, h, c: (0, h))],
        out_specs=pl.BlockSpec((1, seq, w), lambda b, h, c: (b, 0, h)),
        out_shape=jax.ShapeDtypeStruct((batch, seq, HGRN_WIDTH), BF16),
        scratch_shapes=[pltpu.VMEM((seq, w), F32), pltpu.VMEM((seq, w), F32),
                        pltpu.VMEM((hp, HGRN_HEAD_DIM, HGRN_HEAD_DIM), F32),
                        pltpu.VMEM((hp, HGRN_HEAD_DIM, HGRN_HEAD_DIM), F32)],
        compiler_params=_params("parallel", "parallel", "arbitrary"),
        name="hgrn",
    )(qg3, plain3, logf3, key3, qg3, plain3, logf3, key3, qg3, norm_w)


def _pool_kernel(prev_ref, cur_ref, next_ref, band_ref, pw_ref, ps_ref, o_ref, *, seq_len):
    i = pl.program_id(1)
    last = pl.num_programs(1) - 1
    ts = cur_ref.shape[1]
    n = ts + 2 * POOL_HALO
    pos = i * ts + lax.broadcasted_iota(jnp.int32, (ts, 1), 0)
    first_band = POOL_GROUPS - POOL_BAND_GROUPS
    tiles, wsums = {}, {}
    for g in list(range(first_band, POOL_GROUPS)) + list(range(first_band)):
        lanes = slice(g * POOL_GROUP_DIM, (g + 1) * POOL_GROUP_DIM)
        half = POOL_WINDOWS[g] // 2
        prev = jnp.where(i > 0, prev_ref[0, :, lanes], jnp.zeros((), BF16))
        nxt = jnp.where(i < last, next_ref[0, :, lanes], jnp.zeros((), BF16))
        cur = cur_ref[0, :, lanes]
        tiles[g] = cur.astype(F32)
        if g >= first_band:
            wsums[g] = jnp.dot(band_ref[g - first_band], jnp.concatenate([prev, cur, nxt], axis=0),
                               preferred_element_type=F32)
        else:
            ext = jnp.concatenate([prev.astype(F32), tiles[g], nxt.astype(F32)], axis=0)
            w = ext + pltpu.roll(ext, 1, 0)
            step = 1
            while step < half:
                w = pltpu.roll(w, step, 0) + pltpu.roll(w, n - step, 0)
                step *= 2
            wsums[g] = w[POOL_HALO:POOL_HALO + ts]
    for g in range(POOL_GROUPS):
        lanes = slice(g * POOL_GROUP_DIM, (g + 1) * POOL_GROUP_DIM)
        half = POOL_WINDOWS[g] // 2
        count = (jnp.minimum(pos + half, seq_len) - jnp.maximum(pos - half, 0)).astype(F32)
        mixed = wsums[g] / count - tiles[g]
        y = jnp.dot(mixed.astype(BF16), pw_ref[g], preferred_element_type=F32)
        o_ref[0, :, lanes] = (y * ps_ref[:, lanes]).astype(o_ref.dtype)


def _pool(plain3, pool_w, layer, pool_scale):
    batch, seq, _ = plain3.shape
    ts = min(seq, POOL_ROWS)
    hb = ts // POOL_HALO
    nhalo = seq // POOL_HALO
    n = ts + 2 * POOL_HALO
    offset = (jnp.arange(n, dtype=jnp.int32)[None, :] - POOL_HALO) - jnp.arange(ts, dtype=jnp.int32)[:, None]
    bands = jnp.stack([((offset >= -(w // 2)) & (offset < w // 2)).astype(BF16)
                       for w in POOL_WINDOWS[POOL_GROUPS - POOL_BAND_GROUPS:]])
    kern = functools.partial(_pool_kernel, seq_len=seq)
    return pl.pallas_call(
        kern,
        grid=(batch, seq // ts),
        in_specs=[pl.BlockSpec((1, POOL_HALO, D_MODEL), lambda b, i: (b, jnp.maximum(i * hb - 1, 0), 0)),
                  pl.BlockSpec((1, ts, D_MODEL), lambda b, i: (b, i, 0)),
                  pl.BlockSpec((1, POOL_HALO, D_MODEL), lambda b, i: (b, jnp.minimum((i + 1) * hb, nhalo - 1), 0)),
                  pl.BlockSpec((POOL_BAND_GROUPS, ts, n), lambda b, i: (0, 0, 0)),
                  pl.BlockSpec((None, POOL_GROUPS, POOL_GROUP_DIM, POOL_GROUP_DIM), lambda b, i: (layer, 0, 0, 0)),
                  pl.BlockSpec((1, D_MODEL), lambda b, i: (0, 0))],
        out_specs=pl.BlockSpec((1, ts, D_MODEL), lambda b, i: (b, i, 0)),
        out_shape=jax.ShapeDtypeStruct((batch, seq, D_MODEL), BF16),
        compiler_params=_params("parallel", "parallel"),
        name="pool",
    )(plain3, plain3, plain3, bands, pool_w, pool_scale)


def _merge_kernel(ya_ref, o1_ref, o2_ref, o3_ref, m1_ref, m2_ref, m3_ref, l1_ref, l2_ref, l3_ref, hc_ref,
                  xin_ref, x_ref, wa_ref, wb_ref, wc_ref, wo_ref, *rest, alpha, nsub):
    wgate_refs = rest[:-4]
    g_ref, b_ref, xo_ref, xb_ref = rest[-4:]
    d = D_MODEL
    dh = ATTN_HEAD_DIM
    sub = x_ref.shape[0] // nsub
    rows = [slice(s * sub, (s + 1) * sub) for s in range(nsub)]
    xin = [xin_ref[r, :].astype(BF16) for r in rows]
    y_a = [jnp.dot(ya_ref[r, :], wa_ref[...], preferred_element_type=F32) for r in rows]
    y_c = [jnp.dot(hc_ref[r, :], wc_ref[...], preferred_element_type=F32) for r in rows]
    per_gate = d // W_IN_BLOCK
    gates = [[jnp.concatenate([jnp.dot(x, w_ref[...], preferred_element_type=F32)
                               for w_ref in wgate_refs[n * per_gate:(n + 1) * per_gate]], axis=1)
              for n in range(3)] for x in xin]
    att = []
    for r in rows:
        ms = [m1_ref[r, :], m2_ref[r, :], m3_ref[r, :]]
        ls = [l1_ref[r, :], l2_ref[r, :], l3_ref[r, :]]
        top = jnp.maximum(jnp.maximum(ms[0], ms[1]), ms[2])
        es = [jnp.exp2(m - top) for m in ms]
        inv = 1.0 / (es[0] * ls[0] + es[1] * ls[1] + es[2] * ls[2])
        ws = [e * inv for e in es]
        heads = []
        for h in range(ATTN_HEADS_PER_GROUP):
            lanes = slice(h * dh, (h + 1) * dh)
            stat = slice(h * ATTN_STAT_LANES, h * ATTN_STAT_LANES + 1)
            acc = None
            for w, o_ref in zip(ws, (o1_ref, o2_ref, o3_ref)):
                term = jnp.broadcast_to(w[:, stat], (sub, dh)) * o_ref[r, lanes].astype(F32)
                acc = term if acc is None else acc + term
            heads.append(acc)
        att.append(jnp.concatenate(heads, axis=1).astype(BF16))
    y_b = [jnp.dot(a, wb_ref[...], preferred_element_type=F32) for a in att]
    merged = [(_sigmoid(gates[s][0]) * y_a[s] + _sigmoid(gates[s][1]) * y_b[s]
               + _sigmoid(gates[s][2]) * y_c[s]).astype(BF16) for s in range(nsub)]
    mix = [jnp.dot(mg, wo_ref[...], preferred_element_type=F32) for mg in merged]
    for s, r in enumerate(rows):
        x1 = _layer_norm(alpha * x_ref[r, :] + mix[s], g_ref[...], b_ref[...])
        xo_ref[r, :] = x1
        xb_ref[r, :] = x1.astype(BF16)


def _merge(y_pool, o_list, m_list, l_list, y_hgrn, x_bf, x, wa, wb, wc, wo, w_in_bf, layer, ln_g, ln_b, alpha):
    T = x.shape[0]
    tm = min(T, MERGE_ROWS)
    gw = ATTN_GROUP_WIDTH
    d = D_MODEL
    row = lambda w: pl.BlockSpec((tm, w), lambda i: (i, 0))
    const = lambda a: pl.BlockSpec(a.shape, lambda i: (0,) * a.ndim, pipeline_mode=pl.Buffered(1))
    kern = functools.partial(_merge_kernel, alpha=alpha, nsub=MERGE_SUBTILES)
    stat = row(ATTN_HEAD_DIM)
    gate_blocks = range(GATES_COLUMN // W_IN_BLOCK, (GATES_COLUMN + 3 * d) // W_IN_BLOCK)
    gate_specs = [pl.BlockSpec((None, d, W_IN_BLOCK), lambda i, blk=blk: (layer, 0, blk),
                               pipeline_mode=pl.Buffered(1)) for blk in gate_blocks]
    return pl.pallas_call(
        kern,
        grid=(T // tm,),
        in_specs=[row(d), row(gw), row(gw), row(gw), stat, stat, stat, stat, stat, stat, row(d), row(d),
                  row(d)] + [_layer_spec(w, layer) for w in (wa, wb, wc, wo)] + gate_specs
                 + [const(ln_g), const(ln_b)],
        out_specs=[row(d), row(d)],
        out_shape=[jax.ShapeDtypeStruct((T, d), F32), jax.ShapeDtypeStruct((T, d), BF16)],
        compiler_params=_params("parallel"),
        name="merge",
    )(y_pool, *o_list, *m_list, *l_list, y_hgrn, x_bf, x, wa, wb, wc, wo, *([w_in_bf] * len(gate_specs)), ln_g, ln_b)


def _ffn_kernel(xb_ref, x_ref, p_ref, wg_ref, wu_ref, wd_ref, wpp_ref, wpg_ref, g_ref, b_ref,
                xo_ref, xbo_ref, *, alpha, nsub):
    sub = xb_ref.shape[0] // nsub
    rows = [slice(s * sub, (s + 1) * sub) for s in range(nsub)]
    xb = [xb_ref[r, :] for r in rows]
    gate = [jnp.dot(x, wg_ref[...], preferred_element_type=F32) for x in xb]
    up = [jnp.dot(x, wu_ref[...], preferred_element_type=F32) for x in xb]
    pgate = [jnp.dot(x, wpg_ref[...], preferred_element_type=F32) for x in xb]
    ple = [jnp.dot(p_ref[r, :].astype(BF16), wpp_ref[...], preferred_element_type=F32) for r in rows]
    hidden = [(g * _sigmoid(g) * u).astype(BF16) for g, u in zip(gate, up)]
    ffn = [jnp.dot(h, wd_ref[...], preferred_element_type=F32) for h in hidden]
    for s, r in enumerate(rows):
        x2 = _layer_norm(alpha * x_ref[r, :] + ffn[s] + ple[s] * _sigmoid(pgate[s]), g_ref[...], b_ref[...])
        xo_ref[r, :] = x2
        xbo_ref[r, :] = x2.astype(BF16)


def _ffn(x1_bf, x1, p, wg, wu, wd, wpp, wpg, layer, ln_g, ln_b, alpha):
    T = x1.shape[0]
    tm = min(T, FFN_ROWS)
    d = D_MODEL
    row = lambda w: pl.BlockSpec((tm, w), lambda i: (i, 0))
    const = lambda a: pl.BlockSpec(a.shape, lambda i: (0,) * a.ndim, pipeline_mode=pl.Buffered(1))
    weight = lambda a: _layer_spec(a, layer)
    kern = functools.partial(_ffn_kernel, alpha=alpha, nsub=FFN_SUBTILES)
    return pl.pallas_call(
        kern,
        grid=(T // tm,),
        in_specs=[row(d), row(d), pl.BlockSpec((None, tm, PLE_DIM), lambda i: (layer, i, 0)),
                  weight(wg), weight(wu), weight(wd), weight(wpp), weight(wpg), const(ln_g), const(ln_b)],
        out_specs=[row(d), row(d)],
        out_shape=[jax.ShapeDtypeStruct((T, d), F32), jax.ShapeDtypeStruct((T, d), BF16)],
        compiler_params=_params("parallel"),
        name="ffn",
    )(x1_bf, x1, p, wg, wu, wd, wpp, wpg, ln_g, ln_b)


def kernel(x, p, w_in, pool_w, pool_scale, w_branch_a, w_branch_b, w_branch_c, hgrn_lb_logits, hgrn_norm_w, w_out, ln1_g, ln1_b, w_ffn_gate, w_ffn_up, w_ffn_down, w_ple_proj, w_ple_gate, ln2_g, ln2_b):
    batch, seq, d = x.shape
    depth = w_in.shape[0]
    T = batch * seq
    alpha = float((2 * depth) ** 0.25)

    lb = jnp.cumsum(jax.nn.softmax(hgrn_lb_logits.astype(F32), axis=0), axis=0)
    lb = lb - lb[:1]

    w_in_bf = w_in.astype(BF16)
    bf = lambda a: a.astype(BF16)
    pool_w_b, wa, wb, wc, wo = bf(pool_w), bf(w_branch_a), bf(w_branch_b), bf(w_branch_c), bf(w_out)
    wg, wu, wd, wpp, wpg = bf(w_ffn_gate), bf(w_ffn_up), bf(w_ffn_down), bf(w_ple_proj), bf(w_ple_gate)

    slopes = [2.0 ** (-ALIBI_MAX_BIAS * (i + 1) / ATTN_HEADS) for i in range(ATTN_HEADS)]
    hpg = ATTN_HEADS_PER_GROUP
    gw = ATTN_GROUP_WIDTH

    xf = x.reshape(T, d).astype(F32)
    xb = xf
    p_rows = p.reshape(depth, T, PLE_DIM)
    for i in range(depth):
        qg, logf, key, plain = _inproj(xb, w_in_bf, i, lb[i:i + 1])
        plain3 = plain.reshape(batch, seq, PLAIN_WIDTH)
        o_list, m_list, l_list = [], [], []
        for g, (_, dil) in enumerate(ATTN_GROUPS):
            gs = tuple(slopes[g * hpg:(g + 1) * hpg])
            if dil == 1:
                qkv = plain.reshape(batch, 1, seq, PLAIN_WIDTH)
                cols = (PLAIN_AQ // gw, PLAIN_AK // gw, PLAIN_AV // gw)
            else:
                blocks = [(ATTN_COLUMN + part * ATTN_HEADS * ATTN_HEAD_DIM) // W_IN_BLOCK + g for part in range(3)]
                qkv = _inproj_dilated(xf, w_in_bf, i, blocks, batch, seq, dil)
                cols = (0, 1, 2)
            o, m, l = _attention_group(qkv, qkv, qkv, cols, batch, seq, dil, gs)
            o_list.append(o)
            m_list.append(m)
            l_list.append(l)
        y_hgrn = _hgrn(qg.reshape(batch, seq, 2 * HGRN_WIDTH), plain3,
                       logf.reshape(batch, seq, 2 * HGRN_WIDTH), key.reshape(batch, seq, 2 * HGRN_WIDTH),
                       hgrn_norm_w[i:i + 1])
        y_pool = _pool(plain3, pool_w_b, i, pool_scale[i:i + 1])
        x1, x1b = _merge(y_pool.reshape(T, d), o_list, m_list, l_list, y_hgrn.reshape(T, d), xb, xf,
                         wa, wb, wc, wo, w_in_bf, i, ln1_g[i:i + 1], ln1_b[i:i + 1], alpha)
        xf, xb = _ffn(x1b, x1, p_rows, wg, wu, wd, wpp, wpg, i, ln2_g[i:i + 1], ln2_b[i:i + 1], alpha)
    return xf.reshape(batch, seq, d).astype(x.dtype)
```

```python
import functools

import jax
import jax.numpy as jnp
from jax import lax
from jax.experimental import pallas as pl
from jax.experimental.pallas import tpu as pltpu

F32 = jnp.float32
BF16 = jnp.bfloat16
LANES = 128
SUBLANES = 8
LOG2_E = 1.4426950408889634

D_MODEL = 1024
PLE_DIM = 256
POOL_WINDOWS = (2, 4, 8, 16)
POOL_GROUPS = 4
POOL_GROUP_DIM = D_MODEL // POOL_GROUPS
ATTN_GROUPS = ((128, 1), (512, 4), (2048, 16))
ATTN_HEADS_PER_GROUP = 4
ATTN_HEADS = ATTN_HEADS_PER_GROUP * len(ATTN_GROUPS)
ATTN_HEAD_DIM = 128
ATTN_GROUP_WIDTH = ATTN_HEADS_PER_GROUP * ATTN_HEAD_DIM
ATTN_SIDE = 64
ALIBI_MAX_BIAS = 8.0
NEG_INF = -1e30
HGRN_HEADS = 8
HGRN_HEAD_DIM = 128
HGRN_WIDTH = HGRN_HEADS * HGRN_HEAD_DIM
LN_EPS = 1e-5
RMS_EPS = 1e-6

W_IN_BLOCK = 512
ATTN_COLUMN = D_MODEL
HGRN_COLUMN = ATTN_COLUMN + 3 * ATTN_HEADS * ATTN_HEAD_DIM
GATES_COLUMN = HGRN_COLUMN + 5 * HGRN_WIDTH
PLAIN_HI = 1024
PLAIN_AQ = 2048
PLAIN_AK = 2560
PLAIN_AV = 3072
PLAIN_WIDTH = 3584

INPROJ_ROWS = 256
INPROJ_BLOCKS = (4, 4, 7)
DILATED_ROWS = 1024
ATTN_QUERY_BLOCK = 128
ATTN_KEY_WINDOW = ATTN_QUERY_BLOCK + 2 * ATTN_SIDE
ATTN_TOKENS_PER_STEP = 2048
ATTN_BLOCKS_PER_ITER = 2
ATTN_STAT_LANES = 32
ATTN_WINDOW_OFFSETS = 3
HGRN_BLOCK = 64
HGRN_HEADS_PER_STEP = 4
HGRN_FINISH_ROWS = 256
HGRN_CHUNK = 512
MERGE_ROWS = 512
MERGE_SUBTILES = 2
FFN_ROWS = 512
FFN_SUBTILES = 2
POOL_ROWS = 512
POOL_HALO = 16
VMEM_LIMIT = 52 * 1024 * 1024


def _params(*semantics):
    return pltpu.CompilerParams(dimension_semantics=semantics, vmem_limit_bytes=VMEM_LIMIT)


def _layer_spec(stacked, layer):
    tail = stacked.shape[1:]
    return pl.BlockSpec((None,) + tail, lambda i: (layer,) + (0,) * len(tail), pipeline_mode=pl.Buffered(1))


def _sigmoid(x):
    return 0.5 * jnp.tanh(0.5 * x) + 0.5


def _layer_norm(h, g, b):
    mu = jnp.mean(h, axis=-1, keepdims=True)
    d = h - mu
    var = jnp.mean(d * d, axis=-1, keepdims=True)
    return d * lax.rsqrt(var + LN_EPS) * g + b


def _inproj_kernel(x_ref, *refs):
    n_act, n_forget, n_plain = INPROJ_BLOCKS
    w_refs = refs[:n_act + n_forget + n_plain]
    lb_ref, act_ref, logf_ref, key_ref, plain_ref = refs[n_act + n_forget + n_plain:]
    x = x_ref[...].astype(BF16)
    wb = W_IN_BLOCK
    z = [jnp.dot(x, w_ref[...], preferred_element_type=F32) for w_ref in w_refs[:n_act + n_forget]]
    for n, w_ref in enumerate(w_refs[n_act + n_forget:]):
        plain_ref[:, n * wb:(n + 1) * wb] = jnp.dot(x, w_ref[...], preferred_element_type=F32).astype(plain_ref.dtype)
    for n in range(n_act):
        act_ref[:, n * wb:(n + 1) * wb] = (z[n] * _sigmoid(z[n])).astype(act_ref.dtype)
    for n in range(n_forget):
        cols = slice(n * wb, (n + 1) * wb)
        lb = lb_ref[:, cols]
        sig = _sigmoid(z[n_act + n])
        logf_ref[:, cols] = jnp.log(lb + (1.0 - lb) * sig).astype(logf_ref.dtype)
        key_ref[:, cols] = ((1.0 - lb) * (1.0 - sig)).astype(key_ref.dtype)


def _inproj(x, w_in_bf, layer, lb):
    T, D = x.shape
    tm = min(T, INPROJ_ROWS)
    wb = W_IN_BLOCK
    hw = HGRN_WIDTH
    span = lambda start, width: list(range(start // wb, (start + width) // wb))
    attn_part = ATTN_HEADS * ATTN_HEAD_DIM
    act_blocks = span(HGRN_COLUMN, hw) + span(HGRN_COLUMN + 4 * hw, hw)
    forget_blocks = span(HGRN_COLUMN + 2 * hw, 2 * hw)
    plain_blocks = (span(0, D_MODEL) + span(HGRN_COLUMN + hw, hw)
                    + [(ATTN_COLUMN + part * attn_part) // wb for part in range(3)])
    blocks = act_blocks + forget_blocks + plain_blocks
    assert (len(act_blocks), len(forget_blocks), len(plain_blocks)) == INPROJ_BLOCKS
    wspec = lambda blk: pl.BlockSpec((None, D, wb), lambda i: (layer, 0, blk), pipeline_mode=pl.Buffered(1))
    ospec = lambda n: pl.BlockSpec((tm, n * wb), lambda i: (i, 0))
    oshape = lambda n: jax.ShapeDtypeStruct((T, n * wb), BF16)
    n_act, n_forget, n_plain = INPROJ_BLOCKS
    return pl.pallas_call(
        _inproj_kernel,
        grid=(T // tm,),
        in_specs=[pl.BlockSpec((tm, D), lambda i: (i, 0))] + [wspec(blk) for blk in blocks]
                 + [pl.BlockSpec((1, n_forget * wb), lambda i: (0, 0))],
        out_specs=[ospec(n_act), ospec(n_forget), ospec(n_forget), ospec(n_plain)],
        out_shape=[oshape(n_act), oshape(n_forget), oshape(n_forget), oshape(n_plain)],
        compiler_params=_params("parallel"),
        name="inproj",
    )(x, *([w_in_bf] * len(blocks)), lb)


def _inproj_dilated_kernel(*refs, dilation):
    nslab = D_MODEL // LANES
    x_refs = refs[:nslab]
    wq_ref, wk_ref, wv_ref, o_ref, xp_ref = refs[nslab:]
    n = x_refs[0].shape[0] // dilation
    for r in range(dilation):
        for c, x_ref in enumerate(x_refs):
            xp_ref[r * n:(r + 1) * n, c * LANES:(c + 1) * LANES] = (
                x_ref[pl.ds(r, n, stride=dilation), :].astype(BF16))
    xp = xp_ref[...]
    for part, w_ref in enumerate((wq_ref, wk_ref, wv_ref)):
        res = jnp.dot(xp, w_ref[...], preferred_element_type=F32).astype(o_ref.dtype)
        cols = slice(part * ATTN_GROUP_WIDTH, (part + 1) * ATTN_GROUP_WIDTH)
        for r in range(dilation):
            o_ref[0, r, :, cols] = res[r * n:(r + 1) * n]


def _inproj_dilated(x_f32, w_in_bf, layer, blocks, batch, seq, dilation):
    gw = ATTN_GROUP_WIDTH
    tb = min(seq, DILATED_ROWS)
    steps = seq // tb
    n = tb // dilation
    nslab = D_MODEL // LANES
    wspec = lambda blk: pl.BlockSpec((None, D_MODEL, gw), lambda i: (layer, 0, blk))
    xspec = lambda c: pl.BlockSpec((tb, LANES), lambda i: (i, c))
    return pl.pallas_call(
        functools.partial(_inproj_dilated_kernel, dilation=dilation),
        grid=(batch * steps,),
        in_specs=[xspec(c) for c in range(nslab)] + [wspec(blk) for blk in blocks],
        out_specs=pl.BlockSpec((1, dilation, n, 3 * gw), lambda i: (i // steps, 0, i % steps, 0)),
        out_shape=jax.ShapeDtypeStruct((batch, dilation, seq // dilation, 3 * gw), BF16),
        scratch_shapes=[pltpu.VMEM((tb, D_MODEL), BF16)],
        compiler_params=_params("parallel"),
        name="inproj_dilated%d" % dilation,
    )(*([x_f32] * nslab), w_in_bf, w_in_bf, w_in_bf)


def _attn_kernel(q_ref, k_ref, v_ref, o_ref, stat_ref, oacc_ref, sacc_ref, bias_ref, sa_ref, sb_ref, *,
                 seq_len, dilation, slopes):
    c = pl.program_id(1)
    lq = q_ref.shape[2]
    qb, kw = ATTN_QUERY_BLOCK, ATTN_KEY_WINDOW
    nq = lq // qb
    per_iter = ATTN_BLOCKS_PER_ITER
    scale2 = ATTN_HEAD_DIM ** -0.5 * LOG2_E
    heads = range(ATTN_HEADS_PER_GROUP)
    lanes = [slice(h * ATTN_HEAD_DIM, (h + 1) * ATTN_HEAD_DIM) for h in heads]

    @pl.when((pl.program_id(0) == 0) & (c == 0))
    def _():
        row = lax.broadcasted_iota(jnp.int32, (qb, kw), 0)
        col = lax.broadcasted_iota(jnp.int32, (qb, kw), 1)
        for w in range(ATTN_WINDOW_OFFSETS):
            dist = jnp.abs(col - row - w * ATTN_SIDE)
            distf = dist.astype(F32) * float(dilation)
            for h in heads:
                bias_ref[h, w] = jnp.where(dist <= ATTN_SIDE, distf * (-slopes[h] * LOG2_E), NEG_INF * LOG2_E)

    half_group = ATTN_STAT_LANES // 2
    sum_slot = lax.broadcasted_iota(jnp.int32, (kw, ATTN_HEAD_DIM), 1) // half_group
    ones_cols = [jnp.where(sum_slot == 2 * h + 1, 1.0, 0.0).astype(BF16) for h in heads]
    out_slot = lax.broadcasted_iota(jnp.int32, (qb, ATTN_HEAD_DIM), 1) // half_group

    def block_coords(idx):
        r = idx // nq
        r0 = pl.multiple_of((idx % nq) * qb, qb)
        t0 = c * lq + r0
        ks = pl.multiple_of(jnp.clip(t0 - ATTN_SIDE, 0, seq_len - kw), ATTN_SIDE)
        return r, r0, t0, ks

    def score_products(it, s_ref):
        for u in range(per_iter):
            r, r0, _, ks = block_coords(it * per_iter + u)
            q = q_ref[0, r, pl.ds(r0, qb), :]
            k = k_ref[0, r, pl.ds(ks, kw), :]
            for h in heads:
                s_ref[u * len(heads) + h] = lax.dot_general(
                    q[:, lanes[h]], k[:, lanes[h]], (((1,), (1,)), ((), ())), preferred_element_type=F32)

    n_iter = dilation * nq // per_iter
    score_products(0, sa_ref)

    def half_body(it, s_ref, next_ref):
        score_products(jnp.minimum(it + 1, n_iter - 1), next_ref)
        blocks = []
        for u in range(per_iter):
            r, r0, t0, ks = block_coords(it * per_iter + u)
            if dilation == 1:
                out_rows = pl.ds(r0, qb)
            else:
                out_rows = pl.ds(r0 * dilation + r, qb, stride=dilation)
            blocks.append(((t0 - ks) // ATTN_SIDE, out_rows, v_ref[0, r, pl.ds(ks, kw), :]))
        probs = []
        for u, (w, _, _) in enumerate(blocks):
            block_probs = []
            for h in heads:
                s = s_ref[u * len(heads) + h] * scale2 + bias_ref[h, w]
                m = jnp.max(s, axis=1, keepdims=True)
                block_probs.append((jnp.exp2(s - m).astype(BF16), m))
            probs.append(block_probs)
        pv = [[jnp.dot(block_probs[h][0], jnp.concatenate([v[:, lanes[h]], ones_cols[h]], axis=1),
                       preferred_element_type=F32) for h in heads]
              for (_, _, v), block_probs in zip(blocks, probs)]
        for (_, out_rows, _), block_probs, block_pv in zip(blocks, probs, pv):
            stats = block_pv[0][:, ATTN_HEAD_DIM:]
            for h in heads[1:]:
                stats = stats + block_pv[h][:, ATTN_HEAD_DIM:]
            for h in heads:
                stats = jnp.where(out_slot == 2 * h, block_probs[h][1], stats)
                oacc_ref[h, out_rows, :] = block_pv[h][:, :ATTN_HEAD_DIM]
            sacc_ref[out_rows, :] = stats

    def body(it2, carry):
        half_body(2 * it2, sa_ref, sb_ref)
        half_body(2 * it2 + 1, sb_ref, sa_ref)
        return carry

    lax.fori_loop(0, n_iter // 2, body, 0, unroll=True)
    for h in heads:
        o_ref[0, :, lanes[h]] = oacc_ref[h].astype(o_ref.dtype)
    stat_ref[0] = sacc_ref[...]


def _attention_group(q_arr, k_arr, v_arr, cols, batch, seq, dilation, slopes):
    L = seq // dilation
    gw = ATTN_GROUP_WIDTH
    tc = min(seq, ATTN_TOKENS_PER_STEP)
    lq = tc // dilation
    kern = functools.partial(_attn_kernel, seq_len=L, dilation=dilation, slopes=slopes)
    qc, kc, vc = cols
    stat = pl.BlockSpec((1, tc, ATTN_HEAD_DIM), lambda b, c: (b, c, 0))
    stat_shape = jax.ShapeDtypeStruct((batch, seq, ATTN_HEAD_DIM), F32)
    o, stats = pl.pallas_call(
        kern,
        grid=(batch, seq // tc),
        in_specs=[pl.BlockSpec((1, dilation, lq, gw), lambda b, c: (b, 0, c, qc)),
                  pl.BlockSpec((1, dilation, L, gw), lambda b, c: (b, 0, 0, kc)),
                  pl.BlockSpec((1, dilation, L, gw), lambda b, c: (b, 0, 0, vc))],
        out_specs=[pl.BlockSpec((1, tc, gw), lambda b, c: (b, c, 0)), stat],
        out_shape=[jax.ShapeDtypeStruct((batch, seq, gw), BF16), stat_shape],
        scratch_shapes=[pltpu.VMEM((ATTN_HEADS_PER_GROUP, tc, ATTN_HEAD_DIM), F32),
                        pltpu.VMEM((tc, ATTN_HEAD_DIM), F32),
                        pltpu.VMEM((ATTN_HEADS_PER_GROUP, ATTN_WINDOW_OFFSETS, ATTN_QUERY_BLOCK, ATTN_KEY_WINDOW), F32),
                        pltpu.VMEM((ATTN_BLOCKS_PER_ITER * ATTN_HEADS_PER_GROUP, ATTN_QUERY_BLOCK, ATTN_KEY_WINDOW), F32),
                        pltpu.VMEM((ATTN_BLOCKS_PER_ITER * ATTN_HEADS_PER_GROUP, ATTN_QUERY_BLOCK, ATTN_KEY_WINDOW), F32)],
        compiler_params=_params("arbitrary", "arbitrary"),
        name="attn_dil%d" % dilation,
    )(q_arr, k_arr, v_arr)
    T = batch * seq
    return o.reshape(T, gw), stats.reshape(T, ATTN_HEAD_DIM)


def _running_sum_rows(x, row_index, reverse):
    n = x.shape[0]
    shift = 1
    while shift < n:
        if reverse:
            x = x + jnp.where(row_index < n - shift, pltpu.roll(x, n - shift, 0), 0.0)
        else:
            x = x + jnp.where(row_index >= shift, pltpu.roll(x, shift, 0), 0.0)
        shift *= 2
    return x


def _hgrn_kernel(qf_ref, vf_ref, lf_ref, kf_ref, qb_ref, vb_ref, lb_ref, kb_ref, g_ref, nw_ref, o_ref,
                 accf_ref, accb_ref, sf_ref, sb_ref):
    c = pl.program_id(2)
    nc = pl.num_programs(2)
    chunk = qf_ref.shape[1]
    seq = o_ref.shape[1]
    hp = HGRN_HEADS_PER_STEP
    blk = HGRN_BLOCK
    half = blk // 2
    nblk = chunk // blk
    dh = HGRN_HEAD_DIM
    pw = 2 * dh

    @pl.when(c == 0)
    def _():
        sf_ref[...] = jnp.zeros_like(sf_ref)
        sb_ref[...] = jnp.zeros_like(sb_ref)

    row_index = lax.broadcasted_iota(jnp.int32, (blk, pw), 0)
    ti = lax.broadcasted_iota(jnp.int32, (blk, 2 * blk), 0)
    si = lax.broadcasted_iota(jnp.int32, (blk, 2 * blk), 1) % blk
    causal = si <= ti
    anticausal = si >= ti
    zeros_kv = jnp.zeros((blk, dh), BF16)
    zeros_st = jnp.zeros((dh, dh), BF16)

    def block_diag(a, b, z):
        return jnp.concatenate([jnp.concatenate([a, z], axis=1), jnp.concatenate([z, b], axis=1)], axis=0)

    def load_unit(pair, r0, q_ref, v_ref, logf_ref, key_ref, s_ref):
        lanes = slice(pair * pw, (pair + 1) * pw)
        rows = pl.ds(r0, blk)
        return (q_ref[0, rows, lanes], v_ref[0, rows, lanes], logf_ref[0, rows, lanes], key_ref[0, rows, lanes],
                s_ref[2 * pair], s_ref[2 * pair + 1])

    def prepare_unit(loaded, backward):
        q, v, logf, k, st0, st1 = loaded
        a = _running_sum_rows(logf.astype(F32), row_index, backward)
        if backward:
            total = a[0:1, :]
            mid = a[half:half + 1, :]
        else:
            total = a[blk - 1:blk, :]
            mid = a[half - 1:half, :]
        q_mid = q * jnp.exp(a - mid).astype(BF16)
        k_mid_b = k * jnp.exp(mid - a).astype(BF16)
        q_in = q_mid * jnp.exp(mid).astype(BF16)
        k_out = k_mid_b * jnp.exp(total - mid).astype(BF16)
        decay = jnp.broadcast_to(jnp.exp(total), (SUBLANES, pw)).T[:, 0:1]
        return dict(q_mid=q_mid, k_diag=block_diag(k_mid_b[:, :dh], k_mid_b[:, dh:], zeros_kv),
                    q_in=q_in, s_diag=block_diag(st0.astype(BF16), st1.astype(BF16), zeros_st),
                    v=v, v_diag=block_diag(v[:, :dh], v[:, dh:], zeros_kv), k_out=k_out,
                    decay=decay, states=(st0, st1), mask=anticausal if backward else causal)

    nt = (((1,), (1,)), ((), ()))
    tn = (((0,), (0,)), ((), ()))

    def body(n, carry):
        rf = pl.multiple_of(n * blk, blk)
        rb = pl.multiple_of((nblk - 1 - n) * blk, blk)
        units = []
        for pair in range(hp // 2):
            units.append((pair, False, load_unit(pair, rf, qf_ref, vf_ref, lf_ref, kf_ref, sf_ref)))
            units.append((pair, True, load_unit(pair, rb, qb_ref, vb_ref, lb_ref, kb_ref, sb_ref)))
        prepared = [prepare_unit(loaded, backward) for _, backward, loaded in units]
        att = [lax.dot_general(u["q_mid"], u["k_diag"], nt, preferred_element_type=F32) for u in prepared]
        inter = [jnp.dot(u["q_in"], u["s_diag"], preferred_element_type=F32) for u in prepared]
        upd = [[lax.dot_general(u["k_out"][:, j * dh:(j + 1) * dh], u["v"][:, j * dh:(j + 1) * dh], tn,
                                preferred_element_type=F32) for j in range(2)] for u in prepared]
        att = [jnp.where(u["mask"], s, 0.0).astype(BF16) for u, s in zip(prepared, att)]
        intra = [jnp.dot(s, u["v_diag"], preferred_element_type=F32) for u, s in zip(prepared, att)]
        for i, (pair, backward, _) in enumerate(units):
            u = prepared[i]
            lanes = slice(pair * pw, (pair + 1) * pw)
            o = intra[i] + inter[i]
            states = [st * u["decay"][j * dh:(j + 1) * dh, :] + upd[i][j] for j, st in enumerate(u["states"])]
            if backward:
                accb_ref[pl.ds(pl.multiple_of((nc - 1 - c) * chunk + rb, blk), blk), lanes] = o
                sb_ref[2 * pair], sb_ref[2 * pair + 1] = states
            else:
                accf_ref[pl.ds(pl.multiple_of(c * chunk + rf, blk), blk), lanes] = o
                sf_ref[2 * pair], sf_ref[2 * pair + 1] = states
        return carry

    lax.fori_loop(0, nblk, body, 0, unroll=True)

    @pl.when(c == nc - 1)
    def _():
        fin = min(seq, HGRN_FINISH_ROWS)

        def finish(t, carry):
            rows = pl.ds(pl.multiple_of(t * fin, fin), fin)
            for h in range(hp):
                lanes = slice(h * dh, (h + 1) * dh)
                o = accf_ref[rows, lanes] + accb_ref[rows, lanes]
                o = o * lax.rsqrt(jnp.mean(o * o, axis=-1, keepdims=True) + RMS_EPS)
                o = o * nw_ref[:, lanes] * g_ref[0, rows, lanes].astype(F32)
                o_ref[0, rows, lanes] = o.astype(o_ref.dtype)
            return carry

        lax.fori_loop(0, seq // fin, finish, 0, unroll=4)


def _hgrn(qg3, plain3, logf3, key3, norm_w):
    batch, seq, _ = qg3.shape
    hp = HGRN_HEADS_PER_STEP
    w = hp * HGRN_HEAD_DIM
    chunk = min(seq, HGRN_CHUNK)
    nc = seq // chunk

    def fwd(offset):
        base = offset // w
        return pl.BlockSpec((1, chunk, w), lambda b, h, c: (b, c, base + h))

    def bwd(offset):
        base = offset // w
        return pl.BlockSpec((1, chunk, w), lambda b, h, c: (b, nc - 1 - c, base + h))

    gbase = HGRN_WIDTH // w
    return pl.pallas_call(
        _hgrn_kernel,
        grid=(batch, HGRN_HEADS // hp, nc),
        in_specs=[fwd(0), fwd(PLAIN_HI), fwd(0), fwd(0),
                  bwd(0), bwd(PLAIN_HI), bwd(HGRN_WIDTH), bwd(HGRN_WIDTH),
                  pl.BlockSpec((1, seq, w), lambda b, h, c: (b, 0, gbase + h)),
                  pl.BlockSpec((1, w), lambda b, h, c: (0, h))],
        out_specs=pl.BlockSpec((1, seq, w), lambda b, h, c: (b, 0, h)),
        out_shape=jax.ShapeDtypeStruct((batch, seq, HGRN_WIDTH), BF16),
        scratch_shapes=[pltpu.VMEM((seq, w), F32), pltpu.VMEM((seq, w), F32),
                        pltpu.VMEM((hp, HGRN_HEAD_DIM, HGRN_HEAD_DIM), F32),
                        pltpu.VMEM((hp, HGRN_HEAD_DIM, HGRN_HEAD_DIM), F32)],
        compiler_params=_params("parallel", "parallel", "arbitrary"),
        name="hgrn",
    )(qg3, plain3, logf3, key3, qg3, plain3, logf3, key3, qg3, norm_w)


def _pool_kernel(prev_ref, cur_ref, next_ref, pw_ref, ps_ref, o_ref, *, seq_len):
    i = pl.program_id(1)
    last = pl.num_programs(1) - 1
    ts = cur_ref.shape[1]
    n = ts + 2 * POOL_HALO
    pos = i * ts + lax.broadcasted_iota(jnp.int32, (ts, 1), 0)
    for g in range(POOL_GROUPS):
        lanes = slice(g * POOL_GROUP_DIM, (g + 1) * POOL_GROUP_DIM)
        half = POOL_WINDOWS[g] // 2
        cur = cur_ref[0, :, lanes].astype(F32)
        prev = jnp.where(i > 0, prev_ref[0, :, lanes].astype(F32), 0.0)
        nxt = jnp.where(i < last, next_ref[0, :, lanes].astype(F32), 0.0)
        ext = jnp.concatenate([prev, cur, nxt], axis=0)
        w = ext + pltpu.roll(ext, 1, 0)
        step = 1
        while step < half:
            w = pltpu.roll(w, step, 0) + pltpu.roll(w, n - step, 0)
            step *= 2
        wsum = w[POOL_HALO:POOL_HALO + ts]
        count = (jnp.minimum(pos + half, seq_len) - jnp.maximum(pos - half, 0)).astype(F32)
        mixed = wsum / count - cur
        y = jnp.dot(mixed.astype(BF16), pw_ref[g], preferred_element_type=F32)
        o_ref[0, :, lanes] = (y * ps_ref[:, lanes]).astype(o_ref.dtype)


def _pool(plain3, pool_w, layer, pool_scale):
    batch, seq, _ = plain3.shape
    ts = min(seq, POOL_ROWS)
    hb = ts // POOL_HALO
    nhalo = seq // POOL_HALO
    kern = functools.partial(_pool_kernel, seq_len=seq)
    return pl.pallas_call(
        kern,
        grid=(batch, seq // ts),
        in_specs=[pl.BlockSpec((1, POOL_HALO, D_MODEL), lambda b, i: (b, jnp.maximum(i * hb - 1, 0), 0)),
                  pl.BlockSpec((1, ts, D_MODEL), lambda b, i: (b, i, 0)),
                  pl.BlockSpec((1, POOL_HALO, D_MODEL), lambda b, i: (b, jnp.minimum((i + 1) * hb, nhalo - 1), 0)),
                  pl.BlockSpec((None, POOL_GROUPS, POOL_GROUP_DIM, POOL_GROUP_DIM), lambda b, i: (layer, 0, 0, 0)),
                  pl.BlockSpec((1, D_MODEL), lambda b, i: (0, 0))],
        out_specs=pl.BlockSpec((1, ts, D_MODEL), lambda b, i: (b, i, 0)),
        out_shape=jax.ShapeDtypeStruct((batch, seq, D_MODEL), BF16),
        compiler_params=_params("parallel", "parallel"),
        name="pool",
    )(plain3, plain3, plain3, pool_w, pool_scale)


def _merge_kernel(ya_ref, o1_ref, o2_ref, o3_ref, s1_ref, s2_ref, s3_ref, hc_ref,
                  xin_ref, x_ref, wa_ref, wb_ref, wc_ref, wo_ref, *rest, alpha, nsub):
    wgate_refs = rest[:-4]
    g_ref, b_ref, xo_ref, xb_ref = rest[-4:]
    d = D_MODEL
    dh = ATTN_HEAD_DIM
    sub = x_ref.shape[0] // nsub
    rows = [slice(s * sub, (s + 1) * sub) for s in range(nsub)]
    xin = [xin_ref[r, :].astype(BF16) for r in rows]
    y_a = [jnp.dot(ya_ref[r, :], wa_ref[...], preferred_element_type=F32) for r in rows]
    y_c = [jnp.dot(hc_ref[r, :], wc_ref[...], preferred_element_type=F32) for r in rows]
    per_gate = d // W_IN_BLOCK
    gates = [[jnp.concatenate([jnp.dot(x, w_ref[...], preferred_element_type=F32)
                               for w_ref in wgate_refs[n * per_gate:(n + 1) * per_gate]], axis=1)
              for n in range(3)] for x in xin]
    att = []
    for r in rows:
        ms = [s1_ref[r, :], s2_ref[r, :], s3_ref[r, :]]
        ls = [pltpu.roll(m, LANES - ATTN_STAT_LANES // 2, 1) for m in ms]
        top = jnp.maximum(jnp.maximum(ms[0], ms[1]), ms[2])
        es = [jnp.exp2(m - top) for m in ms]
        inv = 1.0 / (es[0] * ls[0] + es[1] * ls[1] + es[2] * ls[2])
        ws = [e * inv for e in es]
        heads = []
        for h in range(ATTN_HEADS_PER_GROUP):
            lanes = slice(h * dh, (h + 1) * dh)
            stat = slice(h * ATTN_STAT_LANES, h * ATTN_STAT_LANES + 1)
            acc = None
            for w, o_ref in zip(ws, (o1_ref, o2_ref, o3_ref)):
                term = jnp.broadcast_to(w[:, stat], (sub, dh)) * o_ref[r, lanes].astype(F32)
                acc = term if acc is None else acc + term
            heads.append(acc)
        att.append(jnp.concatenate(heads, axis=1).astype(BF16))
    y_b = [jnp.dot(a, wb_ref[...], preferred_element_type=F32) for a in att]
    merged = [(_sigmoid(gates[s][0]) * y_a[s] + _sigmoid(gates[s][1]) * y_b[s]
               + _sigmoid(gates[s][2]) * y_c[s]).astype(BF16) for s in range(nsub)]
    mix = [jnp.dot(mg, wo_ref[...], preferred_element_type=F32) for mg in merged]
    for s, r in enumerate(rows):
        x1 = _layer_norm(alpha * x_ref[r, :] + mix[s], g_ref[...], b_ref[...])
        xo_ref[r, :] = x1
        xb_ref[r, :] = x1.astype(BF16)


def _merge(y_pool, o_list, stat_list, y_hgrn, x_bf, x, wa, wb, wc, wo, w_in_bf, layer, ln_g, ln_b, alpha):
    T = x.shape[0]
    tm = min(T, MERGE_ROWS)
    gw = ATTN_GROUP_WIDTH
    d = D_MODEL
    row = lambda w: pl.BlockSpec((tm, w), lambda i: (i, 0))
    const = lambda a: pl.BlockSpec(a.shape, lambda i: (0,) * a.ndim, pipeline_mode=pl.Buffered(1))
    kern = functools.partial(_merge_kernel, alpha=alpha, nsub=MERGE_SUBTILES)
    stat = row(ATTN_HEAD_DIM)
    gate_blocks = range(GATES_COLUMN // W_IN_BLOCK, (GATES_COLUMN + 3 * d) // W_IN_BLOCK)
    gate_specs = [pl.BlockSpec((None, d, W_IN_BLOCK), lambda i, blk=blk: (layer, 0, blk),
                               pipeline_mode=pl.Buffered(1)) for blk in gate_blocks]
    return pl.pallas_call(
        kern,
        grid=(T // tm,),
        in_specs=[row(d), row(gw), row(gw), row(gw), stat, stat, stat, row(d), row(d),
                  row(d)] + [_layer_spec(w, layer) for w in (wa, wb, wc, wo)] + gate_specs
                 + [const(ln_g), const(ln_b)],
        out_specs=[row(d), row(d)],
        out_shape=[jax.ShapeDtypeStruct((T, d), F32), jax.ShapeDtypeStruct((T, d), BF16)],
        compiler_params=_params("parallel"),
        name="merge",
    )(y_pool, *o_list, *stat_list, y_hgrn, x_bf, x, wa, wb, wc, wo, *([w_in_bf] * len(gate_specs)), ln_g, ln_b)


def _ffn_kernel(xb_ref, x_ref, p_ref, wg_ref, wu_ref, wd_ref, wpp_ref, wpg_ref, g_ref, b_ref,
                xo_ref, xbo_ref, *, alpha, nsub):
    sub = xb_ref.shape[0] // nsub
    rows = [slice(s * sub, (s + 1) * sub) for s in range(nsub)]
    xb = [xb_ref[r, :] for r in rows]
    gate = [jnp.dot(x, wg_ref[...], preferred_element_type=F32) for x in xb]
    up = [jnp.dot(x, wu_ref[...], preferred_element_type=F32) for x in xb]
    pgate = [jnp.dot(x, wpg_ref[...], preferred_element_type=F32) for x in xb]
    ple = [jnp.dot(p_ref[r, :].astype(BF16), wpp_ref[...], preferred_element_type=F32) for r in rows]
    hidden = [(g * _sigmoid(g) * u).astype(BF16) for g, u in zip(gate, up)]
    ffn = [jnp.dot(h, wd_ref[...], preferred_element_type=F32) for h in hidden]
    for s, r in enumerate(rows):
        x2 = _layer_norm(alpha * x_ref[r, :] + ffn[s] + ple[s] * _sigmoid(pgate[s]), g_ref[...], b_ref[...])
        xo_ref[r, :] = x2
        xbo_ref[r, :] = x2.astype(BF16)


def _ffn(x1_bf, x1, p, wg, wu, wd, wpp, wpg, layer, ln_g, ln_b, alpha):
    T = x1.shape[0]
    tm = min(T, FFN_ROWS)
    d = D_MODEL
    row = lambda w: pl.BlockSpec((tm, w), lambda i: (i, 0))
    const = lambda a: pl.BlockSpec(a.shape, lambda i: (0,) * a.ndim, pipeline_mode=pl.Buffered(1))
    weight = lambda a: _layer_spec(a, layer)
    kern = functools.partial(_ffn_kernel, alpha=alpha, nsub=FFN_SUBTILES)
    return pl.pallas_call(
        kern,
        grid=(T // tm,),
        in_specs=[row(d), row(d), pl.BlockSpec((None, tm, PLE_DIM), lambda i: (layer, i, 0)),
                  weight(wg), weight(wu), weight(wd), weight(wpp), weight(wpg), const(ln_g), const(ln_b)],
        out_specs=[row(d), row(d)],
        out_shape=[jax.ShapeDtypeStruct((T, d), F32), jax.ShapeDtypeStruct((T, d), BF16)],
        compiler_params=_params("parallel"),
        name="ffn",
    )(x1_bf, x1, p, wg, wu, wd, wpp, wpg, ln_g, ln_b)


def kernel(x, p, w_in, pool_w, pool_scale, w_branch_a, w_branch_b, w_branch_c, hgrn_lb_logits, hgrn_norm_w, w_out, ln1_g, ln1_b, w_ffn_gate, w_ffn_up, w_ffn_down, w_ple_proj, w_ple_gate, ln2_g, ln2_b):
    batch, seq, d = x.shape
    depth = w_in.shape[0]
    T = batch * seq
    alpha = float((2 * depth) ** 0.25)

    lb = jnp.cumsum(jax.nn.softmax(hgrn_lb_logits.astype(F32), axis=0), axis=0)
    lb = lb - lb[:1]

    w_in_bf = w_in.astype(BF16)
    bf = lambda a: a.astype(BF16)
    pool_w_b, wa, wb, wc, wo = bf(pool_w), bf(w_branch_a), bf(w_branch_b), bf(w_branch_c), bf(w_out)
    wg, wu, wd, wpp, wpg = bf(w_ffn_gate), bf(w_ffn_up), bf(w_ffn_down), bf(w_ple_proj), bf(w_ple_gate)

    slopes = [2.0 ** (-ALIBI_MAX_BIAS * (i + 1) / ATTN_HEADS) for i in range(ATTN_HEADS)]
    hpg = ATTN_HEADS_PER_GROUP
    gw = ATTN_GROUP_WIDTH

    xf = x.reshape(T, d).astype(F32)
    xb = xf
    p_rows = p.reshape(depth, T, PLE_DIM)
    for i in range(depth):
        qg, logf, key, plain = _inproj(xb, w_in_bf, i, lb[i:i + 1])
        plain3 = plain.reshape(batch, seq, PLAIN_WIDTH)
        o_list, stat_list = [], []
        for g, (_, dil) in enumerate(ATTN_GROUPS):
            gs = tuple(slopes[g * hpg:(g + 1) * hpg])
            if dil == 1:
                qkv = plain.reshape(batch, 1, seq, PLAIN_WIDTH)
                cols = (PLAIN_AQ // gw, PLAIN_AK // gw, PLAIN_AV // gw)
            else:
                blocks = [(ATTN_COLUMN + part * ATTN_HEADS * ATTN_HEAD_DIM) // W_IN_BLOCK + g for part in range(3)]
                qkv = _inproj_dilated(xf, w_in_bf, i, blocks, batch, seq, dil)
                cols = (0, 1, 2)
            o, stats = _attention_group(qkv, qkv, qkv, cols, batch, seq, dil, gs)
            o_list.append(o)
            stat_list.append(stats)
        y_hgrn = _hgrn(qg.reshape(batch, seq, 2 * HGRN_WIDTH), plain3,
                       logf.reshape(batch, seq, 2 * HGRN_WIDTH), key.reshape(batch, seq, 2 * HGRN_WIDTH),
                       hgrn_norm_w[i:i + 1])
        y_pool = _pool(plain3, pool_w_b, i, pool_scale[i:i + 1])
        x1, x1b = _merge(y_pool.reshape(T, d), o_list, stat_list, y_hgrn.reshape(T, d), xb, xf,
                         wa, wb, wc, wo, w_in_bf, i, ln1_g[i:i + 1], ln1_b[i:i + 1], alpha)
        xf, xb = _ffn(x1b, x1, p_rows, wg, wu, wd, wpp, wpg, i, ln2_g[i:i + 1], ln2_b[i:i + 1], alpha)
    return xf.reshape(batch, seq, d).astype(x.dtype)
```

```python
import functools

import jax
import jax.numpy as jnp
from jax import lax
from jax.experimental import pallas as pl
from jax.experimental.pallas import tpu as pltpu

F32 = jnp.float32
BF16 = jnp.bfloat16
LANES = 128
SUBLANES = 8
LOG2_E = 1.4426950408889634

D_MODEL = 1024
PLE_DIM = 256
POOL_WINDOWS = (2, 4, 8, 16)
POOL_GROUPS = 4
POOL_GROUP_DIM = D_MODEL // POOL_GROUPS
ATTN_GROUPS = ((128, 1), (512, 4), (2048, 16))
ATTN_HEADS_PER_GROUP = 4
ATTN_HEADS = ATTN_HEADS_PER_GROUP * len(ATTN_GROUPS)
ATTN_HEAD_DIM = 128
ATTN_GROUP_WIDTH = ATTN_HEADS_PER_GROUP * ATTN_HEAD_DIM
ATTN_SIDE = 64
ALIBI_MAX_BIAS = 8.0
NEG_INF = -1e30
HGRN_HEADS = 8
HGRN_HEAD_DIM = 128
HGRN_WIDTH = HGRN_HEADS * HGRN_HEAD_DIM
LN_EPS = 1e-5
RMS_EPS = 1e-6

W_IN_BLOCK = 512
ATTN_COLUMN = D_MODEL
HGRN_COLUMN = ATTN_COLUMN + 3 * ATTN_HEADS * ATTN_HEAD_DIM
GATES_COLUMN = HGRN_COLUMN + 5 * HGRN_WIDTH
PLAIN_HI = 1024
PLAIN_AQ = 2048
PLAIN_AK = 2560
PLAIN_AV = 3072
PLAIN_WIDTH = 3584

INPROJ_ROWS = 256
INPROJ_BLOCKS = (4, 4, 7)
DILATED_ROWS = 1024
ATTN_QUERY_BLOCK = 128
ATTN_KEY_WINDOW = ATTN_QUERY_BLOCK + 2 * ATTN_SIDE
ATTN_TOKENS_PER_STEP = 2048
ATTN_BLOCKS_PER_ITER = 2
ATTN_STAT_LANES = 32
ATTN_WINDOW_OFFSETS = 3
HGRN_BLOCK = 64
HGRN_HEADS_PER_STEP = 4
HGRN_FINISH_ROWS = 256
HGRN_CHUNK = 512
MERGE_ROWS = 512
MERGE_SUBTILES = 2
FFN_ROWS = 512
FFN_SUBTILES = 2
POOL_ROWS = 512
POOL_HALO = 16
VMEM_LIMIT = 52 * 1024 * 1024


def _params(*semantics):
    return pltpu.CompilerParams(dimension_semantics=semantics, vmem_limit_bytes=VMEM_LIMIT)


def _layer_spec(stacked, layer):
    tail = stacked.shape[1:]
    return pl.BlockSpec((None,) + tail, lambda i: (layer,) + (0,) * len(tail), pipeline_mode=pl.Buffered(1))


def _sigmoid(x):
    return 0.5 * jnp.tanh(0.5 * x) + 0.5


def _layer_norm(h, g, b):
    mu = jnp.mean(h, axis=-1, keepdims=True)
    d = h - mu
    var = jnp.mean(d * d, axis=-1, keepdims=True)
    return d * lax.rsqrt(var + LN_EPS) * g + b


def _inproj_kernel(x_ref, *refs):
    n_act, n_forget, n_plain = INPROJ_BLOCKS
    w_refs = refs[:n_act + n_forget + n_plain]
    lb_ref, act_ref, logf_ref, key_ref, plain_ref = refs[n_act + n_forget + n_plain:]
    x = x_ref[...].astype(BF16)
    wb = W_IN_BLOCK
    z = [jnp.dot(x, w_ref[...], preferred_element_type=F32) for w_ref in w_refs[:n_act + n_forget]]
    for n, w_ref in enumerate(w_refs[n_act + n_forget:]):
        plain_ref[:, n * wb:(n + 1) * wb] = jnp.dot(x, w_ref[...], preferred_element_type=F32).astype(plain_ref.dtype)
    for n in range(n_act):
        act_ref[:, n * wb:(n + 1) * wb] = (z[n] * _sigmoid(z[n])).astype(act_ref.dtype)
    for n in range(n_forget):
        cols = slice(n * wb, (n + 1) * wb)
        lb = lb_ref[:, cols]
        sig = _sigmoid(z[n_act + n])
        logf_ref[:, cols] = jnp.log(lb + (1.0 - lb) * sig).astype(logf_ref.dtype)
        key_ref[:, cols] = ((1.0 - lb) * (1.0 - sig)).astype(key_ref.dtype)


def _inproj(x, w_in_bf, layer, lb):
    T, D = x.shape
    tm = min(T, INPROJ_ROWS)
    wb = W_IN_BLOCK
    hw = HGRN_WIDTH
    span = lambda start, width: list(range(start // wb, (start + width) // wb))
    attn_part = ATTN_HEADS * ATTN_HEAD_DIM
    act_blocks = span(HGRN_COLUMN, hw) + span(HGRN_COLUMN + 4 * hw, hw)
    forget_blocks = span(HGRN_COLUMN + 2 * hw, 2 * hw)
    plain_blocks = (span(0, D_MODEL) + span(HGRN_COLUMN + hw, hw)
                    + [(ATTN_COLUMN + part * attn_part) // wb for part in range(3)])
    blocks = act_blocks + forget_blocks + plain_blocks
    assert (len(act_blocks), len(forget_blocks), len(plain_blocks)) == INPROJ_BLOCKS
    wspec = lambda blk: pl.BlockSpec((None, D, wb), lambda i: (layer, 0, blk), pipeline_mode=pl.Buffered(1))
    ospec = lambda n: pl.BlockSpec((tm, n * wb), lambda i: (i, 0))
    oshape = lambda n: jax.ShapeDtypeStruct((T, n * wb), BF16)
    n_act, n_forget, n_plain = INPROJ_BLOCKS
    return pl.pallas_call(
        _inproj_kernel,
        grid=(T // tm,),
        in_specs=[pl.BlockSpec((tm, D), lambda i: (i, 0))] + [wspec(blk) for blk in blocks]
                 + [pl.BlockSpec((1, n_forget * wb), lambda i: (0, 0))],
        out_specs=[ospec(n_act), ospec(n_forget), ospec(n_forget), ospec(n_plain)],
        out_shape=[oshape(n_act), oshape(n_forget), oshape(n_forget), oshape(n_plain)],
        compiler_params=_params("parallel"),
        name="inproj",
    )(x, *([w_in_bf] * len(blocks)), lb)


def _inproj_dilated_kernel(*refs, dilation):
    nslab = D_MODEL // LANES
    x_refs = refs[:nslab]
    wq_ref, wk_ref, wv_ref, o_ref, xp_ref = refs[nslab:]
    n = x_refs[0].shape[0] // dilation
    for r in range(dilation):
        for c, x_ref in enumerate(x_refs):
            xp_ref[r * n:(r + 1) * n, c * LANES:(c + 1) * LANES] = (
                x_ref[pl.ds(r, n, stride=dilation), :].astype(BF16))
    xp = xp_ref[...]
    for part, w_ref in enumerate((wq_ref, wk_ref, wv_ref)):
        res = jnp.dot(xp, w_ref[...], preferred_element_type=F32).astype(o_ref.dtype)
        cols = slice(part * ATTN_GROUP_WIDTH, (part + 1) * ATTN_GROUP_WIDTH)
        for r in range(dilation):
            o_ref[0, r, :, cols] = res[r * n:(r + 1) * n]


def _inproj_dilated(x_f32, w_in_bf, layer, blocks, batch, seq, dilation):
    gw = ATTN_GROUP_WIDTH
    tb = min(seq, DILATED_ROWS)
    steps = seq // tb
    n = tb // dilation
    nslab = D_MODEL // LANES
    wspec = lambda blk: pl.BlockSpec((None, D_MODEL, gw), lambda i: (layer, 0, blk))
    xspec = lambda c: pl.BlockSpec((tb, LANES), lambda i: (i, c))
    return pl.pallas_call(
        functools.partial(_inproj_dilated_kernel, dilation=dilation),
        grid=(batch * steps,),
        in_specs=[xspec(c) for c in range(nslab)] + [wspec(blk) for blk in blocks],
        out_specs=pl.BlockSpec((1, dilation, n, 3 * gw), lambda i: (i // steps, 0, i % steps, 0)),
        out_shape=jax.ShapeDtypeStruct((batch, dilation, seq // dilation, 3 * gw), BF16),
        scratch_shapes=[pltpu.VMEM((tb, D_MODEL), BF16)],
        compiler_params=_params("parallel"),
        name="inproj_dilated%d" % dilation,
    )(*([x_f32] * nslab), w_in_bf, w_in_bf, w_in_bf)


def _attn_kernel(q_ref, k_ref, v_ref, o_ref, stat_ref, oacc_ref, sacc_ref, bias_ref, sa_ref, sb_ref, *,
                 seq_len, dilation, slopes):
    c = pl.program_id(1)
    lq = q_ref.shape[2]
    qb, kw = ATTN_QUERY_BLOCK, ATTN_KEY_WINDOW
    nq = lq // qb
    per_iter = ATTN_BLOCKS_PER_ITER
    scale2 = ATTN_HEAD_DIM ** -0.5 * LOG2_E
    heads = range(ATTN_HEADS_PER_GROUP)
    lanes = [slice(h * ATTN_HEAD_DIM, (h + 1) * ATTN_HEAD_DIM) for h in heads]

    @pl.when((pl.program_id(0) == 0) & (c == 0))
    def _():
        row = lax.broadcasted_iota(jnp.int32, (qb, kw), 0)
        col = lax.broadcasted_iota(jnp.int32, (qb, kw), 1)
        for w in range(ATTN_WINDOW_OFFSETS):
            dist = jnp.abs(col - row - w * ATTN_SIDE)
            distf = dist.astype(F32) * float(dilation)
            for h in heads:
                bias_ref[h, w] = jnp.where(dist <= ATTN_SIDE, distf * (-slopes[h] * LOG2_E), NEG_INF * LOG2_E)

    half_group = ATTN_STAT_LANES // 2
    sum_slot = lax.broadcasted_iota(jnp.int32, (kw, ATTN_HEAD_DIM), 1) // half_group
    ones_cols = [jnp.where(sum_slot == 2 * h + 1, 1.0, 0.0).astype(BF16) for h in heads]
    out_slot = lax.broadcasted_iota(jnp.int32, (qb, ATTN_HEAD_DIM), 1) // half_group

    def block_coords(idx):
        r = idx // nq
        r0 = pl.multiple_of((idx % nq) * qb, qb)
        t0 = c * lq + r0
        ks = pl.multiple_of(jnp.clip(t0 - ATTN_SIDE, 0, seq_len - kw), ATTN_SIDE)
        return r, r0, t0, ks

    def score_products(it, s_ref):
        for u in range(per_iter):
            r, r0, _, ks = block_coords(it * per_iter + u)
            q = q_ref[0, r, pl.ds(r0, qb), :]
            k = k_ref[0, r, pl.ds(ks, kw), :]
            for h in heads:
                s_ref[u * len(heads) + h] = lax.dot_general(
                    q[:, lanes[h]], k[:, lanes[h]], (((1,), (1,)), ((), ())), preferred_element_type=F32)

    n_iter = dilation * nq // per_iter
    score_products(0, sa_ref)

    def half_body(it, s_ref, next_ref):
        score_products(jnp.minimum(it + 1, n_iter - 1), next_ref)
        blocks = []
        for u in range(per_iter):
            r, r0, t0, ks = block_coords(it * per_iter + u)
            if dilation == 1:
                out_rows = pl.ds(r0, qb)
            else:
                out_rows = pl.ds(r0 * dilation + r, qb, stride=dilation)
            blocks.append(((t0 - ks) // ATTN_SIDE, out_rows, v_ref[0, r, pl.ds(ks, kw), :]))
        probs = []
        for u, (w, _, _) in enumerate(blocks):
            block_probs = []
            for h in heads:
                s = s_ref[u * len(heads) + h] * scale2 + bias_ref[h, w]
                m = jnp.max(s, axis=1, keepdims=True)
                block_probs.append((jnp.exp2(s - m).astype(BF16), m))
            probs.append(block_probs)
        pv = [[jnp.dot(block_probs[h][0], jnp.concatenate([v[:, lanes[h]], ones_cols[h]], axis=1),
                       preferred_element_type=F32) for h in heads]
              for (_, _, v), block_probs in zip(blocks, probs)]
        for (_, out_rows, _), block_probs, block_pv in zip(blocks, probs, pv):
            stats = block_pv[0][:, ATTN_HEAD_DIM:]
            for h in heads[1:]:
                stats = stats + block_pv[h][:, ATTN_HEAD_DIM:]
            for h in heads:
                stats = jnp.where(out_slot == 2 * h, block_probs[h][1], stats)
                oacc_ref[h, out_rows, :] = block_pv[h][:, :ATTN_HEAD_DIM]
            sacc_ref[out_rows, :] = stats

    def body(it2, carry):
        half_body(2 * it2, sa_ref, sb_ref)
        half_body(2 * it2 + 1, sb_ref, sa_ref)
        return carry

    lax.fori_loop(0, n_iter // 2, body, 0, unroll=True)
    for h in heads:
        o_ref[0, :, lanes[h]] = oacc_ref[h].astype(o_ref.dtype)
    stat_ref[0] = sacc_ref[...]


def _attention_group(q_arr, k_arr, v_arr, cols, batch, seq, dilation, slopes):
    L = seq // dilation
    gw = ATTN_GROUP_WIDTH
    tc = min(seq, ATTN_TOKENS_PER_STEP)
    lq = tc // dilation
    kern = functools.partial(_attn_kernel, seq_len=L, dilation=dilation, slopes=slopes)
    qc, kc, vc = cols
    stat = pl.BlockSpec((1, tc, ATTN_HEAD_DIM), lambda b, c: (b, c, 0))
    stat_shape = jax.ShapeDtypeStruct((batch, seq, ATTN_HEAD_DIM), F32)
    o, stats = pl.pallas_call(
        kern,
        grid=(batch, seq // tc),
        in_specs=[pl.BlockSpec((1, dilation, lq, gw), lambda b, c: (b, 0, c, qc)),
                  pl.BlockSpec((1, dilation, L, gw), lambda b, c: (b, 0, 0, kc)),
                  pl.BlockSpec((1, dilation, L, gw), lambda b, c: (b, 0, 0, vc))],
        out_specs=[pl.BlockSpec((1, tc, gw), lambda b, c: (b, c, 0)), stat],
        out_shape=[jax.ShapeDtypeStruct((batch, seq, gw), BF16), stat_shape],
        scratch_shapes=[pltpu.VMEM((ATTN_HEADS_PER_GROUP, tc, ATTN_HEAD_DIM), F32),
                        pltpu.VMEM((tc, ATTN_HEAD_DIM), F32),
                        pltpu.VMEM((ATTN_HEADS_PER_GROUP, ATTN_WINDOW_OFFSETS, ATTN_QUERY_BLOCK, ATTN_KEY_WINDOW), F32),
                        pltpu.VMEM((ATTN_BLOCKS_PER_ITER * ATTN_HEADS_PER_GROUP, ATTN_QUERY_BLOCK, ATTN_KEY_WINDOW), F32),
                        pltpu.VMEM((ATTN_BLOCKS_PER_ITER * ATTN_HEADS_PER_GROUP, ATTN_QUERY_BLOCK, ATTN_KEY_WINDOW), F32)],
        compiler_params=_params("arbitrary", "arbitrary"),
        name="attn_dil%d" % dilation,
    )(q_arr, k_arr, v_arr)
    T = batch * seq
    return o.reshape(T, gw), stats.reshape(T, ATTN_HEAD_DIM)


def _running_sum_rows(x, row_index, reverse):
    n = x.shape[0]
    shift = 1
    while shift < n:
        if reverse:
            x = x + jnp.where(row_index < n - shift, pltpu.roll(x, n - shift, 0), 0.0)
        else:
            x = x + jnp.where(row_index >= shift, pltpu.roll(x, shift, 0), 0.0)
        shift *= 2
    return x


def _hgrn_kernel(qf_ref, vf_ref, lf_ref, kf_ref, qb_ref, vb_ref, lb_ref, kb_ref, g_ref, nw_ref, o_ref,
                 accf_ref, accb_ref, sf_ref, sb_ref):
    c = pl.program_id(2)
    nc = pl.num_programs(2)
    chunk = qf_ref.shape[1]
    seq = o_ref.shape[1]
    hp = HGRN_HEADS_PER_STEP
    blk = HGRN_BLOCK
    half = blk // 2
    nblk = chunk // blk
    dh = HGRN_HEAD_DIM
    pw = 2 * dh

    @pl.when(c == 0)
    def _():
        sf_ref[...] = jnp.zeros_like(sf_ref)
        sb_ref[...] = jnp.zeros_like(sb_ref)

    row_index = lax.broadcasted_iota(jnp.int32, (blk, pw), 0)
    ti = lax.broadcasted_iota(jnp.int32, (blk, 2 * blk), 0)
    si = lax.broadcasted_iota(jnp.int32, (blk, 2 * blk), 1) % blk
    causal = si <= ti
    anticausal = si >= ti
    zeros_kv = jnp.zeros((blk, dh), BF16)
    zeros_st = jnp.zeros((dh, dh), BF16)

    def block_diag(a, b, z):
        return jnp.concatenate([jnp.concatenate([a, z], axis=1), jnp.concatenate([z, b], axis=1)], axis=0)

    def load_unit(pair, r0, q_ref, v_ref, logf_ref, key_ref, s_ref):
        lanes = slice(pair * pw, (pair + 1) * pw)
        rows = pl.ds(r0, blk)
        return (q_ref[0, rows, lanes], v_ref[0, rows, lanes], logf_ref[0, rows, lanes], key_ref[0, rows, lanes],
                s_ref[2 * pair], s_ref[2 * pair + 1])

    def prepare_unit(loaded, backward):
        q, v, logf, k, st0, st1 = loaded
        a = _running_sum_rows(logf.astype(F32), row_index, backward)
        if backward:
            total = a[0:1, :]
            mid = a[half:half + 1, :]
        else:
            total = a[blk - 1:blk, :]
            mid = a[half - 1:half, :]
        q_mid = q * jnp.exp(a - mid).astype(BF16)
        k_mid_b = k * jnp.exp(mid - a).astype(BF16)
        q_in = q_mid * jnp.exp(mid).astype(BF16)
        k_out = k_mid_b * jnp.exp(total - mid).astype(BF16)
        decay = jnp.broadcast_to(jnp.exp(total), (SUBLANES, pw)).T[:, 0:1]
        return dict(q_mid=q_mid, k_diag=block_diag(k_mid_b[:, :dh], k_mid_b[:, dh:], zeros_kv),
                    q_in=q_in, s_diag=block_diag(st0.astype(BF16), st1.astype(BF16), zeros_st),
                    v=v, v_diag=block_diag(v[:, :dh], v[:, dh:], zeros_kv), k_out=k_out,
                    decay=decay, states=(st0, st1), mask=anticausal if backward else causal)

    nt = (((1,), (1,)), ((), ()))
    tn = (((0,), (0,)), ((), ()))

    def body(n, carry):
        rf = pl.multiple_of(n * blk, blk)
        rb = pl.multiple_of((nblk - 1 - n) * blk, blk)
        units = []
        for pair in range(hp // 2):
            units.append((pair, False, load_unit(pair, rf, qf_ref, vf_ref, lf_ref, kf_ref, sf_ref)))
            units.append((pair, True, load_unit(pair, rb, qb_ref, vb_ref, lb_ref, kb_ref, sb_ref)))
        prepared = [prepare_unit(loaded, backward) for _, backward, loaded in units]
        att = [lax.dot_general(u["q_mid"], u["k_diag"], nt, preferred_element_type=F32) for u in prepared]
        inter = [jnp.dot(u["q_in"], u["s_diag"], preferred_element_type=F32) for u in prepared]
        upd = [[lax.dot_general(u["k_out"][:, j * dh:(j + 1) * dh], u["v"][:, j * dh:(j + 1) * dh], tn,
                                preferred_element_type=F32) for j in range(2)] for u in prepared]
        att = [jnp.where(u["mask"], s, 0.0).astype(BF16) for u, s in zip(prepared, att)]
        intra = [jnp.dot(s, u["v_diag"], preferred_element_type=F32) for u, s in zip(prepared, att)]
        for i, (pair, backward, _) in enumerate(units):
            u = prepared[i]
            lanes = slice(pair * pw, (pair + 1) * pw)
            o = intra[i] + inter[i]
            states = [st * u["decay"][j * dh:(j + 1) * dh, :] + upd[i][j] for j, st in enumerate(u["states"])]
            if backward:
                accb_ref[pl.ds(pl.multiple_of((nc - 1 - c) * chunk + rb, blk), blk), lanes] = o
                sb_ref[2 * pair], sb_ref[2 * pair + 1] = states
            else:
                accf_ref[pl.ds(pl.multiple_of(c * chunk + rf, blk), blk), lanes] = o
                sf_ref[2 * pair], sf_ref[2 * pair + 1] = states
        return carry

    lax.fori_loop(0, nblk, body, 0, unroll=True)

    @pl.when(c == nc - 1)
    def _():
        fin = min(seq, HGRN_FINISH_ROWS)

        def finish(t, carry):
            rows = pl.ds(pl.multiple_of(t * fin, fin), fin)
            for h in range(hp):
                lanes = slice(h * dh, (h + 1) * dh)
                o = accf_ref[rows, lanes] + accb_ref[rows, lanes]
                o = o * lax.rsqrt(jnp.mean(o * o, axis=-1, keepdims=True) + RMS_EPS)
                o = o * nw_ref[:, lanes] * g_ref[0, rows, lanes].astype(F32)
                o_ref[0, rows, lanes] = o.astype(o_ref.dtype)
            return carry

        lax.fori_loop(0, seq // fin, finish, 0, unroll=4)


def _hgrn(qg3, plain3, logf3, key3, norm_w):
    batch, seq, _ = qg3.shape
    hp = HGRN_HEADS_PER_STEP
    w = hp * HGRN_HEAD_DIM
    chunk = min(seq, HGRN_CHUNK)
    nc = seq // chunk

    def fwd(offset):
        base = offset // w
        return pl.BlockSpec((1, chunk, w), lambda b, h, c: (b, c, base + h))

    def bwd(offset):
        base = offset // w
        return pl.BlockSpec((1, chunk, w), lambda b, h, c: (b, nc - 1 - c, base + h))

    gbase = HGRN_WIDTH // w
    return pl.pallas_call(
        _hgrn_kernel,
        grid=(batch, HGRN_HEADS // hp, nc),
        in_specs=[fwd(0), fwd(PLAIN_HI), fwd(0), fwd(0),
                  bwd(0), bwd(PLAIN_HI), bwd(HGRN_WIDTH), bwd(HGRN_WIDTH),
                  pl.BlockSpec((1, seq, w), lambda b, h, c: (b, 0, gbase + h)),
                  pl.BlockSpec((1, w), lambda b, h, c: (0, h))],
        out_specs=pl.BlockSpec((1, seq, w), lambda b, h, c: (b, 0, h)),
        out_shape=jax.ShapeDtypeStruct((batch, seq, HGRN_WIDTH), BF16),
        scratch_shapes=[pltpu.VMEM((seq, w), F32), pltpu.VMEM((seq, w), F32),
                        pltpu.VMEM((hp, HGRN_HEAD_DIM, HGRN_HEAD_DIM), F32),
                        pltpu.VMEM((hp, HGRN_HEAD_DIM, HGRN_HEAD_DIM), F32)],
        compiler_params=_params("parallel", "parallel", "arbitrary"),
        name="hgrn",
    )(qg3, plain3, logf3, key3, qg3, plain3, logf3, key3, qg3, norm_w)


def _pool_kernel(prev_ref, cur_ref, next_ref, pw_ref, ps_ref, o_ref, *, seq_len):
    i = pl.program_id(1)
    last = pl.num_programs(1) - 1
    ts = cur_ref.shape[1]
    n = ts + 2 * POOL_HALO
    pos = i * ts + lax.broadcasted_iota(jnp.int32, (ts, 1), 0)
    for g in range(POOL_GROUPS):
        lanes = slice(g * POOL_GROUP_DIM, (g + 1) * POOL_GROUP_DIM)
        half = POOL_WINDOWS[g] // 2
        cur = cur_ref[0, :, lanes].astype(F32)
        prev = jnp.where(i > 0, prev_ref[0, :, lanes].astype(F32), 0.0)
        nxt = jnp.where(i < last, next_ref[0, :, lanes].astype(F32), 0.0)
        ext = jnp.concatenate([prev, cur, nxt], axis=0)
        w = ext + pltpu.roll(ext, 1, 0)
        step = 1
        while step < half:
            w = pltpu.roll(w, step, 0) + pltpu.roll(w, n - step, 0)
            step *= 2
        wsum = w[POOL_HALO:POOL_HALO + ts]
        count = (jnp.minimum(pos + half, seq_len) - jnp.maximum(pos - half, 0)).astype(F32)
        mixed = wsum / count - cur
        y = jnp.dot(mixed.astype(BF16), pw_ref[g], preferred_element_type=F32)
        o_ref[0, :, lanes] = (y * ps_ref[:, lanes]).astype(o_ref.dtype)


def _pool(plain3, pool_w, layer, pool_scale):
    batch, seq, _ = plain3.shape
    ts = min(seq, POOL_ROWS)
    hb = ts // POOL_HALO
    nhalo = seq // POOL_HALO
    kern = functools.partial(_pool_kernel, seq_len=seq)
    return pl.pallas_call(
        kern,
        grid=(batch, seq // ts),
        in_specs=[pl.BlockSpec((1, POOL_HALO, D_MODEL), lambda b, i: (b, jnp.maximum(i * hb - 1, 0), 0)),
                  pl.BlockSpec((1, ts, D_MODEL), lambda b, i: (b, i, 0)),
                  pl.BlockSpec((1, POOL_HALO, D_MODEL), lambda b, i: (b, jnp.minimum((i + 1) * hb, nhalo - 1), 0)),
                  pl.BlockSpec((None, POOL_GROUPS, POOL_GROUP_DIM, POOL_GROUP_DIM), lambda b, i: (layer, 0, 0, 0)),
                  pl.BlockSpec((1, D_MODEL), lambda b, i: (0, 0))],
        out_specs=pl.BlockSpec((1, ts, D_MODEL), lambda b, i: (b, i, 0)),
        out_shape=jax.ShapeDtypeStruct((batch, seq, D_MODEL), BF16),
        compiler_params=_params("parallel", "parallel"),
        name="pool",
    )(plain3, plain3, plain3, pool_w, pool_scale)


def _merge_kernel(ya_ref, o1_ref, o2_ref, o3_ref, s1_ref, s2_ref, s3_ref, hc_ref,
                  xin_ref, x_ref, wa_ref, wb_ref, wc_ref, wo_ref, *rest, alpha, nsub):
    wgate_refs = rest[:-4]
    g_ref, b_ref, xo_ref, xb_ref = rest[-4:]
    d = D_MODEL
    dh = ATTN_HEAD_DIM
    sub = x_ref.shape[0] // nsub
    rows = [slice(s * sub, (s + 1) * sub) for s in range(nsub)]
    xin = [xin_ref[r, :].astype(BF16) for r in rows]
    y_a = [jnp.dot(ya_ref[r, :], wa_ref[...], preferred_element_type=F32) for r in rows]
    y_c = [jnp.dot(hc_ref[r, :], wc_ref[...], preferred_element_type=F32) for r in rows]
    att = []
    for r in rows:
        ms = [s1_ref[r, :], s2_ref[r, :], s3_ref[r, :]]
        ls = [pltpu.roll(m, LANES - ATTN_STAT_LANES // 2, 1) for m in ms]
        top = jnp.maximum(jnp.maximum(ms[0], ms[1]), ms[2])
        es = [jnp.exp2(m - top) for m in ms]
        inv = 1.0 / (es[0] * ls[0] + es[1] * ls[1] + es[2] * ls[2])
        ws = [e * inv for e in es]
        heads = []
        for h in range(ATTN_HEADS_PER_GROUP):
            lanes = slice(h * dh, (h + 1) * dh)
            stat = slice(h * ATTN_STAT_LANES, h * ATTN_STAT_LANES + 1)
            acc = None
            for w, o_ref in zip(ws, (o1_ref, o2_ref, o3_ref)):
                term = jnp.broadcast_to(w[:, stat], (sub, dh)) * o_ref[r, lanes].astype(F32)
                acc = term if acc is None else acc + term
            heads.append(acc)
        att.append(jnp.concatenate(heads, axis=1).astype(BF16))
    y_b = [jnp.dot(a, wb_ref[...], preferred_element_type=F32) for a in att]
    per_gate = d // W_IN_BLOCK
    gates = [[jnp.concatenate([jnp.dot(x, w_ref[...], preferred_element_type=F32)
                               for w_ref in wgate_refs[n * per_gate:(n + 1) * per_gate]], axis=1)
              for n in range(3)] for x in xin]
    merged = [(_sigmoid(gates[s][0]) * y_a[s] + _sigmoid(gates[s][1]) * y_b[s]
               + _sigmoid(gates[s][2]) * y_c[s]).astype(BF16) for s in range(nsub)]
    mix = [jnp.dot(mg, wo_ref[...], preferred_element_type=F32) for mg in merged]
    for s, r in enumerate(rows):
        x1 = _layer_norm(alpha * x_ref[r, :] + mix[s], g_ref[...], b_ref[...])
        xo_ref[r, :] = x1
        xb_ref[r, :] = x1.astype(BF16)


def _merge(y_pool, o_list, stat_list, y_hgrn, x_bf, x, wa, wb, wc, wo, w_in_bf, layer, ln_g, ln_b, alpha):
    T = x.shape[0]
    tm = min(T, MERGE_ROWS)
    gw = ATTN_GROUP_WIDTH
    d = D_MODEL
    row = lambda w: pl.BlockSpec((tm, w), lambda i: (i, 0))
    const = lambda a: pl.BlockSpec(a.shape, lambda i: (0,) * a.ndim, pipeline_mode=pl.Buffered(1))
    kern = functools.partial(_merge_kernel, alpha=alpha, nsub=MERGE_SUBTILES)
    stat = row(ATTN_HEAD_DIM)
    gate_blocks = range(GATES_COLUMN // W_IN_BLOCK, (GATES_COLUMN + 3 * d) // W_IN_BLOCK)
    gate_specs = [pl.BlockSpec((None, d, W_IN_BLOCK), lambda i, blk=blk: (layer, 0, blk),
                               pipeline_mode=pl.Buffered(1)) for blk in gate_blocks]
    return pl.pallas_call(
        kern,
        grid=(T // tm,),
        in_specs=[row(d), row(gw), row(gw), row(gw), stat, stat, stat, row(d), row(d),
                  row(d)] + [_layer_spec(w, layer) for w in (wa, wb, wc, wo)] + gate_specs
                 + [const(ln_g), const(ln_b)],
        out_specs=[row(d), row(d)],
        out_shape=[jax.ShapeDtypeStruct((T, d), F32), jax.ShapeDtypeStruct((T, d), BF16)],
        compiler_params=_params("parallel"),
        name="merge",
    )(y_pool, *o_list, *stat_list, y_hgrn, x_bf, x, wa, wb, wc, wo, *([w_in_bf] * len(gate_specs)), ln_g, ln_b)


def _ffn_kernel(xb_ref, x_ref, p_ref, wg_ref, wu_ref, wd_ref, wpp_ref, wpg_ref, g_ref, b_ref,
                xo_ref, xbo_ref, *, alpha, nsub):
    sub = xb_ref.shape[0] // nsub
    rows = [slice(s * sub, (s + 1) * sub) for s in range(nsub)]
    xb = [xb_ref[r, :] for r in rows]
    gate = [jnp.dot(x, wg_ref[...], preferred_element_type=F32) for x in xb]
    up = [jnp.dot(x, wu_ref[...], preferred_element_type=F32) for x in xb]
    pgate = [jnp.dot(x, wpg_ref[...], preferred_element_type=F32) for x in xb]
    ple = [jnp.dot(p_ref[r, :].astype(BF16), wpp_ref[...], preferred_element_type=F32) for r in rows]
    hidden = [(g * _sigmoid(g) * u).astype(BF16) for g, u in zip(gate, up)]
    ffn = [jnp.dot(h, wd_ref[...], preferred_element_type=F32) for h in hidden]
    for s, r in enumerate(rows):
        x2 = _layer_norm(alpha * x_ref[r, :] + ffn[s] + ple[s] * _sigmoid(pgate[s]), g_ref[...], b_ref[...])
        xo_ref[r, :] = x2
        xbo_ref[r, :] = x2.astype(BF16)


def _ffn(x1_bf, x1, p, wg, wu, wd, wpp, wpg, layer, ln_g, ln_b, alpha):
    T = x1.shape[0]
    tm = min(T, FFN_ROWS)
    d = D_MODEL
    row = lambda w: pl.BlockSpec((tm, w), lambda i: (i, 0))
    const = lambda a: pl.BlockSpec(a.shape, lambda i: (0,) * a.ndim, pipeline_mode=pl.Buffered(1))
    weight = lambda a: _layer_spec(a, layer)
    kern = functools.partial(_ffn_kernel, alpha=alpha, nsub=FFN_SUBTILES)
    return pl.pallas_call(
        kern,
        grid=(T // tm,),
        in_specs=[row(d), row(d), pl.BlockSpec((None, tm, PLE_DIM), lambda i: (layer, i, 0)),
                  weight(wg), weight(wu), weight(wd), weight(wpp), weight(wpg), const(ln_g), const(ln_b)],
        out_specs=[row(d), row(d)],
        out_shape=[jax.ShapeDtypeStruct((T, d), F32), jax.ShapeDtypeStruct((T, d), BF16)],
        compiler_params=_params("parallel"),
        name="ffn",
    )(x1_bf, x1, p, wg, wu, wd, wpp, wpg, ln_g, ln_b)


def kernel(x, p, w_in, pool_w, pool_scale, w_branch_a, w_branch_b, w_branch_c, hgrn_lb_logits, hgrn_norm_w, w_out, ln1_g, ln1_b, w_ffn_gate, w_ffn_up, w_ffn_down, w_ple_proj, w_ple_gate, ln2_g, ln2_b):
    batch, seq, d = x.shape
    depth = w_in.shape[0]
    T = batch * seq
    alpha = float((2 * depth) ** 0.25)

    lb = jnp.cumsum(jax.nn.softmax(hgrn_lb_logits.astype(F32), axis=0), axis=0)
    lb = lb - lb[:1]

    w_in_bf = w_in.astype(BF16)
    bf = lambda a: a.astype(BF16)
    pool_w_b, wa, wb, wc, wo = bf(pool_w), bf(w_branch_a), bf(w_branch_b), bf(w_branch_c), bf(w_out)
    wg, wu, wd, wpp, wpg = bf(w_ffn_gate), bf(w_ffn_up), bf(w_ffn_down), bf(w_ple_proj), bf(w_ple_gate)

    slopes = [2.0 ** (-ALIBI_MAX_BIAS * (i + 1) / ATTN_HEADS) for i in range(ATTN_HEADS)]
    hpg = ATTN_HEADS_PER_GROUP
    gw = ATTN_GROUP_WIDTH

    xf = x.reshape(T, d).astype(F32)
    xb = xf
    p_rows = p.reshape(depth, T, PLE_DIM)
    for i in range(depth):
        qg, logf, key, plain = _inproj(xb, w_in_bf, i, lb[i:i + 1])
        plain3 = plain.reshape(batch, seq, PLAIN_WIDTH)
        o_list, stat_list = [], []
        for g, (_, dil) in enumerate(ATTN_GROUPS):
            gs = tuple(slopes[g * hpg:(g + 1) * hpg])
            if dil == 1:
                qkv = plain.reshape(batch, 1, seq, PLAIN_WIDTH)
                cols = (PLAIN_AQ // gw, PLAIN_AK // gw, PLAIN_AV // gw)
            else:
                blocks = [(ATTN_COLUMN + part * ATTN_HEADS * ATTN_HEAD_DIM) // W_IN_BLOCK + g for part in range(3)]
                qkv = _inproj_dilated(xf, w_in_bf, i, blocks, batch, seq, dil)
                cols = (0, 1, 2)
            o, stats = _attention_group(qkv, qkv, qkv, cols, batch, seq, dil, gs)
            o_list.append(o)
            stat_list.append(stats)
        y_hgrn = _hgrn(qg.reshape(batch, seq, 2 * HGRN_WIDTH), plain3,
                       logf.reshape(batch, seq, 2 * HGRN_WIDTH), key.reshape(batch, seq, 2 * HGRN_WIDTH),
                       hgrn_norm_w[i:i + 1])
        y_pool = _pool(plain3, pool_w_b, i, pool_scale[i:i + 1])
        x1, x1b = _merge(y_pool.reshape(T, d), o_list, stat_list, y_hgrn.reshape(T, d), xb, xf,
                         wa, wb, wc, wo, w_in_bf, i, ln1_g[i:i + 1], ln1_b[i:i + 1], alpha)
        xf, xb = _ffn(x1b, x1, p_rows, wg, wu, wd, wpp, wpg, i, ln2_g[i:i + 1], ln2_b[i:i + 1], alpha)
    return xf.reshape(batch, seq, d).astype(x.dtype)
```
